```python
import math
import jax
import jax.numpy as jnp
from jax import lax
import numpy as np

D_MODEL = 2048
BATCH = 2
SEQ = 4096
DEPTH = 4
DEC_BATCH = 8
DEC_SEQ = 8
PAST_LEN = 16384
PAGE_SIZE = 128

N_GROUPS = 4
GW = D_MODEL // N_GROUPS
HD = 64
SB_H = GW // HD
NSA_H = GW // HD
NSA_KVH = 2
NSA_REP = NSA_H // NSA_KVH
CMP_BLOCK = 64
SEL_BLOCK = CMP_BLOCK
SEL_TOPN = 16
SEL_FORCE = float(NSA_REP + 1)
WINDOW = 512
LRU_W = GW
LRU_BLKS = 8
LRU_BW = LRU_W // LRU_BLKS
LRU_C = 8.0
CONV_W = 4
GDN_H = 4
GDN_DK = GW // GDN_H
GDN_DV = GW // GDN_H
GDN_CHUNK = 64
D_FF = ((8 * D_MODEL // 3 + 255) // 256) * 256
QBLK = 128
ALPHA = (2 * DEPTH) ** 0.25
BETA_INIT = (8 * DEPTH) ** -0.25
LN_EPS = 1e-5
RMS_EPS = 1e-6
_IN_SPLITS = (GW, GW, GW, GW, 6 * NSA_KVH * HD, 3 * NSA_H, LRU_W, LRU_W, 3 * GW, GW, GDN_H, GDN_H)
P_IN = sum(_IN_SPLITS)

kernel_name = "hybrid_sb_nsa_rglru_gdn_decoder_step"


def layer_norm(x, g, b):
    xf = x.astype(jnp.float32)
    mu = jnp.mean(xf, -1, keepdims=True)
    var = jnp.mean(jnp.square(xf - mu), -1, keepdims=True)
    return ((xf - mu) * lax.rsqrt(var + LN_EPS) * g + b).astype(x.dtype)


def rms_norm(x, g):
    xf = x.astype(jnp.float32)
    return (xf * lax.rsqrt(jnp.mean(xf * xf, -1, keepdims=True) + RMS_EPS) * g).astype(x.dtype)


def l2norm(x):
    return x * lax.rsqrt(jnp.sum(x * x, -1, keepdims=True) + RMS_EPS)


def swiglu(x, w_gu, w_down):
    g, u = jnp.split(x @ w_gu, 2, axis=-1)
    return (jax.nn.silu(g) * u) @ w_down


def split_in(h):
    outs, start = [], 0
    for w in _IN_SPLITS:
        outs.append(h[..., start:start + w])
        start += w
    return outs


def causal_dwconv(x, buf, w):
    xp = jnp.concatenate([buf.astype(x.dtype), x], axis=1)
    T = x.shape[1]
    y = sum(xp[:, j:j + T] * w[j] for j in range(CONV_W))
    return y, xp[:, -(CONV_W - 1):]


def masked_softmax(s, mask):
    s = jnp.where(mask, s, -jnp.inf)
    m = jnp.max(s, -1, keepdims=True)
    m = jnp.where(jnp.isfinite(m), m, 0.0)
    e = jnp.exp(s - m)
    return e / jnp.maximum(jnp.sum(e, -1, keepdims=True), 1e-30)


def stick_breaking_block(q, k, v, q_pos, k_pos):
    z = jnp.einsum("bqhd,bkhd->bhqk", q, k).astype(jnp.float32) * (HD ** -0.5)
    mask = k_pos[None, :] < q_pos[:, None]
    log_1mb = jnp.where(mask, jax.nn.log_sigmoid(-z), 0.0)
    tail = lax.cumsum(log_1mb, axis=3, reverse=True) - log_1mb
    w = jnp.exp(jnp.where(mask, jax.nn.log_sigmoid(z) + tail, -jnp.inf))
    return jnp.einsum("bhqk,bkhd->bqhd", w.astype(v.dtype), v)


def sb_attend(q, k, v, q0):
    B, T = q.shape[:2]
    blk = math.gcd(T, QBLK)
    nb = T // blk
    k_pos = jnp.arange(k.shape[1])
    qb = q.reshape(B, nb, blk, SB_H, HD).swapaxes(0, 1)

    def body(args):
        qi, i = args
        return stick_breaking_block(qi, k, v, q0 + i * blk + jnp.arange(blk), k_pos)

    o = lax.map(body, (qb, jnp.arange(nb)))
    return o.swapaxes(0, 1).reshape(B, T, SB_H * HD)


def nsa_compress(kv, phi):
    B, L = kv.shape[:2]
    blocks = kv.reshape(B, L // CMP_BLOCK, CMP_BLOCK, NSA_KVH, HD)
    return jnp.einsum("bnjgd,jde->bnge", blocks, phi)


def nsa_block(q, gates, q_pos, kc, vc, ksT, vsT, kw, vw, kw_pos, w0):
    f32 = jnp.float32
    scale = HD ** -0.5
    B, Tq = q.shape[:2]
    NB = kc.shape[1]
    blk_ids = jnp.arange(NB)
    cmask = ((blk_ids + 1) * CMP_BLOCK - 1)[None, :] <= q_pos[:, None]
    s_c = jnp.einsum("bqgrd,bngd->bqgrn", q, kc).astype(f32) * scale
    p_c = masked_softmax(s_c, cmask[None, :, None, None, :])
    o_c = jnp.einsum("bqgrn,bngd->bqgrd", p_c.astype(vc.dtype), vc)
    cur = q_pos // SEL_BLOCK
    forced = (blk_ids[None, :] == 0) | (blk_ids[None, :] == cur[:, None])
    valid = blk_ids[None, :] <= cur[:, None]
    imp = jnp.sum(p_c, axis=3)
    imp = jnp.where(forced[None, :, None, :], SEL_FORCE, jnp.where(valid[None, :, None, :], imp, -1.0))
    _, sel = lax.top_k(imp, min(SEL_TOPN, NB))
    tok = (sel[..., None] * SEL_BLOCK + jnp.arange(SEL_BLOCK)).reshape(B, Tq, NSA_KVH, -1)
    bi = jnp.arange(B)[:, None, None, None]
    gi = jnp.arange(NSA_KVH)[None, None, :, None]
    k_s = ksT[bi, gi, tok]
    v_s = vsT[bi, gi, tok]
    s_s = jnp.einsum("bqgrd,bqgmd->bqgrm", q, k_s).astype(f32) * scale
    p_s = masked_softmax(s_s, (tok <= q_pos[None, :, None, None])[:, :, :, None, :])
    o_s = jnp.einsum("bqgrm,bqgmd->bqgrd", p_s.astype(v_s.dtype), v_s)
    wmask = ((kw_pos[None, :] <= q_pos[:, None]) & (kw_pos[None, :] > q_pos[:, None] - WINDOW)
             & (kw_pos[None, :] >= w0))
    s_w = jnp.einsum("bqgrd,bkgd->bqgrk", q, kw).astype(f32) * scale
    p_w = masked_softmax(s_w, wmask[None, :, None, None, :])
    o_w = jnp.einsum("bqgrk,bkgd->bqgrd", p_w.astype(vw.dtype), vw)
    return (gates[:, :, 0, :, :, None] * o_c + gates[:, :, 1, :, :, None] * o_s
            + gates[:, :, 2, :, :, None] * o_w)


def nsa_attend(q, gates, q0, kc, vc, ksT, vsT, kw, vw, w0):
    B, T = q.shape[:2]
    blk = math.gcd(T, QBLK)
    nb = T // blk
    pad = ((0, 0), (WINDOW, 0), (0, 0), (0, 0))
    kwp, vwp = jnp.pad(kw, pad), jnp.pad(vw, pad)
    qb = q.reshape((B, nb, blk) + q.shape[2:]).swapaxes(0, 1)
    gb = gates.reshape((B, nb, blk) + gates.shape[2:]).swapaxes(0, 1)

    def body(args):
        qi, gi, i = args
        qs = q0 + i * blk
        q_pos = qs + jnp.arange(blk)
        kwi = lax.dynamic_slice_in_dim(kwp, qs - w0, WINDOW + blk, axis=1)
        vwi = lax.dynamic_slice_in_dim(vwp, qs - w0, WINDOW + blk, axis=1)
        kw_pos = qs - WINDOW + jnp.arange(WINDOW + blk)
        return nsa_block(qi, gi, q_pos, kc, vc, ksT, vsT, kwi, vwi, kw_pos, w0)

    o = lax.map(body, (qb, gb, jnp.arange(nb)))
    return o.swapaxes(0, 1).reshape(B, T, NSA_H * HD)


def rg_lru(u, h0, gate_w, gate_b, lam):
    f32 = jnp.float32
    B, T, C = u.shape
    ub = u.reshape(B, T, LRU_BLKS, LRU_BW)
    gts = jnp.einsum("btnc,knce->kbtne", ub, gate_w).reshape(2, B, T, C) + gate_b[:, None, None, :]
    r = jax.nn.sigmoid(gts[0].astype(f32))
    i = jax.nn.sigmoid(gts[1].astype(f32))
    log_a = -LRU_C * r * jax.nn.softplus(-lam.astype(f32))
    a = jnp.exp(log_a)
    b = jnp.sqrt(-jnp.expm1(2.0 * log_a)) * (i * u.astype(f32))
    b = b.at[:, 0].add(a[:, 0] * h0.astype(f32))
    _, h = lax.associative_scan(lambda c1, c2: (c1[0] * c2[0], c2[0] * c1[1] + c2[1]), (a, b), axis=1)
    return h, h[:, -1]


def gated_delta_chunked(q, k, v, g, beta, S0):
    B, T, H, DK = q.shape
    DV = v.shape[-1]
    C = min(GDN_CHUNK, T)
    pad = (-T) % C
    n = (T + pad) // C
    q, k, v, g, beta = [jnp.pad(t, [(0, 0), (0, pad)] + [(0, 0)] * (t.ndim - 2)) for t in (q, k, v, g, beta)]

    def chunks(t):
        return jnp.moveaxis(t.reshape((B, n, C) + t.shape[2:]), (1, 3), (0, 2))

    qc, kc, vc, gc, bc = [chunks(t) for t in (q, k, v, g, beta)]
    gcum = jnp.cumsum(gc, axis=-1)
    incl = jnp.tril(jnp.ones((C, C), dtype=bool))
    strict = jnp.tril(jnp.ones((C, C), dtype=bool), k=-1)
    decay = jnp.exp(jnp.where(incl, gcum[..., :, None] - gcum[..., None, :], -jnp.inf))
    kb = kc * bc[..., None]
    a_mat = jnp.where(strict, jnp.einsum("nbhid,nbhjd->nbhij", kb, kc) * decay, 0.0)
    rhs = jnp.concatenate([vc * bc[..., None], kb * jnp.exp(gcum)[..., None]], axis=-1)
    sol = lax.linalg.triangular_solve(a_mat + jnp.eye(C, dtype=a_mat.dtype), rhs,
                                      left_side=True, lower=True, unit_diagonal=True)
    u, w = sol[..., :DV], sol[..., DV:]
    qk = jnp.where(incl, jnp.einsum("nbhid,nbhjd->nbhij", qc, kc) * decay, 0.0)
    qg = qc * jnp.exp(gcum)[..., None]
    kd = kc * jnp.exp(gcum[..., -1:] - gcum)[..., None]
    g_last = jnp.exp(gcum[..., -1])

    def step(S, inp):
        qg_i, kd_i, u_i, w_i, qk_i, gl_i = inp
        v_new = u_i - jnp.einsum("bhcd,bhde->bhce", w_i, S)
        o = jnp.einsum("bhcd,bhde->bhce", qg_i, S) + jnp.einsum("bhij,bhje->bhie", qk_i, v_new)
        S = S * gl_i[..., None, None] + jnp.einsum("bhcd,bhce->bhde", kd_i, v_new)
        return S, o

    S, o = lax.scan(step, S0, (qg, kd, u, w, qk, g_last))
    o = jnp.moveaxis(o, (0, 2), (1, 3)).reshape(B, n * C, H, DV)[:, :T]
    return o, S


def gated_deltanet(qkv_raw, z, b_raw, a_raw, buf, S0, conv_w, A_log, dt_bias, norm_g):
    f32 = jnp.float32
    B, T, _ = qkv_raw.shape
    qkv, new_buf = causal_dwconv(qkv_raw, buf, conv_w)
    qkv = jax.nn.silu(qkv.astype(f32)).reshape(B, T, 3, GDN_H, GDN_DK)
    q = l2norm(qkv[:, :, 0]) * (GDN_DK ** -0.5)
    k = l2norm(qkv[:, :, 1])
    v = qkv[:, :, 2]
    beta = jax.nn.sigmoid(b_raw.astype(f32))
    g = -jnp.exp(A_log.astype(f32)) * jax.nn.softplus(a_raw.astype(f32) + dt_bias.astype(f32))
    o, S = gated_delta_chunked(q, k, v, g, beta, S0.astype(f32))
    o = rms_norm(o, norm_g) * jax.nn.silu(z.astype(f32).reshape(B, T, GDN_H, GDN_DV))
    return o.reshape(B, T, GW).astype(z.dtype), S, new_buf


def trunk_layer(x, P, past, lw):
    past_sb, past_nsa, past_win, h0, lru_buf, S0, gdn_buf = past
    B, T, _ = x.shape
    x = layer_norm(ALPHA * x + 0.5 * swiglu(x, lw["ffn_gu"][0], lw["ffn_down"][0]), lw["ln_g"][0], lw["ln_b"][0])
    (sb_q, sb_k, sb_v, nsa_q, nsa_kv, nsa_gate, lru_x, lru_gate,
     gdn_qkv, gdn_z, gdn_b, gdn_a) = split_in(x @ lw["w_in"])

    sb_rows = jnp.stack([sb_k.reshape(B, T, SB_H, HD), sb_v.reshape(B, T, SB_H, HD)], axis=2)
    sb_all = jnp.concatenate([past_sb.astype(sb_rows.dtype), sb_rows], axis=1)
    o_a = sb_attend(sb_q.reshape(B, T, SB_H, HD), sb_all[:, :, 0], sb_all[:, :, 1], P)

    nsa_kv = nsa_kv.reshape(B, T, 6, NSA_KVH, HD)
    nsa_rows = nsa_kv[:, :, :4]
    full = jnp.concatenate([past_nsa.astype(nsa_rows.dtype), nsa_rows], axis=1)
    full = jnp.pad(full, ((0, 0), (0, (-(P + T)) % SEL_BLOCK), (0, 0), (0, 0), (0, 0)))
    kc = nsa_compress(full[:, :, 0], lw["nsa_phi"][0])
    vc = nsa_compress(full[:, :, 1], lw["nsa_phi"][1])
    ksT = full[:, :, 2].transpose(0, 2, 1, 3)
    vsT = full[:, :, 3].transpose(0, 2, 1, 3)
    Lw = past_win.shape[1]
    win_all = jnp.concatenate([past_win.astype(nsa_kv.dtype), nsa_kv[:, :, 4:]], axis=1)
    gates = jax.nn.sigmoid(nsa_gate.astype(jnp.float32)).reshape(B, T, 3, NSA_KVH, NSA_REP).astype(x.dtype)
    o_b = nsa_attend(nsa_q.reshape(B, T, NSA_KVH, NSA_REP, HD), gates, P, kc, vc, ksT, vsT,
                     win_all[:, :, 0], win_all[:, :, 1], P - Lw)
    new_win = win_all[:, -min(WINDOW, Lw + T):]

    u, new_lru_buf = causal_dwconv(lru_x, lru_buf, lw["lru_conv_w"])
    h_seq, h_last = rg_lru(u + lw["lru_conv_b"], h0, lw["lru_gate_w"], lw["lru_gate_b"], lw["lru_lambda"])
    o_c = h_seq.astype(x.dtype) * jax.nn.gelu(lru_gate)

    o_d, S_new, new_gdn_buf = gated_deltanet(gdn_qkv, gdn_z, gdn_b, gdn_a, gdn_buf, S0, lw["gdn_conv_w"],
                                             lw["gdn_A_log"], lw["gdn_dt_bias"], lw["gdn_norm_g"])

    gg = lw["grp_norm_g"]
    mix = jnp.concatenate([rms_norm(o_a, gg[0]), rms_norm(o_b, gg[1]), rms_norm(o_c, gg[2]), o_d], axis=-1)
    x = layer_norm(ALPHA * x + mix @ lw["w_out"], lw["ln_g"][1], lw["ln_b"][1])
    x = layer_norm(ALPHA * x + 0.5 * swiglu(x, lw["ffn_gu"][1], lw["ffn_down"][1]), lw["ln_g"][2], lw["ln_b"][2])
    return x, (sb_rows, nsa_rows, new_win, h_last, new_lru_buf, S_new, new_gdn_buf)


def setup_inputs(seed: int = 0) -> dict:
    key = jax.random.key(seed)
    ks = jax.random.split(key, 32)
    nrm = jax.random.normal
    n_pages = PAST_LEN // PAGE_SIZE
    n_used = DEC_BATCH * n_pages
    n_pool = n_used + (n_used + 3) // 4
    win_len = min(WINDOW, PAST_LEN)
    page_table = jax.random.permutation(ks[0], n_pool)[:n_used].reshape(DEC_BATCH, n_pages).astype(jnp.int32)
    lam_a = jax.random.uniform(ks[20], (DEPTH, LRU_W), minval=0.9, maxval=0.999) ** (1.0 / LRU_C)
    dt0 = jnp.exp(jax.random.uniform(ks[24], (DEPTH, GDN_H), minval=math.log(1e-3), maxval=math.log(1e-1)))
    return {
        "x_prompt": nrm(ks[1], (BATCH, SEQ, D_MODEL)),
        "x_sample": nrm(ks[2], (DEC_BATCH, DEC_SEQ, D_MODEL)),
        "cache_sb_kv": nrm(ks[3], (DEPTH, n_pool, PAGE_SIZE, 2, SB_H, HD)),
        "cache_nsa_kv": nrm(ks[4], (DEPTH, n_pool, PAGE_SIZE, 4, NSA_KVH, HD)),
        "cache_nsa_win": nrm(ks[5], (DEPTH, DEC_BATCH, win_len, 2, NSA_KVH, HD)),
        "state_lru_h": 0.5 * nrm(ks[6], (DEPTH, DEC_BATCH, LRU_W)),
        "state_lru_conv": nrm(ks[7], (DEPTH, DEC_BATCH, CONV_W - 1, LRU_W)),
        "state_gdn_S": nrm(ks[8], (DEPTH, DEC_BATCH, GDN_H, GDN_DK, GDN_DV)) * (GDN_DK ** -0.5),
        "state_gdn_conv": nrm(ks[9], (DEPTH, DEC_BATCH, CONV_W - 1, 3 * GW)),
        "page_table": page_table,
        "ln_g": 1.0 + 0.02 * nrm(ks[10], (DEPTH, 3, D_MODEL)),
        "ln_b": 0.02 * nrm(ks[11], (DEPTH, 3, D_MODEL)),
        "ffn_gu": nrm(ks[12], (DEPTH, 2, D_MODEL, 2 * D_FF)) * (D_MODEL ** -0.5),
        "ffn_down": nrm(ks[13], (DEPTH, 2, D_FF, D_MODEL)) * (BETA_INIT * D_FF ** -0.5),
        "w_in": nrm(ks[14], (DEPTH, D_MODEL, P_IN)) * (D_MODEL ** -0.5),
        "w_out": nrm(ks[15], (DEPTH, N_GROUPS * GW, D_MODEL)) * (BETA_INIT * (N_GROUPS * GW) ** -0.5),
        "grp_norm_g": 1.0 + 0.02 * nrm(ks[16], (DEPTH, 3, GW)),
        "nsa_phi": nrm(ks[17], (DEPTH, 2, CMP_BLOCK, HD, HD)) * ((CMP_BLOCK * HD) ** -0.5),
        "lru_conv_w": nrm(ks[18], (DEPTH, CONV_W, LRU_W)) * (CONV_W ** -0.5),
        "lru_conv_b": 0.02 * nrm(ks[19], (DEPTH, LRU_W)),
        "lru_gate_w": nrm(ks[21], (DEPTH, 2, LRU_BLKS, LRU_BW, LRU_BW)) * (LRU_BW ** -0.5),
        "lru_gate_b": 0.02 * nrm(ks[22], (DEPTH, 2, LRU_W)),
        "lru_lambda": jnp.log(lam_a) - jnp.log1p(-lam_a),
        "gdn_conv_w": nrm(ks[23], (DEPTH, CONV_W, 3 * GW)) * (CONV_W ** -0.5),
        "gdn_A_log": jnp.log(jax.random.uniform(ks[25], (DEPTH, GDN_H), minval=1.0, maxval=16.0)),
        "gdn_dt_bias": dt0 + jnp.log(-jnp.expm1(-dt0)),
        "gdn_norm_g": 1.0 + 0.02 * nrm(ks[26], (DEPTH, GDN_DV)),
    }


def reference(x_prompt, x_sample, cache_sb_kv, cache_nsa_kv, cache_nsa_win, state_lru_h, state_lru_conv,
              state_gdn_S, state_gdn_conv, page_table, ln_g, ln_b, ffn_gu, ffn_down, w_in, w_out, grp_norm_g,
              nsa_phi, lru_conv_w, lru_conv_b, lru_gate_w, lru_gate_b, lru_lambda, gdn_conv_w, gdn_A_log,
              gdn_dt_bias, gdn_norm_g):
    past_len = page_table.shape[1] * PAGE_SIZE
    n_b, n_db = x_prompt.shape[0], x_sample.shape[0]
    dt = x_prompt.dtype
    f32 = jnp.float32
    y_p, y_s = x_prompt, x_sample
    st_p, st_s = [], []
    for l in range(DEPTH):
        lw = {"ln_g": ln_g[l], "ln_b": ln_b[l], "ffn_gu": ffn_gu[l], "ffn_down": ffn_down[l],
              "w_in": w_in[l], "w_out": w_out[l], "grp_norm_g": grp_norm_g[l], "nsa_phi": nsa_phi[l],
              "lru_conv_w": lru_conv_w[l], "lru_conv_b": lru_conv_b[l], "lru_gate_w": lru_gate_w[l],
              "lru_gate_b": lru_gate_b[l], "lru_lambda": lru_lambda[l], "gdn_conv_w": gdn_conv_w[l],
              "gdn_A_log": gdn_A_log[l], "gdn_dt_bias": gdn_dt_bias[l], "gdn_norm_g": gdn_norm_g[l]}
        past_p = (jnp.zeros((n_b, 0, 2, SB_H, HD), dt), jnp.zeros((n_b, 0, 4, NSA_KVH, HD), dt),
                  jnp.zeros((n_b, 0, 2, NSA_KVH, HD), dt), jnp.zeros((n_b, LRU_W), f32),
                  jnp.zeros((n_b, CONV_W - 1, LRU_W), dt), jnp.zeros((n_b, GDN_H, GDN_DK, GDN_DV), f32),
                  jnp.zeros((n_b, CONV_W - 1, 3 * GW), dt))
        past_s = (cache_sb_kv[l, page_table].reshape((n_db, past_len) + cache_sb_kv.shape[3:]),
                  cache_nsa_kv[l, page_table].reshape((n_db, past_len) + cache_nsa_kv.shape[3:]),
                  cache_nsa_win[l], state_lru_h[l], state_lru_conv[l], state_gdn_S[l], state_gdn_conv[l])
        y_p, new_p = trunk_layer(y_p, 0, past_p, lw)
        y_s, new_s = trunk_layer(y_s, past_len, past_s, lw)
        st_p.append(new_p)
        st_s.append(new_s)
    p = [jnp.stack(a) for a in zip(*st_p)]
    s = [jnp.stack(a) for a in zip(*st_s)]
    return (y_p, y_s, p[0], p[1], p[2], p[3], p[4], p[5], p[6], s[0], s[1], s[2], s[3], s[4], s[5], s[6])
```

```python
import functools
import math

import numpy as np
import jax
import jax.numpy as jnp
from jax import lax
from jax.experimental import pallas as pl
from jax.experimental.pallas import tpu as pltpu

F32 = jnp.float32
BF16 = jnp.bfloat16
HIGHEST = lax.Precision.HIGHEST

PAGE = 128
GW = 512
HD = 64
SB_H = 8
NSA_G = 2
NSA_R = 4
CMP_BLOCK = 64
SEL_TOPN = 16
SEL_FORCE = float(NSA_R + 1)
WINDOW = 512
LRU_BLKS = 8
LRU_C = 8.0
CONV_W = 4
GDN_H = 4
GDN_D = 128
GDN_CHUNK = 64
LN_EPS = 1e-5
RMS_EPS = 1e-6
NEG = -1e30

LANES = 128
SUBLANES = 8
VMEM_LIMIT = 56 * 1024 * 1024

C_SBK, C_SBV, C_SBQ, C_NSAQ, C_NSAKV = 0, 512, 1024, 1536, 2048
C_LRUX, C_LRUG, C_GZ, C_NSAW, C_SMALL, C_GQKV = 2560, 3072, 3584, 4096, 4352, 4608
P_IN_PAD = 6144
SM_GATE, SM_BETA, SM_DECAY = 0, 24, 28
CONV_PAD = 8


def _pick(n, pref, mult=SUBLANES):
    if n <= pref:
        return n
    for t in range(pref, 0, -1):
        if n % t == 0 and t % mult == 0:
            return t
    return n


def _cparams(sem):
    return pltpu.CompilerParams(dimension_semantics=sem, vmem_limit_bytes=VMEM_LIMIT)


def _sigmoid(x):
    return 1.0 / (1.0 + jnp.exp(-x))


def _silu(x):
    return x * _sigmoid(x)


def _softplus(x):
    return jnp.maximum(x, 0.0) + jnp.log1p(jnp.exp(-jnp.abs(x)))


def _log_sigmoid(x):
    return jnp.minimum(x, 0.0) - jnp.log1p(jnp.exp(-jnp.abs(x)))


def _gelu_tanh(x):
    return 0.5 * x * (1.0 + jnp.tanh(math.sqrt(2.0 / math.pi) * (x + 0.044715 * (x * x * x))))


def _layer_norm(y, g, b):
    mu = jnp.mean(y, axis=-1, keepdims=True)
    d = y - mu
    var = jnp.mean(d * d, axis=-1, keepdims=True)
    return d * lax.rsqrt(var + LN_EPS) * g + b


def _dot(a, b):
    return jnp.dot(a, b, preferred_element_type=F32)


def _dot_nt(a, b):
    return lax.dot_general(a, b, (((1,), (1,)), ((), ())), preferred_element_type=F32)


def _dot_tn(a, b):
    return lax.dot_general(a, b, (((0,), (0,)), ((), ())), preferred_element_type=F32)


def _dot_hp(a, b):
    return jnp.dot(a, b, preferred_element_type=F32, precision=HIGHEST)


def _ffn_ln_kernel(x_ref, wg_ref, wu_ref, wd_ref, g_ref, b_ref, o_ref, xb_ref, acc_ref, *, alpha):
    j = pl.program_id(1)

    @pl.when(j == 0)
    def _():
        xb_ref[...] = x_ref[...].astype(BF16)
        acc_ref[...] = jnp.zeros_like(acc_ref)

    xb = xb_ref[...]
    gate = _dot(xb, wg_ref[...])
    up = _dot(xb, wu_ref[...])
    act = (_silu(gate) * up).astype(BF16)
    acc_ref[...] += _dot(act, wd_ref[...])

    @pl.when(j == pl.num_programs(1) - 1)
    def _():
        y = alpha * x_ref[...] + 0.5 * acc_ref[...]
        o_ref[...] = _layer_norm(y, g_ref[...], b_ref[...])


def _ffn_ln(x, wgu, wd, ln_g, ln_b, *, layer, which, ln_idx, alpha):
    n, d = x.shape
    f = wd.shape[2]
    tm = _pick(n, 512)
    tf = _pick(f, 512, LANES)
    nf = f // tf
    return pl.pallas_call(
        functools.partial(_ffn_ln_kernel, alpha=alpha),
        grid=(n // tm, nf),
        in_specs=[
            pl.BlockSpec((tm, d), lambda i, j: (i, 0)),
            pl.BlockSpec((None, None, d, tf), lambda i, j: (layer, which, 0, j)),
            pl.BlockSpec((None, None, d, tf), lambda i, j: (layer, which, 0, j + nf)),
            pl.BlockSpec((None, None, tf, d), lambda i, j: (layer, which, j, 0)),
            pl.BlockSpec((None, None, 1, d), lambda i, j: (layer, ln_idx, 0, 0)),
            pl.BlockSpec((None, None, 1, d), lambda i, j: (layer, ln_idx, 0, 0)),
        ],
        out_specs=pl.BlockSpec((tm, d), lambda i, j: (i, 0)),
        out_shape=jax.ShapeDtypeStruct((n, d), F32),
        scratch_shapes=[pltpu.VMEM((tm, d), BF16), pltpu.VMEM((tm, d), F32)],
        compiler_params=_cparams(("parallel", "arbitrary")),
        name="ffn_ln",
    )(x, wgu, wgu, wd, ln_g, ln_b)


def _matmul_kernel(x_ref, w_ref, o_ref):
    k = pl.program_id(2)

    @pl.when(k == 0)
    def _():
        o_ref[...] = jnp.zeros_like(o_ref)

    o_ref[...] += _dot(x_ref[...].astype(BF16), w_ref[...])


def _matmul(x, w, *, layer, tm_pref=1024, tn_pref=512, tk_pref=2048, name="matmul"):
    m, kdim = x.shape
    nout = w.shape[2]
    tm = _pick(m, tm_pref)
    tn = _pick(nout, tn_pref, LANES)
    tk = _pick(kdim, tk_pref, LANES)
    return pl.pallas_call(
        _matmul_kernel,
        grid=(m // tm, nout // tn, kdim // tk),
        in_specs=[
            pl.BlockSpec((tm, tk), lambda i, j, k: (i, k)),
            pl.BlockSpec((None, tk, tn), lambda i, j, k: (layer, k, j)),
        ],
        out_specs=pl.BlockSpec((tm, tn), lambda i, j, k: (i, j)),
        out_shape=jax.ShapeDtypeStruct((m, nout), F32),
        compiler_params=_cparams(("parallel", "parallel", "arbitrary")),
        name=name,
    )(x, w)


def _gather_kernel(pt_ref, *refs, n_group, t_new, col_splits):
    del pt_ref
    page_refs = refs[:n_group]
    new_ref = refs[n_group]
    out_refs = refs[n_group + 1:]
    s = pl.program_id(1)
    last = pl.num_programs(1) - 1

    @pl.when(s < last)
    def _():
        for g in range(n_group):
            for o_ref, (c0, c1) in zip(out_refs, col_splits):
                o_ref[g * PAGE:(g + 1) * PAGE, :] = page_refs[g][:, c0:c1]

    @pl.when(s == last)
    def _():
        for o_ref, (c0, c1) in zip(out_refs, col_splits):
            o_ref[...] = jnp.zeros_like(o_ref)
            o_ref[0:t_new, :] = new_ref[:, c0:c1]


def _gather_pages(cache, page_table, new_arr, new_colblk, *, layer, col_splits):
    b, n_pages = page_table.shape
    w = cache.shape[-1]
    t_new = new_arr.shape[1]
    n_group = math.gcd(n_pages, 8)
    n_steps = n_pages // n_group + 1
    rows = n_group * PAGE

    def page_map(g):
        return lambda bi, s, pt: (layer, pt[bi, jnp.minimum(s * n_group + g, n_pages - 1)], 0, 0)

    in_specs = [pl.BlockSpec((None, None, PAGE, w), page_map(g)) for g in range(n_group)]
    in_specs.append(pl.BlockSpec((None, t_new, w), lambda bi, s, pt: (bi, 0, new_colblk)))
    out_specs = [pl.BlockSpec((None, rows, c1 - c0), lambda bi, s, pt: (bi, s, 0)) for c0, c1 in col_splits]
    out_shape = [jax.ShapeDtypeStruct((b, n_steps * rows, c1 - c0), F32) for c0, c1 in col_splits]
    return pl.pallas_call(
        functools.partial(_gather_kernel, n_group=n_group, t_new=t_new, col_splits=col_splits),
        grid_spec=pltpu.PrefetchScalarGridSpec(
            num_scalar_prefetch=1, grid=(b, n_steps), in_specs=in_specs, out_specs=out_specs),
        out_shape=out_shape,
        compiler_params=_cparams(("parallel", "arbitrary")),
        name="gather_pages",
    )(page_table, *([cache] * n_group), new_arr)


def _sb_kernel(q_ref, k_ref, v_ref, o_ref, acc_ref, c_ref, *, tq, tk, q0):
    i = pl.program_id(1)
    j = pl.program_id(2)
    jt = (q0 + (i + 1) * tq - 2) // tk - j

    @pl.when(j == 0)
    def _():
        acc_ref[...] = jnp.zeros_like(acc_ref)
        c_ref[...] = jnp.zeros_like(c_ref)

    @pl.when(jt >= 0)
    def _():
        qpos = q0 + i * tq + lax.broadcasted_iota(jnp.int32, (tq, tk), 0)
        kpos = jt * tk + lax.broadcasted_iota(jnp.int32, (tq, tk), 1)
        mask = kpos < qpos
        tri = (lax.broadcasted_iota(jnp.int32, (tk, tk), 0) > lax.broadcasted_iota(jnp.int32, (tk, tk), 1)
               ).astype(BF16)
        lane_half = lax.broadcasted_iota(jnp.int32, (tq, LANES), 1) // HD
        for h in range(SB_H):
            c, half = divmod(h, 2)
            cols = slice(c * LANES, (c + 1) * LANES)
            qc = jnp.where(lane_half == half, q_ref[:, cols] * (HD ** -0.5), 0.0).astype(BF16)
            z = _dot_nt(qc, k_ref[:, cols].astype(BF16))
            ls = _log_sigmoid(z)
            l1m = jnp.where(mask, ls - z, 0.0)
            hi = l1m.astype(BF16)
            lo = (l1m - hi.astype(F32)).astype(BF16)
            tail = _dot(hi, tri) + _dot(lo, tri) + c_ref[h]
            w = jnp.where(mask, jnp.exp(ls + tail), 0.0)
            pv = _dot(w.astype(BF16), v_ref[:, cols].astype(BF16))
            acc_ref[:, cols] += jnp.where(lane_half == half, pv, 0.0)
            c_ref[h] += jnp.sum(l1m, axis=1, keepdims=True)

    @pl.when(j == pl.num_programs(2) - 1)
    def _():
        o_ref[...] = acc_ref[...]


def _sb_attend(q_arr, q_blk, k_arr, k_blk, v_arr, v_blk, *, q0):
    b, t = q_arr.shape[:2]
    l = k_arr.shape[1]
    tq = _pick(t, 256)
    tk = _pick(l, 256, LANES)
    nq = t // tq
    nk = (q0 + t - 2) // tk + 1

    def kv_map(blk):
        return lambda bi, i, j: (bi, jnp.maximum((q0 + (i + 1) * tq - 2) // tk - j, 0), blk)

    return pl.pallas_call(
        functools.partial(_sb_kernel, tq=tq, tk=tk, q0=q0),
        grid=(b, nq, nk),
        in_specs=[
            pl.BlockSpec((None, tq, GW), lambda bi, i, j: (bi, i, q_blk)),
            pl.BlockSpec((None, tk, GW), kv_map(k_blk)),
            pl.BlockSpec((None, tk, GW), kv_map(v_blk)),
        ],
        out_specs=pl.BlockSpec((None, tq, GW), lambda bi, i, j: (bi, i, 0)),
        out_shape=jax.ShapeDtypeStruct((b, t, GW), F32),
        scratch_shapes=[pltpu.VMEM((tq, GW), F32), pltpu.VMEM((SB_H, tq, 1), F32)],
        compiler_params=_cparams(("parallel", "parallel", "arbitrary")),
        name="sb_attend",
    )(q_arr, k_arr, v_arr)


def _masked_softmax(s, mask):
    sm = jnp.where(mask, s, NEG)
    m = jnp.max(sm, axis=-1, keepdims=True)
    e = jnp.where(mask, jnp.exp(sm - m), 0.0)
    return e / jnp.maximum(jnp.sum(e, axis=-1, keepdims=True), 1e-30)


def _nsa_kernel(q_ref, sm_ref, kc_ref, ks_ref, win_ref, o_ref,
                qs_ref, sel_ref, part_ref, m_ref, l_ref, acc_ref,
                *, tq, tk, q0, lw, nb, nbp, n_sel, wl):
    i = pl.program_id(1)
    j = pl.program_id(2)
    rq = NSA_R * tq
    qs0 = q0 + i * tq
    j_last = (qs0 + tq - 1) // tk
    scale = HD ** -0.5

    @pl.when(j == 0)
    def _():
        lane_half = lax.broadcasted_iota(jnp.int32, (tq, LANES), 1) // HD
        gates = _sigmoid(sm_ref[...])
        qpos_b = qs0 + lax.broadcasted_iota(jnp.int32, (tq, nbp), 0)
        blk = lax.broadcasted_iota(jnp.int32, (tq, nbp), 1)
        cmask = ((blk + 1) * CMP_BLOCK - 1 <= qpos_b) & (blk < nb)
        cur = qpos_b // CMP_BLOCK
        forced = (blk == 0) | (blk == cur)
        valid = blk <= cur
        start = pl.multiple_of(jnp.maximum(lw + (i + 1) * tq - wl, 0), SUBLANES)
        qpos_w = qs0 + lax.broadcasted_iota(jnp.int32, (tq, wl), 0)
        kpos_w = q0 - lw + start + lax.broadcasted_iota(jnp.int32, (tq, wl), 1)
        wmask = (kpos_w <= qpos_w) & (kpos_w > qpos_w - WINDOW)
        kw = win_ref[pl.ds(start, wl), 0:LANES].astype(BF16)
        vw = win_ref[pl.ds(start, wl), LANES:2 * LANES].astype(BF16)
        kc = kc_ref[:, 0:LANES].astype(BF16)
        vc = kc_ref[:, LANES:2 * LANES].astype(BF16)
        for g in range(NSA_G):
            for r in range(NSA_R):
                chunk, half = 2 * g + r // 2, r % 2
                qc = q_ref[:, chunk * LANES:(chunk + 1) * LANES]
                if half != g:
                    qc = pltpu.roll(qc, HD, 1)
                qs_ref[g, r * tq:(r + 1) * tq, :] = jnp.where(lane_half == g, qc * scale, 0.0)
            qs = qs_ref[g].astype(BF16)
            s_c = _dot_nt(qs, kc).reshape(NSA_R, tq, nbp)
            p_c = _masked_softmax(s_c, cmask[None])
            o_c = _dot(p_c.reshape(rq, nbp).astype(BF16), vc)
            imp = jnp.sum(p_c, axis=0)
            imp = jnp.where(forced, SEL_FORCE, jnp.where(valid, imp, -1.0))
            imp = jnp.where(blk < nb, imp, -2.0)
            rank = jnp.zeros((tq, nbp), F32)
            for mblk in range(nb):
                col = imp[:, mblk:mblk + 1]
                beats = (col > imp) | ((col == imp) & (blk > mblk))
                rank = rank + jnp.where(beats, 1.0, 0.0)
            sel_ref[g] = jnp.where(rank < n_sel, 1.0, 0.0)
            s_w = _dot_nt(qs, kw).reshape(NSA_R, tq, wl)
            p_w = _masked_softmax(s_w, wmask[None])
            o_w = _dot(p_w.reshape(rq, wl).astype(BF16), vw)
            for r in range(NSA_R):
                rows = slice(r * tq, (r + 1) * tq)
                lane_c = SM_GATE + 0 * NSA_G * NSA_R + g * NSA_R + r
                lane_w = SM_GATE + 2 * NSA_G * NSA_R + g * NSA_R + r
                part_ref[g, rows, :] = (gates[:, lane_c:lane_c + 1] * o_c[rows]
                                        + gates[:, lane_w:lane_w + 1] * o_w[rows])
        m_ref[...] = jnp.full_like(m_ref, NEG)
        l_ref[...] = jnp.zeros_like(l_ref)
        acc_ref[...] = jnp.zeros_like(acc_ref)

    @pl.when(j <= j_last)
    def _():
        ks = ks_ref[:, 0:LANES].astype(BF16)
        vs = ks_ref[:, LANES:2 * LANES].astype(BF16)
        kblk = (j * tk + lax.broadcasted_iota(jnp.int32, (nbp, tk), 1)) // CMP_BLOCK
        expand = (lax.broadcasted_iota(jnp.int32, (nbp, tk), 0) == kblk).astype(BF16)
        qpos = qs0 + lax.broadcasted_iota(jnp.int32, (tq, tk), 0)
        kpos = j * tk + lax.broadcasted_iota(jnp.int32, (tq, tk), 1)
        causal = kpos <= qpos
        for g in range(NSA_G):
            mask = ((_dot(sel_ref[g].astype(BF16), expand) > 0.5) & causal)[None]
            s = _dot_nt(qs_ref[g].astype(BF16), ks).reshape(NSA_R, tq, tk)
            sm = jnp.where(mask, s, NEG)
            m_old = m_ref[g].reshape(NSA_R, tq, 1)
            m_new = jnp.maximum(m_old, jnp.max(sm, axis=-1, keepdims=True))
            p = jnp.where(mask, jnp.exp(sm - m_new), 0.0)
            corr = jnp.exp(m_old - m_new).reshape(rq, 1)
            l_ref[g] = corr * l_ref[g] + jnp.sum(p, axis=-1, keepdims=True).reshape(rq, 1)
            acc_ref[g] = corr * acc_ref[g] + _dot(p.reshape(rq, tk).astype(BF16), vs)
            m_ref[g] = m_new.reshape(rq, 1)

    @pl.when(j == pl.num_programs(2) - 1)
    def _():
        gates = _sigmoid(sm_ref[...])
        lane_half = lax.broadcasted_iota(jnp.int32, (tq, LANES), 1) // HD
        for g in range(NSA_G):
            o_s = acc_ref[g] / jnp.maximum(l_ref[g], 1e-30)
            res = []
            for r in range(NSA_R):
                rows = slice(r * tq, (r + 1) * tq)
                lane_s = SM_GATE + 1 * NSA_G * NSA_R + g * NSA_R + r
                res.append(part_ref[g, rows, :] + gates[:, lane_s:lane_s + 1] * o_s[rows])
            for c2 in range(2):
                even, odd = res[2 * c2], res[2 * c2 + 1]
                if g == 1:
                    even = pltpu.roll(even, HD, 1)
                else:
                    odd = pltpu.roll(odd, HD, 1)
                chunk = 2 * g + c2
                o_ref[:, chunk * LANES:(chunk + 1) * LANES] = jnp.where(lane_half == 0, even, odd)


def _nsa_attend(q_arr, q_blk, sm_arr, sm_blk, kc_arr, ks_arr, ks_blk, win_arr, *, q0, lw, n_real_keys):
    b, t = q_arr.shape[:2]
    l = ks_arr.shape[1]
    nbp = kc_arr.shape[1]
    nb = -(-n_real_keys // CMP_BLOCK)
    n_sel = min(SEL_TOPN, nb)
    tq = _pick(t, 128)
    tk = _pick(l, 1024, LANES)
    nq = t // tq
    nk = (q0 + t - 1) // tk + 1
    wl = min(WINDOW + tq, lw + t)
    rq = NSA_R * tq

    def ks_map(bi, i, j):
        return (bi, jnp.minimum(j, (q0 + (i + 1) * tq - 1) // tk), ks_blk)

    return pl.pallas_call(
        functools.partial(_nsa_kernel, tq=tq, tk=tk, q0=q0, lw=lw, nb=nb, nbp=nbp, n_sel=n_sel, wl=wl),
        grid=(b, nq, nk),
        in_specs=[
            pl.BlockSpec((None, tq, GW), lambda bi, i, j: (bi, i, q_blk)),
            pl.BlockSpec((None, tq, LANES), lambda bi, i, j: (bi, i, sm_blk)),
            pl.BlockSpec((None, nbp, 2 * LANES), lambda bi, i, j: (bi, 0, 0)),
            pl.BlockSpec((None, tk, 2 * LANES), ks_map),
            pl.BlockSpec((None, lw + t, 2 * LANES), lambda bi, i, j: (bi, 0, 0)),
        ],
        out_specs=pl.BlockSpec((None, tq, GW), lambda bi, i, j: (bi, i, 0)),
        out_shape=jax.ShapeDtypeStruct((b, t, GW), F32),
        scratch_shapes=[
            pltpu.VMEM((NSA_G, rq, LANES), F32),
            pltpu.VMEM((NSA_G, tq, nbp), F32),
            pltpu.VMEM((NSA_G, rq, LANES), F32),
            pltpu.VMEM((NSA_G, rq, 1), F32),
            pltpu.VMEM((NSA_G, rq, 1), F32),
            pltpu.VMEM((NSA_G, rq, LANES), F32),
        ],
        compiler_params=_cparams(("parallel", "parallel", "arbitrary")),
        name="nsa_attend",
    )(q_arr, sm_arr, kc_arr, ks_arr, win_arr)


def _lru_kernel(x_ref, gt_ref, buf_ref, h0_ref, cw_ref, cb_ref, wbd_ref, gb_ref, lam_ref,
                o_ref, hl_ref, nb_ref, xp_ref, a_ref, b_ref, h_ref, *, tt):
    s = pl.program_id(1)
    hist = CONV_PAD - (CONV_W - 1)

    @pl.when(s == 0)
    def _():
        xp_ref[hist:CONV_PAD, :] = buf_ref[...]
        h_ref[...] = h0_ref[...]

    xp_ref[CONV_PAD:CONV_PAD + tt, :] = x_ref[...]
    u = cb_ref[...]
    for jw in range(CONV_W):
        u = u + cw_ref[jw:jw + 1, :] * xp_ref[hist + jw:hist + jw + tt, :]
    gts = _dot(u.astype(BF16), wbd_ref[...]) + gb_ref[...]
    r = _sigmoid(gts[:, :GW])
    ig = _sigmoid(gts[:, GW:])
    log_a = -LRU_C * r * _softplus(-lam_ref[...])
    a = jnp.exp(log_a)
    a_ref[...] = a
    b_ref[...] = jnp.sqrt(-jnp.tanh(log_a) * (a * a + 1.0)) * (ig * u)

    def step(t, h):
        h = a_ref[pl.ds(t, 1), :] * h + b_ref[pl.ds(t, 1), :]
        b_ref[pl.ds(t, 1), :] = h
        return h

    h = lax.fori_loop(0, tt, step, h_ref[...], unroll=8)
    h_ref[...] = h
    o_ref[...] = b_ref[...] * _gelu_tanh(gt_ref[...])
    tail = xp_ref[tt + hist:tt + CONV_PAD, :]
    xp_ref[hist:CONV_PAD, :] = tail

    @pl.when(s == pl.num_programs(1) - 1)
    def _():
        hl_ref[...] = h
        nb_ref[...] = tail


def _rg_lru(h_arr, buf, h0, conv_w, conv_b, wbd, gate_b, lam, *, layer):
    b, t = h_arr.shape[:2]
    tt = _pick(t, 512)
    x_blk, g_blk = C_LRUX // GW, C_LRUG // GW
    wspec = lambda shape: pl.BlockSpec((None,) + shape, lambda bi, s: (layer,) + (0,) * len(shape))
    return pl.pallas_call(
        functools.partial(_lru_kernel, tt=tt),
        grid=(b, t // tt),
        in_specs=[
            pl.BlockSpec((None, tt, GW), lambda bi, s: (bi, s, x_blk)),
            pl.BlockSpec((None, tt, GW), lambda bi, s: (bi, s, g_blk)),
            pl.BlockSpec((None, CONV_W - 1, GW), lambda bi, s: (bi, 0, 0)),
            pl.BlockSpec((None, 1, GW), lambda bi, s: (bi, 0, 0)),
            wspec((CONV_W, GW)), wspec((1, GW)), wspec((GW, 2 * GW)), wspec((1, 2 * GW)), wspec((1, GW)),
        ],
        out_specs=[
            pl.BlockSpec((None, tt, GW), lambda bi, s: (bi, s, 0)),
            pl.BlockSpec((None, 1, GW), lambda bi, s: (bi, 0, 0)),
            pl.BlockSpec((None, CONV_W - 1, GW), lambda bi, s: (bi, 0, 0)),
        ],
        out_shape=[
            jax.ShapeDtypeStruct((b, t, GW), F32),
            jax.ShapeDtypeStruct((b, 1, GW), F32),
            jax.ShapeDtypeStruct((b, CONV_W - 1, GW), F32),
        ],
        scratch_shapes=[
            pltpu.VMEM((CONV_PAD + tt, GW), F32),
            pltpu.VMEM((tt, GW), F32),
            pltpu.VMEM((tt, GW), F32),
            pltpu.VMEM((1, GW), F32),
        ],
        compiler_params=_cparams(("parallel", "arbitrary")),
        name="rg_lru",
    )(h_arr, h_arr, buf, h0, conv_w, conv_b, wbd, gate_b, lam)


def _gdn_kernel(qkv_ref, z_ref, sm_ref, buf_ref, s0_ref, cw_ref, al_ref, dtb_ref, ng_ref,
                o_ref, sout_ref, nb_ref, xp_ref, smp_ref, st_ref, *, tt, tc):
    s = pl.program_id(1)
    hist = CONV_PAD - (CONV_W - 1)
    n_chunks = tc // GDN_CHUNK

    @pl.when(s == 0)
    def _():
        xp_ref[...] = jnp.zeros_like(xp_ref)
        smp_ref[...] = jnp.zeros_like(smp_ref)
        xp_ref[hist:CONV_PAD, :] = buf_ref[...]
        st_ref[...] = s0_ref[...]

    xp_ref[CONV_PAD:CONV_PAD + tt, :] = qkv_ref[...]
    smp_ref[0:tt, :] = sm_ref[...]
    row_ok = lax.broadcasted_iota(jnp.int32, (tc, 1), 0) < tt
    y = cw_ref[0:1, :] * xp_ref[hist:hist + tc, :]
    for jw in range(1, CONV_W):
        y = y + cw_ref[jw:jw + 1, :] * xp_ref[hist + jw:hist + jw + tc, :]
    y = jnp.where(row_ok, _silu(y), 0.0)
    sm = smp_ref[...]
    beta_all = jnp.where(row_ok, _sigmoid(sm), 0.0)
    g_all = jnp.where(row_ok, -jnp.exp(al_ref[...]) * _softplus(sm + dtb_ref[...]), 0.0)

    ri = lax.broadcasted_iota(jnp.int32, (tc, tc), 0)
    ci = lax.broadcasted_iota(jnp.int32, (tc, tc), 1)
    same = (ri // GDN_CHUNK) == (ci // GDN_CHUNK)
    incl = same & (ci <= ri)
    strict = same & (ci < ri)
    gcum = _dot_hp(jnp.where(incl, 1.0, 0.0), g_all)
    gcum_t = _dot_hp(g_all.T, jnp.where(same & (ri <= ci), 1.0, 0.0))
    gtot = _dot_hp(jnp.where(same, 1.0, 0.0), g_all)
    eye = jnp.where(ri == ci, 1.0, 0.0)

    for h in range(GDN_H):
        q = y[:, h * GDN_D:(h + 1) * GDN_D]
        k = y[:, GW + h * GDN_D:GW + (h + 1) * GDN_D]
        v = y[:, 2 * GW + h * GDN_D:2 * GW + (h + 1) * GDN_D]
        q = q * lax.rsqrt(jnp.sum(q * q, axis=-1, keepdims=True) + RMS_EPS) * (GDN_D ** -0.5)
        k = k * lax.rsqrt(jnp.sum(k * k, axis=-1, keepdims=True) + RMS_EPS)
        beta = beta_all[:, SM_BETA + h:SM_BETA + h + 1]
        gc = gcum[:, SM_DECAY + h:SM_DECAY + h + 1]
        gr = gcum_t[SM_DECAY + h:SM_DECAY + h + 1, :]
        gt = gtot[:, SM_DECAY + h:SM_DECAY + h + 1]
        decay = jnp.exp(jnp.where(incl, gc - gr, NEG))
        kb = k * beta
        kbf = k.astype(BF16)
        a_mat = jnp.where(strict, _dot_nt(kb.astype(BF16), kbf) * decay, 0.0)
        qk = jnp.where(incl, _dot_nt(q.astype(BF16), kbf) * decay, 0.0)
        pw = -a_mat
        tinv = eye + pw
        for _ in range(int(math.log2(GDN_CHUNK)) - 1):
            pw = _dot_hp(pw, pw)
            tinv = tinv + _dot_hp(tinv, pw)
        eg = jnp.exp(gc)
        sol = _dot_hp(tinv, jnp.concatenate([v * beta, kb * eg], axis=1))
        u, w = sol[:, :GDN_D], sol[:, GDN_D:]
        qg = q * eg
        kd = k * jnp.exp(gt - gc)
        st = st_ref[h]
        outs = []
        for c in range(n_chunks):
            rows = slice(c * GDN_CHUNK, (c + 1) * GDN_CHUNK)
            stb = st.astype(BF16)
            v_new = u[rows] - _dot(w[rows].astype(BF16), stb)
            v_new_b = v_new.astype(BF16)
            outs.append(_dot(qg[rows].astype(BF16), stb) + _dot(qk[rows, rows].astype(BF16), v_new_b))
            g_last = jnp.exp(gt[c * GDN_CHUNK:c * GDN_CHUNK + 1, :])
            st = st * g_last + _dot_tn(kd[rows].astype(BF16), v_new_b)
        st_ref[h] = st
        o = outs[0] if n_chunks == 1 else jnp.concatenate(outs, axis=0)
        o = o * lax.rsqrt(jnp.mean(o * o, axis=-1, keepdims=True) + RMS_EPS) * ng_ref[...]
        o_ref[:, h * GDN_D:(h + 1) * GDN_D] = o[0:tt] * _silu(z_ref[:, h * GDN_D:(h + 1) * GDN_D])

    tail = xp_ref[tt + hist:tt + CONV_PAD, :]
    xp_ref[hist:CONV_PAD, :] = tail

    @pl.when(s == pl.num_programs(1) - 1)
    def _():
        sout_ref[...] = st_ref[...]
        nb_ref[...] = tail


def _gated_deltanet(h_arr, buf, s0, conv_w, a_log_row, dt_bias_row, norm_g, *, layer):
    b, t = h_arr.shape[:2]
    tt = _pick(t, 256)
    tc = -(-tt // GDN_CHUNK) * GDN_CHUNK
    assert tc == tt or t == tt, "sequence length must be a multiple of the chunk unless it fits one block"
    w3 = 3 * GW
    wspec = lambda shape: pl.BlockSpec((None,) + shape, lambda bi, s: (layer,) + (0,) * len(shape))
    return pl.pallas_call(
        functools.partial(_gdn_kernel, tt=tt, tc=tc),
        grid=(b, t // tt),
        in_specs=[
            pl.BlockSpec((None, tt, w3), lambda bi, s: (bi, s, C_GQKV // w3)),
            pl.BlockSpec((None, tt, GW), lambda bi, s: (bi, s, C_GZ // GW)),
            pl.BlockSpec((None, tt, LANES), lambda bi, s: (bi, s, C_SMALL // LANES)),
            pl.BlockSpec((None, CONV_W - 1, w3), lambda bi, s: (bi, 0, 0)),
            pl.BlockSpec((None, GDN_H, GDN_D, GDN_D), lambda bi, s: (bi, 0, 0, 0)),
            wspec((CONV_W, w3)), wspec((1, LANES)), wspec((1, LANES)), wspec((1, GDN_D)),
        ],
        out_specs=[
            pl.BlockSpec((None, tt, GW), lambda bi, s: (bi, s, 0)),
            pl.BlockSpec((None, GDN_H, GDN_D, GDN_D), lambda bi, s: (bi, 0, 0, 0)),
            pl.BlockSpec((None, CONV_W - 1, w3), lambda bi, s: (bi, 0, 0)),
        ],
        out_shape=[
            jax.ShapeDtypeStruct((b, t, GW), F32),
            jax.ShapeDtypeStruct((b, GDN_H, GDN_D, GDN_D), F32),
            jax.ShapeDtypeStruct((b, CONV_W - 1, w3), F32),
        ],
        scratch_shapes=[
            pltpu.VMEM((CONV_PAD + tc, w3), F32),
            pltpu.VMEM((tc, LANES), F32),
            pltpu.VMEM((GDN_H, GDN_D, GDN_D), F32),
        ],
        compiler_params=_cparams(("parallel", "arbitrary")),
        name="gated_deltanet",
    )(h_arr, h_arr, h_arr, buf, s0, conv_w, a_log_row, dt_bias_row, norm_g)


def _outproj_kernel(oa_ref, ob_ref, oc_ref, od_ref, x_ref, w_ref, gg_ref, g_ref, b_ref, o_ref, *, alpha):
    def rms(v, gain):
        return (v * lax.rsqrt(jnp.mean(v * v, axis=-1, keepdims=True) + RMS_EPS) * gain).astype(BF16)

    acc = _dot(rms(oa_ref[...], gg_ref[0:1, :]), w_ref[0:GW, :])
    acc += _dot(rms(ob_ref[...], gg_ref[1:2, :]), w_ref[GW:2 * GW, :])
    acc += _dot(rms(oc_ref[...], gg_ref[2:3, :]), w_ref[2 * GW:3 * GW, :])
    acc += _dot(od_ref[...].astype(BF16), w_ref[3 * GW:4 * GW, :])
    o_ref[...] = _layer_norm(alpha * x_ref[...] + acc, g_ref[...], b_ref[...])


def _outproj_ln(oa, ob, oc, od, x, w_out, grp_g, ln_g, ln_b, *, layer, alpha):
    n, d = x.shape
    tm = _pick(n, 512)
    mix_spec = pl.BlockSpec((tm, GW), lambda i: (i, 0))
    return pl.pallas_call(
        functools.partial(_outproj_kernel, alpha=alpha),
        grid=(n // tm,),
        in_specs=[
            mix_spec, mix_spec, mix_spec, mix_spec,
            pl.BlockSpec((tm, d), lambda i: (i, 0)),
            pl.BlockSpec((None, 4 * GW, d), lambda i: (layer, 0, 0)),
            pl.BlockSpec((None, 3, GW), lambda i: (layer, 0, 0)),
            pl.BlockSpec((None, None, 1, d), lambda i: (layer, 1, 0, 0)),
            pl.BlockSpec((None, None, 1, d), lambda i: (layer, 1, 0, 0)),
        ],
        out_specs=pl.BlockSpec((tm, d), lambda i: (i, 0)),
        out_shape=jax.ShapeDtypeStruct((n, d), F32),
        compiler_params=_cparams(("parallel",)),
        name="outproj_ln",
    )(oa, ob, oc, od, x, w_out, grp_g, ln_g, ln_b)


def _prep_weights(ffn_gu, ffn_down, w_in, w_out, nsa_phi, lru_gate_w, lru_gate_b, gdn_A_log, gdn_dt_bias):
    depth, d_model, _ = w_in.shape
    o_nsakv, o_gate, o_lrux, o_gqkv, o_gz, o_gb = 2048, 2816, 2840, 3864, 5400, 5912
    zeros = jnp.zeros((depth, d_model, P_IN_PAD - 5920), w_in.dtype)
    w_in_p = jnp.concatenate([
        w_in[..., 512:1536],
        w_in[..., 0:512],
        w_in[..., 1536:2048],
        w_in[..., o_nsakv:o_nsakv + 512],
        w_in[..., o_lrux:o_lrux + 1024],
        w_in[..., o_gz:o_gz + 512],
        w_in[..., o_nsakv + 512:o_gate],
        w_in[..., o_gate:o_lrux],
        w_in[..., o_gb:o_gb + 8],
        zeros,
        w_in[..., o_gqkv:o_gz],
    ], axis=-1).astype(BF16)
    eye_c = jnp.eye(2, dtype=F32)
    eye_g = jnp.eye(NSA_G, dtype=F32)
    phi_big = jnp.einsum("lcjde,cx,gy->ljcgdxye", nsa_phi, eye_c, eye_g)
    phi_big = phi_big.reshape(depth, CMP_BLOCK * 2 * NSA_G * HD, 2 * NSA_G * HD).astype(BF16)
    eye_b = jnp.eye(LRU_BLKS, dtype=F32)
    wbd = jnp.einsum("lknce,nm->lnckme", lru_gate_w, eye_b).reshape(depth, GW, 2 * GW).astype(BF16)
    gate_b = lru_gate_b.reshape(depth, 1, 2 * GW)
    pad_l = jnp.zeros((depth, SM_DECAY), F32)
    pad_r = jnp.zeros((depth, LANES - SM_DECAY - GDN_H), F32)
    a_log_row = jnp.concatenate([pad_l, gdn_A_log, pad_r], axis=1)[:, None, :]
    dt_bias_row = jnp.concatenate([pad_l, gdn_dt_bias, pad_r], axis=1)[:, None, :]
    return dict(
        wgu=ffn_gu.astype(BF16), wd=ffn_down.astype(BF16), w_in=w_in_p, w_out=w_out.astype(BF16),
        phi_big=phi_big, wbd=wbd, gate_b=gate_b, a_log_row=a_log_row, dt_bias_row=dt_bias_row)


def _trunk_layer(x, b, t, q0, past, wts, layer, alpha):
    ln_g, ln_b = wts["ln_g"], wts["ln_b"]
    x = _ffn_ln(x, wts["wgu"], wts["wd"], ln_g, ln_b, layer=layer, which=0, ln_idx=0, alpha=alpha)
    h = _matmul(x, wts["w_in"], layer=layer, name="in_proj")
    h3 = h.reshape(b, t, P_IN_PAD)

    sb_rows = h3[:, :, C_SBK:C_SBK + 2 * GW]
    nsa_rows = h3[:, :, C_NSAKV:C_NSAKV + GW]
    nsa_win_new = h3[:, :, C_NSAW:C_NSAW + 2 * LANES]

    if past["page_table"] is None:
        lw = 0
        o_a = _sb_attend(h3, C_SBQ // GW, h3, C_SBK // GW, h3, C_SBV // GW, q0=q0)
        cmp_rows = h3[:, :, C_NSAKV:C_NSAKV + 2 * LANES]
        ks_arr, ks_blk = h3, (C_NSAKV + 2 * LANES) // (2 * LANES)
        win_all = nsa_win_new
    else:
        pt = past["page_table"]
        lw = past["nsa_win"].shape[2]
        (sb_dense,) = _gather_pages(past["sb_kv"], pt, h3, C_SBK // (2 * GW), layer=layer,
                                    col_splits=((0, 2 * GW),))
        o_a = _sb_attend(h3, C_SBQ // GW, sb_dense, 0, sb_dense, 1, q0=q0)
        cmp_rows, ks_arr = _gather_pages(past["nsa_kv"], pt, h3, C_NSAKV // GW, layer=layer,
                                         col_splits=((0, 2 * LANES), (2 * LANES, 4 * LANES)))
        ks_blk = 0
        win_all = jnp.concatenate([past["nsa_win"][layer], nsa_win_new], axis=1)

    l_cmp = cmp_rows.shape[1]
    nb_arr = l_cmp // CMP_BLOCK
    kc = _matmul(cmp_rows.reshape(b * nb_arr, CMP_BLOCK * 2 * LANES), wts["phi_big"], layer=layer,
                 tm_pref=512, tn_pref=256, name="nsa_compress")
    kc = kc.reshape(b, nb_arr, 2 * LANES)
    nbp = -(-nb_arr // LANES) * LANES
    kc = jnp.pad(kc, ((0, 0), (0, nbp - nb_arr), (0, 0)))
    o_b = _nsa_attend(h3, C_NSAQ // GW, h3, C_SMALL // LANES, kc, ks_arr, ks_blk, win_all,
                      q0=q0, lw=lw, n_real_keys=q0 + t)
    new_win = win_all[:, -min(WINDOW, lw + t):]

    o_c, h_last, new_lru_buf = _rg_lru(h3, past["lru_conv"], past["lru_h"], wts["lru_conv_w"], wts["lru_conv_b"],
                                       wts["wbd"], wts["gate_b"], wts["lru_lambda"], layer=layer)
    o_d, s_new, new_gdn_buf = _gated_deltanet(h3, past["gdn_conv"], past["gdn_S"], wts["gdn_conv_w"],
                                              wts["a_log_row"], wts["dt_bias_row"], wts["gdn_norm_g"], layer=layer)

    n = b * t
    x = _outproj_ln(o_a.reshape(n, GW), o_b.reshape(n, GW), o_c.reshape(n, GW), o_d.reshape(n, GW), x,
                    wts["w_out"], wts["grp_norm_g"], ln_g, ln_b, layer=layer, alpha=alpha)
    x = _ffn_ln(x, wts["wgu"], wts["wd"], ln_g, ln_b, layer=layer, which=1, ln_idx=2, alpha=alpha)
    new_state = (
        sb_rows.reshape(b, t, 2, SB_H, HD),
        nsa_rows.reshape(b, t, 4, NSA_G, HD),
        new_win.reshape(b, new_win.shape[1], 2, NSA_G, HD),
        h_last.reshape(b, GW),
        new_lru_buf,
        s_new,
        new_gdn_buf,
    )
    return x, new_state


def kernel(x_prompt, x_sample, cache_sb_kv, cache_nsa_kv, cache_nsa_win, state_lru_h, state_lru_conv,
           state_gdn_S, state_gdn_conv, page_table, ln_g, ln_b, ffn_gu, ffn_down, w_in, w_out, grp_norm_g,
           nsa_phi, lru_conv_w, lru_conv_b, lru_gate_w, lru_gate_b, lru_lambda, gdn_conv_w, gdn_A_log,
           gdn_dt_bias, gdn_norm_g):
    depth, d_model = w_in.shape[0], w_in.shape[1]
    alpha = (2 * depth) ** 0.25
    n_b, seq = x_prompt.shape[:2]
    n_db, dec_seq = x_sample.shape[:2]
    past_len = page_table.shape[1] * PAGE
    n_pool = cache_sb_kv.shape[1]

    wts = _prep_weights(ffn_gu, ffn_down, w_in, w_out, nsa_phi, lru_gate_w, lru_gate_b, gdn_A_log, gdn_dt_bias)
    wts.update(
        ln_g=ln_g.reshape(depth, 3, 1, d_model), ln_b=ln_b.reshape(depth, 3, 1, d_model),
        grp_norm_g=grp_norm_g, lru_conv_w=lru_conv_w, lru_conv_b=lru_conv_b.reshape(depth, 1, GW),
        lru_lambda=lru_lambda.reshape(depth, 1, GW), gdn_conv_w=gdn_conv_w,
        gdn_norm_g=gdn_norm_g.reshape(depth, 1, GDN_D))

    sb_cache = cache_sb_kv.reshape(depth, n_pool, PAGE, 2 * GW)
    nsa_cache = cache_nsa_kv.reshape(depth, n_pool, PAGE, GW)
    nsa_win = cache_nsa_win.reshape(depth, n_db, cache_nsa_win.shape[2], 2 * LANES)

    y_p = x_prompt.reshape(n_b * seq, d_model)
    y_s = x_sample.reshape(n_db * dec_seq, d_model)
    st_p, st_s = [], []
    for l in range(depth):
        past_p = dict(page_table=None,
                      lru_h=jnp.zeros((n_b, 1, GW), F32), lru_conv=jnp.zeros((n_b, CONV_W - 1, GW), F32),
                      gdn_S=jnp.zeros((n_b, GDN_H, GDN_D, GDN_D), F32),
                      gdn_conv=jnp.zeros((n_b, CONV_W - 1, 3 * GW), F32))
        past_s = dict(page_table=page_table, sb_kv=sb_cache, nsa_kv=nsa_cache, nsa_win=nsa_win,
                      lru_h=state_lru_h[l].reshape(n_db, 1, GW), lru_conv=state_lru_conv[l],
                      gdn_S=state_gdn_S[l], gdn_conv=state_gdn_conv[l])
        y_p, new_p = _trunk_layer(y_p, n_b, seq, 0, past_p, wts, l, alpha)
        y_s, new_s = _trunk_layer(y_s, n_db, dec_seq, past_len, past_s, wts, l, alpha)
        st_p.append(new_p)
        st_s.append(new_s)
    p = [jnp.stack(a) for a in zip(*st_p)]
    s = [jnp.stack(a) for a in zip(*st_s)]
    return (y_p.reshape(n_b, seq, d_model), y_s.reshape(n_db, dec_seq, d_model),
            p[0], p[1], p[2], p[3], p[4], p[5], p[6], s[0], s[1], s[2], s[3], s[4], s[5], s[6])
```

```python
import functools
import math

import numpy as np
import jax
import jax.numpy as jnp
from jax import lax
from jax.experimental import pallas as pl
from jax.experimental.pallas import tpu as pltpu

F32 = jnp.float32
BF16 = jnp.bfloat16
HIGHEST = lax.Precision.HIGHEST

PAGE = 128
GW = 512
HD = 64
SB_H = 8
NSA_G = 2
NSA_R = 4
CMP_BLOCK = 64
SEL_TOPN = 16
SEL_FORCE = float(NSA_R + 1)
WINDOW = 512
LRU_BLKS = 8
LRU_C = 8.0
CONV_W = 4
GDN_H = 4
GDN_D = 128
GDN_CHUNK = 64
LN_EPS = 1e-5
RMS_EPS = 1e-6
NEG = -1e30
LOG2E = 1.4426950408889634
QSCALE2 = (HD ** -0.5) * LOG2E

LANES = 128
SUBLANES = 8
VMEM_LIMIT = 56 * 1024 * 1024

C_SBK, C_SBV, C_SBQ, C_NSAQ, C_NSAKV = 0, 512, 1024, 1536, 2048
C_LRUX, C_LRUG, C_GZ, C_NSAW, C_SMALL, C_GQKV = 2560, 3072, 3584, 4096, 4352, 4608
P_IN_PAD = 6144
SM_GATE, SM_BETA, SM_DECAY = 0, 24, 28
CONV_PAD = 8


def _pick(n, pref, mult=SUBLANES):
    if n <= pref:
        return n
    for t in range(pref, 0, -1):
        if n % t == 0 and t % mult == 0:
            return t
    return n


def _cparams(sem):
    return pltpu.CompilerParams(dimension_semantics=sem, vmem_limit_bytes=VMEM_LIMIT)


def _sigmoid(x):
    return 1.0 / (1.0 + jnp.exp(-x))


def _silu(x):
    return x * _sigmoid(x)


def _softplus(x):
    return jnp.maximum(x, 0.0) + jnp.log1p(jnp.exp(-jnp.abs(x)))


def _log_sigmoid(x):
    return jnp.minimum(x, 0.0) - jnp.log1p(jnp.exp(-jnp.abs(x)))


def _gelu_tanh(x):
    return 0.5 * x * (1.0 + jnp.tanh(math.sqrt(2.0 / math.pi) * (x + 0.044715 * (x * x * x))))


def _layer_norm(y, g, b):
    mu = jnp.mean(y, axis=-1, keepdims=True)
    d = y - mu
    var = jnp.mean(d * d, axis=-1, keepdims=True)
    return d * lax.rsqrt(var + LN_EPS) * g + b


def _dot(a, b):
    return jnp.dot(a, b, preferred_element_type=F32)


def _dot_nt(a, b):
    return lax.dot_general(a, b, (((1,), (1,)), ((), ())), preferred_element_type=F32)


def _dot_tn(a, b):
    return lax.dot_general(a, b, (((0,), (0,)), ((), ())), preferred_element_type=F32)


def _split_bf16(x):
    hi = x.astype(BF16)
    return hi, (x - hi.astype(F32)).astype(BF16)


def _dot_hp(a, b):
    ah, al = _split_bf16(a)
    bh, bl = _split_bf16(b)
    return _dot(jnp.concatenate([ah, ah, al], axis=1), jnp.concatenate([bh, bl, bh], axis=0))


def _dot_hp_exact_lhs(a_bf16, b):
    bh, bl = _split_bf16(b)
    return _dot(jnp.concatenate([a_bf16, a_bf16], axis=1), jnp.concatenate([bh, bl], axis=0))


def _dot_hp_exact_rhs(a, b_bf16):
    ah, al = _split_bf16(a)
    return _dot(jnp.concatenate([ah, al], axis=1), jnp.concatenate([b_bf16, b_bf16], axis=0))


def _ffn_ln_kernel(x_ref, wg_ref, wu_ref, wd_ref, g_ref, b_ref, o_ref, xb_ref, acc_ref, *, alpha):
    j = pl.program_id(1)

    @pl.when(j == 0)
    def _():
        xb_ref[...] = x_ref[...].astype(BF16)
        acc_ref[...] = jnp.zeros_like(acc_ref)

    xb = xb_ref[...]
    gate = _dot(xb, wg_ref[...])
    up = _dot(xb, wu_ref[...])
    act = (_silu(gate) * up).astype(BF16)
    acc_ref[...] += _dot(act, wd_ref[...])

    @pl.when(j == pl.num_programs(1) - 1)
    def _():
        y = alpha * x_ref[...] + 0.5 * acc_ref[...]
        o_ref[...] = _layer_norm(y, g_ref[...], b_ref[...])


def _ffn_ln(x, wgu, wd, ln_g, ln_b, *, layer, which, ln_idx, alpha):
    n, d = x.shape
    f = wd.shape[2]
    tm = _pick(n, 512)
    tf = _pick(f, 512, LANES)
    nf = f // tf
    return pl.pallas_call(
        functools.partial(_ffn_ln_kernel, alpha=alpha),
        grid=(n // tm, nf),
        in_specs=[
            pl.BlockSpec((tm, d), lambda i, j: (i, 0)),
            pl.BlockSpec((None, None, d, tf), lambda i, j: (layer, which, 0, j)),
            pl.BlockSpec((None, None, d, tf), lambda i, j: (layer, which, 0, j + nf)),
            pl.BlockSpec((None, None, tf, d), lambda i, j: (layer, which, j, 0)),
            pl.BlockSpec((None, None, 1, d), lambda i, j: (layer, ln_idx, 0, 0)),
            pl.BlockSpec((None, None, 1, d), lambda i, j: (layer, ln_idx, 0, 0)),
        ],
        out_specs=pl.BlockSpec((tm, d), lambda i, j: (i, 0)),
        out_shape=jax.ShapeDtypeStruct((n, d), F32),
        scratch_shapes=[pltpu.VMEM((tm, d), BF16), pltpu.VMEM((tm, d), F32)],
        compiler_params=_cparams(("parallel", "arbitrary")),
        name="ffn_ln",
    )(x, wgu, wgu, wd, ln_g, ln_b)


def _matmul_kernel(x_ref, w_ref, o_ref):
    k = pl.program_id(2)

    @pl.when(k == 0)
    def _():
        o_ref[...] = jnp.zeros_like(o_ref)

    o_ref[...] += _dot(x_ref[...].astype(BF16), w_ref[...])


def _matmul(x, w, *, layer, tm_pref=1024, tn_pref=512, tk_pref=2048, name="matmul"):
    m, kdim = x.shape
    nout = w.shape[2]
    tm = _pick(m, tm_pref)
    tn = _pick(nout, tn_pref, LANES)
    tk = _pick(kdim, tk_pref, LANES)
    return pl.pallas_call(
        _matmul_kernel,
        grid=(m // tm, nout // tn, kdim // tk),
        in_specs=[
            pl.BlockSpec((tm, tk), lambda i, j, k: (i, k)),
            pl.BlockSpec((None, tk, tn), lambda i, j, k: (layer, k, j)),
        ],
        out_specs=pl.BlockSpec((tm, tn), lambda i, j, k: (i, j)),
        out_shape=jax.ShapeDtypeStruct((m, nout), F32),
        compiler_params=_cparams(("parallel", "parallel", "arbitrary")),
        name=name,
    )(x, w)


def _gather_nsa_kernel(pt_ref, *refs, n_group, t_new):
    del pt_ref
    page_refs = refs[:n_group]
    new_ref = refs[n_group]
    cmp_ref, sel_ref = refs[n_group + 1:]
    s = pl.program_id(1)
    last = pl.num_programs(1) - 1

    @pl.when(s < last)
    def _():
        for g in range(n_group):
            rows = slice(g * PAGE, (g + 1) * PAGE)
            for kind in range(4):
                x = page_refs[g][kind].reshape(NSA_G * HD, PAGE).T
                o_ref = cmp_ref if kind < 2 else sel_ref
                o_ref[rows, (kind % 2) * LANES:(kind % 2 + 1) * LANES] = x

    @pl.when(s == last)
    def _():
        for o_ref, c0 in ((cmp_ref, 0), (sel_ref, 2 * LANES)):
            o_ref[...] = jnp.zeros_like(o_ref)
            o_ref[0:t_new, :] = new_ref[:, c0:c0 + 2 * LANES]


def _gather_nsa_pages(cache_t, page_table, new_arr, new_colblk, *, layer):
    b, n_pages = page_table.shape
    t_new = new_arr.shape[1]
    n_group = math.gcd(n_pages, 8)
    n_steps = n_pages // n_group + 1
    rows = n_group * PAGE

    def page_map(g):
        return lambda bi, s, pt: (layer, pt[bi, jnp.minimum(s * n_group + g, n_pages - 1)], 0, 0, 0, 0)

    in_specs = [pl.BlockSpec((None, None, 4, NSA_G, HD, PAGE), page_map(g)) for g in range(n_group)]
    in_specs.append(pl.BlockSpec((None, t_new, GW), lambda bi, s, pt: (bi, 0, new_colblk)))
    out_specs = [pl.BlockSpec((None, rows, 2 * LANES), lambda bi, s, pt: (bi, s, 0)) for _ in range(2)]
    out_shape = [jax.ShapeDtypeStruct((b, n_steps * rows, 2 * LANES), F32) for _ in range(2)]
    return pl.pallas_call(
        functools.partial(_gather_nsa_kernel, n_group=n_group, t_new=t_new),
        grid_spec=pltpu.PrefetchScalarGridSpec(
            num_scalar_prefetch=1, grid=(b, n_steps), in_specs=in_specs, out_specs=out_specs),
        out_shape=out_shape,
        compiler_params=_cparams(("parallel", "arbitrary")),
        name="gather_nsa_pages",
    )(page_table, *([cache_t] * n_group), new_arr)


def _suffix_matrix(n):
    j = lax.broadcasted_iota(jnp.int32, (2 * n, n), 0)
    s = lax.broadcasted_iota(jnp.int32, (2 * n, n), 1)
    return ((j > s) & ((j < n) | (j - n > s))).astype(BF16)


def _sb_weights(z2, tri2, carry, mask):
    nl = jnp.maximum(z2, 0.0) + jnp.log2(1.0 + jnp.exp2(-jnp.abs(z2)))
    if mask is not None:
        nl = jnp.where(mask, nl, 0.0)
    hi = nl.astype(BF16)
    lo = (nl - hi.astype(F32)).astype(BF16)
    tail = _dot(jnp.concatenate([hi, lo], axis=1), tri2)
    w = jnp.exp2(z2 - nl - tail - carry)
    if mask is not None:
        w = jnp.where(mask, w, 0.0)
    return w.astype(BF16), tail[:, 0:1] + nl[:, 0:1]


def _sb_kernel(q_ref, k_ref, v_ref, o_ref, acc_ref, c_ref, *, tq, tk, q0):
    i = pl.program_id(1)
    j = pl.program_id(2)
    jt = (q0 + (i + 1) * tq - 2) // tk - j
    qs0 = q0 + i * tq

    @pl.when(j == 0)
    def _():
        acc_ref[...] = jnp.zeros_like(acc_ref)
        c_ref[...] = jnp.zeros_like(c_ref)

    def tile(masked):
        mask = None
        if masked:
            qpos = qs0 + lax.broadcasted_iota(jnp.int32, (tq, tk), 0)
            kpos = jt * tk + lax.broadcasted_iota(jnp.int32, (tq, tk), 1)
            mask = kpos < qpos
        tri = _suffix_matrix(tk)
        lane_half = lax.broadcasted_iota(jnp.int32, (tq, LANES), 1) // HD
        for h in range(SB_H):
            c, half = divmod(h, 2)
            cols = slice(c * LANES, (c + 1) * LANES)
            qc = jnp.where(lane_half == half, q_ref[:, cols] * QSCALE2, 0.0).astype(BF16)
            w, rs = _sb_weights(_dot_nt(qc, k_ref[:, cols].astype(BF16)), tri, c_ref[h], mask)
            acc_ref[h] += _dot(w, v_ref[:, cols].astype(BF16))
            c_ref[h] += rs

    whole = (jt + 1) * tk <= qs0

    @pl.when((jt >= 0) & whole)
    def _():
        tile(False)

    @pl.when((jt >= 0) & jnp.logical_not(whole))
    def _():
        tile(True)

    @pl.when(j == pl.num_programs(2) - 1)
    def _():
        lane_half = lax.broadcasted_iota(jnp.int32, (tq, LANES), 1) // HD
        for c in range(SB_H // 2):
            o_ref[:, c * LANES:(c + 1) * LANES] = jnp.where(lane_half == 0, acc_ref[2 * c], acc_ref[2 * c + 1])


def _sb_attend(q_arr, q_blk, k_arr, k_blk, v_arr, v_blk, *, q0):
    b, t = q_arr.shape[:2]
    l = k_arr.shape[1]
    tq = _pick(t, 256)
    tk = _pick(l, 256, LANES)
    nq = t // tq
    nk = (q0 + t - 2) // tk + 1

    def kv_map(blk):
        return lambda bi, i, j: (bi, jnp.maximum((q0 + (i + 1) * tq - 2) // tk - j, 0), blk)

    return pl.pallas_call(
        functools.partial(_sb_kernel, tq=tq, tk=tk, q0=q0),
        grid=(b, nq, nk),
        in_specs=[
            pl.BlockSpec((None, tq, GW), lambda bi, i, j: (bi, i, q_blk)),
            pl.BlockSpec((None, tk, GW), kv_map(k_blk)),
            pl.BlockSpec((None, tk, GW), kv_map(v_blk)),
        ],
        out_specs=pl.BlockSpec((None, tq, GW), lambda bi, i, j: (bi, i, 0)),
        out_shape=jax.ShapeDtypeStruct((b, t, GW), F32),
        scratch_shapes=[pltpu.VMEM((SB_H, tq, LANES), F32), pltpu.VMEM((SB_H, tq, 1), F32)],
        compiler_params=_cparams(("parallel", "parallel", "arbitrary")),
        name="sb_attend",
    )(q_arr, k_arr, v_arr)


def _sb_paged_kernel(pt_ref, q_ref, new_ref, *refs, n_group, t):
    del pt_ref
    page_refs = refs[:n_group]
    o_ref, qbd_ref, acc_ref, c_ref = refs[n_group:]
    s = pl.program_id(1)
    m = SB_H * t
    tri = _suffix_matrix(PAGE)

    @pl.when(s == 0)
    def _():
        lane_head = lax.broadcasted_iota(jnp.int32, (t, GW), 1) // HD
        q = q_ref[...] * QSCALE2
        for h in range(SB_H):
            qbd_ref[h * t:(h + 1) * t, :] = jnp.where(lane_head == h, q, 0.0)
        pad = jnp.zeros((PAGE - t, GW), F32)
        k_new = jnp.concatenate([new_ref[:, 0:GW], pad], axis=0).astype(BF16)
        v_new = jnp.concatenate([new_ref[:, GW:2 * GW], pad], axis=0).astype(BF16)
        row = lax.broadcasted_iota(jnp.int32, (m, PAGE), 0)
        mask = lax.broadcasted_iota(jnp.int32, (m, PAGE), 1) < row - (row // t) * t
        w, rs = _sb_weights(_dot_nt(qbd_ref[...].astype(BF16), k_new), tri, 0.0, mask)
        acc_ref[...] = _dot(w, v_new)
        c_ref[...] = rs

    qb = qbd_ref[...].astype(BF16)
    carry = c_ref[...]
    acc = acc_ref[...]
    for g in reversed(range(n_group)):
        kt = page_refs[g][0].reshape(GW, PAGE).astype(BF16)
        vt = page_refs[g][1].reshape(GW, PAGE).astype(BF16)
        w, rs = _sb_weights(_dot(qb, kt), tri, carry, None)
        acc = acc + _dot_nt(w, vt)
        carry = carry + rs
    acc_ref[...] = acc
    c_ref[...] = carry

    @pl.when(s == pl.num_programs(1) - 1)
    def _():
        lane_head = lax.broadcasted_iota(jnp.int32, (t, GW), 1) // HD
        out = jnp.zeros((t, GW), F32)
        for h in range(SB_H):
            out = out + jnp.where(lane_head == h, acc_ref[h * t:(h + 1) * t, :], 0.0)
        o_ref[...] = out


def _sb_attend_paged(q_arr, q_blk, new_blk, cache_t, page_table, *, layer):
    b, t = q_arr.shape[:2]
    n_pages = page_table.shape[1]
    n_group = math.gcd(n_pages, 8)
    n_steps = n_pages // n_group
    m = SB_H * t

    def page_map(g):
        return lambda bi, s, pt: (layer, pt[bi, n_pages - (s + 1) * n_group + g], 0, 0, 0, 0)

    in_specs = [
        pl.BlockSpec((None, t, GW), lambda bi, s, pt: (bi, 0, q_blk)),
        pl.BlockSpec((None, t, 2 * GW), lambda bi, s, pt: (bi, 0, new_blk)),
    ] + [pl.BlockSpec((None, None, 2, SB_H, HD, PAGE), page_map(g)) for g in range(n_group)]
    return pl.pallas_call(
        functools.partial(_sb_paged_kernel, n_group=n_group, t=t),
        grid_spec=pltpu.PrefetchScalarGridSpec(
            num_scalar_prefetch=1, grid=(b, n_steps), in_specs=in_specs,
            out_specs=pl.BlockSpec((None, t, GW), lambda bi, s, pt: (bi, 0, 0)),
            scratch_shapes=[pltpu.VMEM((m, GW), F32), pltpu.VMEM((m, GW), F32), pltpu.VMEM((m, 1), F32)]),
        out_shape=jax.ShapeDtypeStruct((b, t, GW), F32),
        compiler_params=_cparams(("parallel", "arbitrary")),
        name="sb_attend_paged",
    )(page_table, q_arr, q_arr, *([cache_t] * n_group))


def _masked_softmax(s2, mask):
    sm = jnp.where(mask, s2, NEG)
    m = jnp.max(sm, axis=-1, keepdims=True)
    e = jnp.where(mask, jnp.exp2(sm - m), 0.0)
    return e / jnp.maximum(jnp.sum(e, axis=-1, keepdims=True), 1e-30)


def _select_blocks(imp, blk, nb, n_sel):
    tq, nbp = imp.shape
    if tq % LANES != 0:
        rank = jnp.zeros((tq, nbp), F32)
        for mblk in range(nb):
            col = imp[:, mblk:mblk + 1]
            beats = (col > imp) | ((col == imp) & (blk > mblk))
            rank = rank + jnp.where(beats, 1.0, 0.0)
        return jnp.where(rank < n_sel, 1.0, 0.0)
    imp_t = imp.T
    n_parts = -(-nb // SUBLANES)
    parts = [imp_t[p * SUBLANES:(p + 1) * SUBLANES, :] for p in range(n_parts)]
    row_in_part = lax.broadcasted_iota(jnp.int32, (SUBLANES, tq), 0)
    ranks = [jnp.zeros((SUBLANES, tq), F32) for _ in range(n_parts)]
    for mblk in range(nb):
        pm, rm = divmod(mblk, SUBLANES)
        row = parts[pm][rm:rm + 1, :]
        for p in range(n_parts):
            if p < pm:
                beats = row > parts[p]
            elif p > pm:
                beats = row >= parts[p]
            else:
                beats = (row > parts[p]) | ((row == parts[p]) & (row_in_part > rm))
            ranks[p] = ranks[p] + jnp.where(beats, 1.0, 0.0)
    sel_t = [jnp.where(r < n_sel, 1.0, 0.0) for r in ranks]
    if nbp > n_parts * SUBLANES:
        sel_t.append(jnp.zeros((nbp - n_parts * SUBLANES, tq), F32))
    return jnp.concatenate(sel_t, axis=0).T


def _nsa_kernel(q_ref, sm_ref, kc_ref, ks_ref, win_ref, o_ref,
                qs_ref, sel_ref, part_ref, m_ref, l_ref, acc_ref,
                *, tq, tk, q0, lw, nb, nbp, n_sel, wl):
    i = pl.program_id(1)
    j = pl.program_id(2)
    rq = NSA_R * tq
    qs0 = q0 + i * tq
    j_last = (qs0 + tq - 1) // tk
    scale = QSCALE2

    @pl.when(j == 0)
    def _():
        lane_half = lax.broadcasted_iota(jnp.int32, (tq, LANES), 1) // HD
        gates = _sigmoid(sm_ref[...])
        qpos_b = qs0 + lax.broadcasted_iota(jnp.int32, (tq, nbp), 0)
        blk = lax.broadcasted_iota(jnp.int32, (tq, nbp), 1)
        cmask = ((blk + 1) * CMP_BLOCK - 1 <= qpos_b) & (blk < nb)
        cur = qpos_b // CMP_BLOCK
        forced = (blk == 0) | (blk == cur)
        valid = blk <= cur
        start = pl.multiple_of(jnp.maximum(lw + (i + 1) * tq - wl, 0), SUBLANES)
        qpos_w = qs0 + lax.broadcasted_iota(jnp.int32, (tq, wl), 0)
        kpos_w = q0 - lw + start + lax.broadcasted_iota(jnp.int32, (tq, wl), 1)
        wmask = (kpos_w <= qpos_w) & (kpos_w > qpos_w - WINDOW)
        kw = win_ref[pl.ds(start, wl), 0:LANES].astype(BF16)
        vw = win_ref[pl.ds(start, wl), LANES:2 * LANES].astype(BF16)
        kc = kc_ref[:, 0:LANES].astype(BF16)
        vc = kc_ref[:, LANES:2 * LANES].astype(BF16)
        for g in range(NSA_G):
            for r in range(NSA_R):
                chunk, half = 2 * g + r // 2, r % 2
                qc = q_ref[:, chunk * LANES:(chunk + 1) * LANES]
                if half != g:
                    qc = pltpu.roll(qc, HD, 1)
                qs_ref[g, r * tq:(r + 1) * tq, :] = jnp.where(lane_half == g, qc * scale, 0.0)
            qs = qs_ref[g].astype(BF16)
            s_c = _dot_nt(qs, kc).reshape(NSA_R, tq, nbp)
            p_c = _masked_softmax(s_c, cmask[None])
            o_c = _dot(p_c.reshape(rq, nbp).astype(BF16), vc)
            imp = jnp.sum(p_c, axis=0)
            imp = jnp.where(forced, SEL_FORCE, jnp.where(valid, imp, -1.0))
            imp = jnp.where(blk < nb, imp, -2.0)
            sel_ref[g] = _select_blocks(imp, blk, nb, n_sel)
            s_w = _dot_nt(qs, kw).reshape(NSA_R, tq, wl)
            p_w = _masked_softmax(s_w, wmask[None])
            o_w = _dot(p_w.reshape(rq, wl).astype(BF16), vw)
            for r in range(NSA_R):
                rows = slice(r * tq, (r + 1) * tq)
                lane_c = SM_GATE + 0 * NSA_G * NSA_R + g * NSA_R + r
                lane_w = SM_GATE + 2 * NSA_G * NSA_R + g * NSA_R + r
                part_ref[g, rows, :] = (gates[:, lane_c:lane_c + 1] * o_c[rows]
                                        + gates[:, lane_w:lane_w + 1] * o_w[rows])
        m_ref[...] = jnp.full_like(m_ref, NEG)
        l_ref[...] = jnp.zeros_like(l_ref)
        acc_ref[...] = jnp.zeros_like(acc_ref)

    @pl.when(j <= j_last)
    def _():
        slab = tq if tq >= LANES else rq
        ck = tk
        for g in range(NSA_G):
            sel_b = sel_ref[g].astype(BF16)
            for k0 in range(0, tk, ck):
                ks = ks_ref[k0:k0 + ck, 0:LANES].astype(BF16)
                vs = ks_ref[k0:k0 + ck, LANES:2 * LANES].astype(BF16)
                kblk = (j * tk + k0 + lax.broadcasted_iota(jnp.int32, (nbp, ck), 1)) // CMP_BLOCK
                expand = (lax.broadcasted_iota(jnp.int32, (nbp, ck), 0) == kblk).astype(BF16)
                qpos = qs0 + lax.broadcasted_iota(jnp.int32, (tq, ck), 0)
                kpos = j * tk + k0 + lax.broadcasted_iota(jnp.int32, (tq, ck), 1)
                bias = jnp.where((_dot(sel_b, expand) > 0.5) & (kpos <= qpos), 0.0, NEG)
                if slab != tq:
                    bias = jnp.concatenate([bias] * (slab // tq), axis=0)
                for r0 in range(0, rq, slab):
                    rows = slice(r0, r0 + slab)
                    s = _dot_nt(qs_ref[g, rows, :].astype(BF16), ks) + bias
                    m_old = m_ref[g, rows, :]
                    m_new = jnp.maximum(m_old, jnp.max(s, axis=-1, keepdims=True))
                    p = jnp.exp2(s - m_new)
                    corr = jnp.exp2(m_old - m_new)
                    l_ref[g, rows, :] = corr * l_ref[g, rows, :] + jnp.sum(p, axis=-1, keepdims=True)
                    acc_ref[g, rows, :] = corr * acc_ref[g, rows, :] + _dot(p.astype(BF16), vs)
                    m_ref[g, rows, :] = m_new

    @pl.when(j == pl.num_programs(2) - 1)
    def _():
        gates = _sigmoid(sm_ref[...])
        lane_half = lax.broadcasted_iota(jnp.int32, (tq, LANES), 1) // HD
        for g in range(NSA_G):
            o_s = acc_ref[g] / jnp.maximum(l_ref[g], 1e-30)
            res = []
            for r in range(NSA_R):
                rows = slice(r * tq, (r + 1) * tq)
                lane_s = SM_GATE + 1 * NSA_G * NSA_R + g * NSA_R + r
                res.append(part_ref[g, rows, :] + gates[:, lane_s:lane_s + 1] * o_s[rows])
            for c2 in range(2):
                even, odd = res[2 * c2], res[2 * c2 + 1]
                if g == 1:
                    even = pltpu.roll(even, HD, 1)
                else:
                    odd = pltpu.roll(odd, HD, 1)
                chunk = 2 * g + c2
                o_ref[:, chunk * LANES:(chunk + 1) * LANES] = jnp.where(lane_half == 0, even, odd)


def _nsa_attend(q_arr, q_blk, sm_arr, sm_blk, kc_arr, ks_arr, ks_blk, win_arr, *, q0, lw, n_real_keys):
    b, t = q_arr.shape[:2]
    l = ks_arr.shape[1]
    nbp = kc_arr.shape[1]
    nb = -(-n_real_keys // CMP_BLOCK)
    n_sel = min(SEL_TOPN, nb)
    tq = _pick(t, 128)
    tk = _pick(l, 1024, LANES)
    nq = t // tq
    nk = (q0 + t - 1) // tk + 1
    wl = min(WINDOW + tq, lw + t)
    rq = NSA_R * tq

    def ks_map(bi, i, j):
        return (bi, jnp.minimum(j, (q0 + (i + 1) * tq - 1) // tk), ks_blk)

    return pl.pallas_call(
        functools.partial(_nsa_kernel, tq=tq, tk=tk, q0=q0, lw=lw, nb=nb, nbp=nbp, n_sel=n_sel, wl=wl),
        grid=(b, nq, nk),
        in_specs=[
            pl.BlockSpec((None, tq, GW), lambda bi, i, j: (bi, i, q_blk)),
            pl.BlockSpec((None, tq, LANES), lambda bi, i, j: (bi, i, sm_blk)),
            pl.BlockSpec((None, nbp, 2 * LANES), lambda bi, i, j: (bi, 0, 0)),
            pl.BlockSpec((None, tk, 2 * LANES), ks_map),
            pl.BlockSpec((None, lw + t, 2 * LANES), lambda bi, i, j: (bi, 0, 0)),
        ],
        out_specs=pl.BlockSpec((None, tq, GW), lambda bi, i, j: (bi, i, 0)),
        out_shape=jax.ShapeDtypeStruct((b, t, GW), F32),
        scratch_shapes=[
            pltpu.VMEM((NSA_G, rq, LANES), F32),
            pltpu.VMEM((NSA_G, tq, nbp), F32),
            pltpu.VMEM((NSA_G, rq, LANES), F32),
            pltpu.VMEM((NSA_G, rq, 1), F32),
            pltpu.VMEM((NSA_G, rq, 1), F32),
            pltpu.VMEM((NSA_G, rq, LANES), F32),
        ],
        compiler_params=_cparams(("parallel", "parallel", "arbitrary")),
        name="nsa_attend",
    )(q_arr, sm_arr, kc_arr, ks_arr, win_arr)


def _lru_kernel(x_ref, gt_ref, buf_ref, h0_ref, cw_ref, cb_ref, wbd_ref, gb_ref, lam_ref,
                o_ref, hl_ref, nb_ref, xp_ref, a_ref, b_ref, h_ref, *, tt):
    s = pl.program_id(1)
    hist = CONV_PAD - (CONV_W - 1)

    @pl.when(s == 0)
    def _():
        xp_ref[hist:CONV_PAD, :] = buf_ref[...]
        h_ref[...] = h0_ref[...]

    xp_ref[CONV_PAD:CONV_PAD + tt, :] = x_ref[...]
    u = cb_ref[...]
    for jw in range(CONV_W):
        u = u + cw_ref[jw:jw + 1, :] * xp_ref[hist + jw:hist + jw + tt, :]
    gts = _dot(u.astype(BF16), wbd_ref[...]) + gb_ref[...]
    r = _sigmoid(gts[:, :GW])
    ig = _sigmoid(gts[:, GW:])
    log_a = -LRU_C * r * _softplus(-lam_ref[...])
    a = jnp.exp(log_a)
    a_ref[...] = a
    b_ref[...] = jnp.sqrt(-jnp.tanh(log_a) * (a * a + 1.0)) * (ig * u)

    def step(t, h):
        h = a_ref[pl.ds(t, 1), :] * h + b_ref[pl.ds(t, 1), :]
        b_ref[pl.ds(t, 1), :] = h
        return h

    h = lax.fori_loop(0, tt, step, h_ref[...], unroll=8)
    h_ref[...] = h
    o_ref[...] = b_ref[...] * _gelu_tanh(gt_ref[...])
    tail = xp_ref[tt + hist:tt + CONV_PAD, :]
    xp_ref[hist:CONV_PAD, :] = tail

    @pl.when(s == pl.num_programs(1) - 1)
    def _():
        hl_ref[...] = h
        nb_ref[...] = tail


def _rg_lru(h_arr, buf, h0, conv_w, conv_b, wbd, gate_b, lam, *, layer):
    b, t = h_arr.shape[:2]
    tt = _pick(t, 512)
    x_blk, g_blk = C_LRUX // GW, C_LRUG // GW
    wspec = lambda shape: pl.BlockSpec((None,) + shape, lambda bi, s: (layer,) + (0,) * len(shape))
    return pl.pallas_call(
        functools.partial(_lru_kernel, tt=tt),
        grid=(b, t // tt),
        in_specs=[
            pl.BlockSpec((None, tt, GW), lambda bi, s: (bi, s, x_blk)),
            pl.BlockSpec((None, tt, GW), lambda bi, s: (bi, s, g_blk)),
            pl.BlockSpec((None, CONV_W - 1, GW), lambda bi, s: (bi, 0, 0)),
            pl.BlockSpec((None, 1, GW), lambda bi, s: (bi, 0, 0)),
            wspec((CONV_W, GW)), wspec((1, GW)), wspec((GW, 2 * GW)), wspec((1, 2 * GW)), wspec((1, GW)),
        ],
        out_specs=[
            pl.BlockSpec((None, tt, GW), lambda bi, s: (bi, s, 0)),
            pl.BlockSpec((None, 1, GW), lambda bi, s: (bi, 0, 0)),
            pl.BlockSpec((None, CONV_W - 1, GW), lambda bi, s: (bi, 0, 0)),
        ],
        out_shape=[
            jax.ShapeDtypeStruct((b, t, GW), F32),
            jax.ShapeDtypeStruct((b, 1, GW), F32),
            jax.ShapeDtypeStruct((b, CONV_W - 1, GW), F32),
        ],
        scratch_shapes=[
            pltpu.VMEM((CONV_PAD + tt, GW), F32),
            pltpu.VMEM((tt, GW), F32),
            pltpu.VMEM((tt, GW), F32),
            pltpu.VMEM((1, GW), F32),
        ],
        compiler_params=_cparams(("parallel", "arbitrary")),
        name="rg_lru",
    )(h_arr, h_arr, buf, h0, conv_w, conv_b, wbd, gate_b, lam)


def _gdn_kernel(qkv_ref, z_ref, sm_ref, buf_ref, s0_ref, cw_ref, al_ref, dtb_ref, ng_ref,
                o_ref, sout_ref, nb_ref, xp_ref, smp_ref, st_ref, *, tt, tc):
    s = pl.program_id(1)
    hist = CONV_PAD - (CONV_W - 1)
    n_chunks = tc // GDN_CHUNK

    @pl.when(s == 0)
    def _():
        xp_ref[...] = jnp.zeros_like(xp_ref)
        smp_ref[...] = jnp.zeros_like(smp_ref)
        xp_ref[hist:CONV_PAD, :] = buf_ref[...]
        st_ref[...] = s0_ref[...]

    xp_ref[CONV_PAD:CONV_PAD + tt, :] = qkv_ref[...]
    smp_ref[0:tt, :] = sm_ref[...]
    row_ok = lax.broadcasted_iota(jnp.int32, (tc, 1), 0) < tt
    y = cw_ref[0:1, :] * xp_ref[hist:hist + tc, :]
    for jw in range(1, CONV_W):
        y = y + cw_ref[jw:jw + 1, :] * xp_ref[hist + jw:hist + jw + tc, :]
    y = jnp.where(row_ok, _silu(y), 0.0)
    sm = smp_ref[...]
    beta_all = jnp.where(row_ok, _sigmoid(sm), 0.0)
    g_all = jnp.where(row_ok, -jnp.exp(al_ref[...]) * _softplus(sm + dtb_ref[...]), 0.0)

    ri = lax.broadcasted_iota(jnp.int32, (tc, tc), 0)
    ci = lax.broadcasted_iota(jnp.int32, (tc, tc), 1)
    same = (ri // GDN_CHUNK) == (ci // GDN_CHUNK)
    incl = same & (ci <= ri)
    strict = same & (ci < ri)
    ones_where = lambda m: jnp.where(m, 1.0, 0.0).astype(BF16)
    gcum = _dot_hp_exact_lhs(ones_where(incl), g_all)
    gcum_t = _dot_hp_exact_rhs(g_all.T, ones_where(same & (ri <= ci)))
    gtot = _dot_hp_exact_lhs(ones_where(same), g_all)
    eye = jnp.where(ri == ci, 1.0, 0.0)

    for h in range(GDN_H):
        q = y[:, h * GDN_D:(h + 1) * GDN_D]
        k = y[:, GW + h * GDN_D:GW + (h + 1) * GDN_D]
        v = y[:, 2 * GW + h * GDN_D:2 * GW + (h + 1) * GDN_D]
        q = q * lax.rsqrt(jnp.sum(q * q, axis=-1, keepdims=True) + RMS_EPS) * (GDN_D ** -0.5)
        k = k * lax.rsqrt(jnp.sum(k * k, axis=-1, keepdims=True) + RMS_EPS)
        beta = beta_all[:, SM_BETA + h:SM_BETA + h + 1]
        gc = gcum[:, SM_DECAY + h:SM_DECAY + h + 1]
        gr = gcum_t[SM_DECAY + h:SM_DECAY + h + 1, :]
        gt = gtot[:, SM_DECAY + h:SM_DECAY + h + 1]
        decay = jnp.exp(jnp.where(incl, gc - gr, NEG))
        kb = k * beta
        kbf = k.astype(BF16)
        a_mat = jnp.where(strict, _dot_nt(kb.astype(BF16), kbf) * decay, 0.0)
        qk = jnp.where(incl, _dot_nt(q.astype(BF16), kbf) * decay, 0.0)
        pw = -a_mat
        tinv = eye + pw
        for _ in range(int(math.log2(GDN_CHUNK)) - 1):
            pw = _dot_hp(pw, pw)
            tinv = tinv + _dot_hp(tinv, pw)
        eg = jnp.exp(gc)
        sol = _dot_hp(tinv, jnp.concatenate([v * beta, kb * eg], axis=1))
        u, w = sol[:, :GDN_D], sol[:, GDN_D:]
        qg = q * eg
        kd = k * jnp.exp(gt - gc)
        st = st_ref[h]
        outs = []
        for c in range(n_chunks):
            rows = slice(c * GDN_CHUNK, (c + 1) * GDN_CHUNK)
            stb = st.astype(BF16)
            v_new = u[rows] - _dot(w[rows].astype(BF16), stb)
            v_new_b = v_new.astype(BF16)
            outs.append(_dot(qg[rows].astype(BF16), stb) + _dot(qk[rows, rows].astype(BF16), v_new_b))
            g_last = jnp.exp(gt[c * GDN_CHUNK:c * GDN_CHUNK + 1, :])
            st = st * g_last + _dot_tn(kd[rows].astype(BF16), v_new_b)
        st_ref[h] = st
        o = outs[0] if n_chunks == 1 else jnp.concatenate(outs, axis=0)
        o = o * lax.rsqrt(jnp.mean(o * o, axis=-1, keepdims=True) + RMS_EPS) * ng_ref[...]
        o_ref[:, h * GDN_D:(h + 1) * GDN_D] = o[0:tt] * _silu(z_ref[:, h * GDN_D:(h + 1) * GDN_D])

    tail = xp_ref[tt + hist:tt + CONV_PAD, :]
    xp_ref[hist:CONV_PAD, :] = tail

    @pl.when(s == pl.num_programs(1) - 1)
    def _():
        sout_ref[...] = st_ref[...]
        nb_ref[...] = tail


def _gated_deltanet(h_arr, buf, s0, conv_w, a_log_row, dt_bias_row, norm_g, *, layer):
    b, t = h_arr.shape[:2]
    tt = _pick(t, 256)
    tc = -(-tt // GDN_CHUNK) * GDN_CHUNK
    assert tc == tt or t == tt, "sequence length must be a multiple of the chunk unless it fits one block"
    w3 = 3 * GW
    wspec = lambda shape: pl.BlockSpec((None,) + shape, lambda bi, s: (layer,) + (0,) * len(shape))
    return pl.pallas_call(
        functools.partial(_gdn_kernel, tt=tt, tc=tc),
        grid=(b, t // tt),
        in_specs=[
            pl.BlockSpec((None, tt, w3), lambda bi, s: (bi, s, C_GQKV // w3)),
            pl.BlockSpec((None, tt, GW), lambda bi, s: (bi, s, C_GZ // GW)),
            pl.BlockSpec((None, tt, LANES), lambda bi, s: (bi, s, C_SMALL // LANES)),
            pl.BlockSpec((None, CONV_W - 1, w3), lambda bi, s: (bi, 0, 0)),
            pl.BlockSpec((None, GDN_H, GDN_D, GDN_D), lambda bi, s: (bi, 0, 0, 0)),
            wspec((CONV_W, w3)), wspec((1, LANES)), wspec((1, LANES)), wspec((1, GDN_D)),
        ],
        out_specs=[
            pl.BlockSpec((None, tt, GW), lambda bi, s: (bi, s, 0)),
            pl.BlockSpec((None, GDN_H, GDN_D, GDN_D), lambda bi, s: (bi, 0, 0, 0)),
            pl.BlockSpec((None, CONV_W - 1, w3), lambda bi, s: (bi, 0, 0)),
        ],
        out_shape=[
            jax.ShapeDtypeStruct((b, t, GW), F32),
            jax.ShapeDtypeStruct((b, GDN_H, GDN_D, GDN_D), F32),
            jax.ShapeDtypeStruct((b, CONV_W - 1, w3), F32),
        ],
        scratch_shapes=[
            pltpu.VMEM((CONV_PAD + tc, w3), F32),
            pltpu.VMEM((tc, LANES), F32),
            pltpu.VMEM((GDN_H, GDN_D, GDN_D), F32),
        ],
        compiler_params=_cparams(("parallel", "arbitrary")),
        name="gated_deltanet",
    )(h_arr, h_arr, h_arr, buf, s0, conv_w, a_log_row, dt_bias_row, norm_g)


def _outproj_kernel(oa_ref, ob_ref, oc_ref, od_ref, x_ref, w_ref, gg_ref, g_ref, b_ref, o_ref, *, alpha):
    def rms(v, gain):
        return (v * lax.rsqrt(jnp.mean(v * v, axis=-1, keepdims=True) + RMS_EPS) * gain).astype(BF16)

    acc = _dot(rms(oa_ref[...], gg_ref[0:1, :]), w_ref[0:GW, :])
    acc += _dot(rms(ob_ref[...], gg_ref[1:2, :]), w_ref[GW:2 * GW, :])
    acc += _dot(rms(oc_ref[...], gg_ref[2:3, :]), w_ref[2 * GW:3 * GW, :])
    acc += _dot(od_ref[...].astype(BF16), w_ref[3 * GW:4 * GW, :])
    o_ref[...] = _layer_norm(alpha * x_ref[...] + acc, g_ref[...], b_ref[...])


def _outproj_ln(oa, ob, oc, od, x, w_out, grp_g, ln_g, ln_b, *, layer, alpha):
    n, d = x.shape
    tm = _pick(n, 512)
    mix_spec = pl.BlockSpec((tm, GW), lambda i: (i, 0))
    return pl.pallas_call(
        functools.partial(_outproj_kernel, alpha=alpha),
        grid=(n // tm,),
        in_specs=[
            mix_spec, mix_spec, mix_spec, mix_spec,
            pl.BlockSpec((tm, d), lambda i: (i, 0)),
            pl.BlockSpec((None, 4 * GW, d), lambda i: (layer, 0, 0)),
            pl.BlockSpec((None, 3, GW), lambda i: (layer, 0, 0)),
            pl.BlockSpec((None, None, 1, d), lambda i: (layer, 1, 0, 0)),
            pl.BlockSpec((None, None, 1, d), lambda i: (layer, 1, 0, 0)),
        ],
        out_specs=pl.BlockSpec((tm, d), lambda i: (i, 0)),
        out_shape=jax.ShapeDtypeStruct((n, d), F32),
        compiler_params=_cparams(("parallel",)),
        name="outproj_ln",
    )(oa, ob, oc, od, x, w_out, grp_g, ln_g, ln_b)


def _prep_weights(ffn_gu, ffn_down, w_in, w_out, nsa_phi, lru_gate_w, lru_gate_b, gdn_A_log, gdn_dt_bias):
    depth, d_model, _ = w_in.shape
    o_nsakv, o_gate, o_lrux, o_gqkv, o_gz, o_gb = 2048, 2816, 2840, 3864, 5400, 5912
    zeros = jnp.zeros((depth, d_model, P_IN_PAD - 5920), w_in.dtype)
    w_in_p = jnp.concatenate([
        w_in[..., 512:1536],
        w_in[..., 0:512],
        w_in[..., 1536:2048],
        w_in[..., o_nsakv:o_nsakv + 512],
        w_in[..., o_lrux:o_lrux + 1024],
        w_in[..., o_gz:o_gz + 512],
        w_in[..., o_nsakv + 512:o_gate],
        w_in[..., o_gate:o_lrux],
        w_in[..., o_gb:o_gb + 8],
        zeros,
        w_in[..., o_gqkv:o_gz],
    ], axis=-1).astype(BF16)
    eye_c = jnp.eye(2, dtype=F32)
    eye_g = jnp.eye(NSA_G, dtype=F32)
    phi_big = jnp.einsum("lcjde,cx,gy->ljcgdxye", nsa_phi, eye_c, eye_g)
    phi_big = phi_big.reshape(depth, CMP_BLOCK * 2 * NSA_G * HD, 2 * NSA_G * HD).astype(BF16)
    eye_b = jnp.eye(LRU_BLKS, dtype=F32)
    wbd = jnp.einsum("lknce,nm->lnckme", lru_gate_w, eye_b).reshape(depth, GW, 2 * GW).astype(BF16)
    gate_b = lru_gate_b.reshape(depth, 1, 2 * GW)
    pad_l = jnp.zeros((depth, SM_DECAY), F32)
    pad_r = jnp.zeros((depth, LANES - SM_DECAY - GDN_H), F32)
    a_log_row = jnp.concatenate([pad_l, gdn_A_log, pad_r], axis=1)[:, None, :]
    dt_bias_row = jnp.concatenate([pad_l, gdn_dt_bias, pad_r], axis=1)[:, None, :]
    return dict(
        wgu=ffn_gu.astype(BF16), wd=ffn_down.astype(BF16), w_in=w_in_p, w_out=w_out.astype(BF16),
        phi_big=phi_big, wbd=wbd, gate_b=gate_b, a_log_row=a_log_row, dt_bias_row=dt_bias_row)


def _trunk_layer(x, b, t, q0, past, wts, layer, alpha):
    ln_g, ln_b = wts["ln_g"], wts["ln_b"]
    x = _ffn_ln(x, wts["wgu"], wts["wd"], ln_g, ln_b, layer=layer, which=0, ln_idx=0, alpha=alpha)
    h = _matmul(x, wts["w_in"], layer=layer, name="in_proj")
    h3 = h.reshape(b, t, P_IN_PAD)

    sb_rows = h3[:, :, C_SBK:C_SBK + 2 * GW]
    nsa_rows = h3[:, :, C_NSAKV:C_NSAKV + GW]
    nsa_win_new = h3[:, :, C_NSAW:C_NSAW + 2 * LANES]

    if past["page_table"] is None:
        lw = 0
        o_a = _sb_attend(h3, C_SBQ // GW, h3, C_SBK // GW, h3, C_SBV // GW, q0=q0)
        cmp_rows = h3[:, :, C_NSAKV:C_NSAKV + 2 * LANES]
        ks_arr, ks_blk = h3, (C_NSAKV + 2 * LANES) // (2 * LANES)
        win_all = nsa_win_new
    else:
        pt = past["page_table"]
        lw = past["nsa_win"].shape[2]
        o_a = _sb_attend_paged(h3, C_SBQ // GW, C_SBK // (2 * GW), past["sb_kv"], pt, layer=layer)
        cmp_rows, ks_arr = _gather_nsa_pages(past["nsa_kv"], pt, h3, C_NSAKV // GW, layer=layer)
        ks_blk = 0
        win_all = jnp.concatenate([past["nsa_win"][layer], nsa_win_new], axis=1)

    l_cmp = cmp_rows.shape[1]
    nb_arr = l_cmp // CMP_BLOCK
    kc = _matmul(cmp_rows.reshape(b * nb_arr, CMP_BLOCK * 2 * LANES), wts["phi_big"], layer=layer,
                 tm_pref=512, tn_pref=256, name="nsa_compress")
    kc = kc.reshape(b, nb_arr, 2 * LANES)
    nbp = -(-nb_arr // LANES) * LANES
    kc = jnp.pad(kc, ((0, 0), (0, nbp - nb_arr), (0, 0)))
    o_b = _nsa_attend(h3, C_NSAQ // GW, h3, C_SMALL // LANES, kc, ks_arr, ks_blk, win_all,
                      q0=q0, lw=lw, n_real_keys=q0 + t)
    new_win = win_all[:, -min(WINDOW, lw + t):]

    o_c, h_last, new_lru_buf = _rg_lru(h3, past["lru_conv"], past["lru_h"], wts["lru_conv_w"], wts["lru_conv_b"],
                                       wts["wbd"], wts["gate_b"], wts["lru_lambda"], layer=layer)
    o_d, s_new, new_gdn_buf = _gated_deltanet(h3, past["gdn_conv"], past["gdn_S"], wts["gdn_conv_w"],
                                              wts["a_log_row"], wts["dt_bias_row"], wts["gdn_norm_g"], layer=layer)

    n = b * t
    x = _outproj_ln(o_a.reshape(n, GW), o_b.reshape(n, GW), o_c.reshape(n, GW), o_d.reshape(n, GW), x,
                    wts["w_out"], wts["grp_norm_g"], ln_g, ln_b, layer=layer, alpha=alpha)
    x = _ffn_ln(x, wts["wgu"], wts["wd"], ln_g, ln_b, layer=layer, which=1, ln_idx=2, alpha=alpha)
    new_state = (
        sb_rows.reshape(b, t, 2, SB_H, HD),
        nsa_rows.reshape(b, t, 4, NSA_G, HD),
        new_win.reshape(b, new_win.shape[1], 2, NSA_G, HD),
        h_last.reshape(b, GW),
        new_lru_buf,
        s_new,
        new_gdn_buf,
    )
    return x, new_state


def kernel(x_prompt, x_sample, cache_sb_kv, cache_nsa_kv, cache_nsa_win, state_lru_h, state_lru_conv,
           state_gdn_S, state_gdn_conv, page_table, ln_g, ln_b, ffn_gu, ffn_down, w_in, w_out, grp_norm_g,
           nsa_phi, lru_conv_w, lru_conv_b, lru_gate_w, lru_gate_b, lru_lambda, gdn_conv_w, gdn_A_log,
           gdn_dt_bias, gdn_norm_g):
    depth, d_model = w_in.shape[0], w_in.shape[1]
    alpha = (2 * depth) ** 0.25
    n_b, seq = x_prompt.shape[:2]
    n_db, dec_seq = x_sample.shape[:2]
    past_len = page_table.shape[1] * PAGE
    n_pool = cache_sb_kv.shape[1]

    wts = _prep_weights(ffn_gu, ffn_down, w_in, w_out, nsa_phi, lru_gate_w, lru_gate_b, gdn_A_log, gdn_dt_bias)
    wts.update(
        ln_g=ln_g.reshape(depth, 3, 1, d_model), ln_b=ln_b.reshape(depth, 3, 1, d_model),
        grp_norm_g=grp_norm_g, lru_conv_w=lru_conv_w, lru_conv_b=lru_conv_b.reshape(depth, 1, GW),
        lru_lambda=lru_lambda.reshape(depth, 1, GW), gdn_conv_w=gdn_conv_w,
        gdn_norm_g=gdn_norm_g.reshape(depth, 1, GDN_D))

    sb_cache = jnp.transpose(cache_sb_kv, (0, 1, 3, 4, 5, 2))
    nsa_cache = jnp.transpose(cache_nsa_kv, (0, 1, 3, 4, 5, 2))
    nsa_win = cache_nsa_win.reshape(depth, n_db, cache_nsa_win.shape[2], 2 * LANES)

    y_p = x_prompt.reshape(n_b * seq, d_model)
    y_s = x_sample.reshape(n_db * dec_seq, d_model)
    st_p, st_s = [], []
    for l in range(depth):
        past_p = dict(page_table=None,
                      lru_h=jnp.zeros((n_b, 1, GW), F32), lru_conv=jnp.zeros((n_b, CONV_W - 1, GW), F32),
                      gdn_S=jnp.zeros((n_b, GDN_H, GDN_D, GDN_D), F32),
                      gdn_conv=jnp.zeros((n_b, CONV_W - 1, 3 * GW), F32))
        past_s = dict(page_table=page_table, sb_kv=sb_cache, nsa_kv=nsa_cache, nsa_win=nsa_win,
                      lru_h=state_lru_h[l].reshape(n_db, 1, GW), lru_conv=state_lru_conv[l],
                      gdn_S=state_gdn_S[l], gdn_conv=state_gdn_conv[l])
        y_p, new_p = _trunk_layer(y_p, n_b, seq, 0, past_p, wts, l, alpha)
        y_s, new_s = _trunk_layer(y_s, n_db, dec_seq, past_len, past_s, wts, l, alpha)
        st_p.append(new_p)
        st_s.append(new_s)
    p = [jnp.stack(a) for a in zip(*st_p)]
    s = [jnp.stack(a) for a in zip(*st_s)]
    return (y_p.reshape(n_b, seq, d_model), y_s.reshape(n_db, dec_seq, d_model),
            p[0], p[1], p[2], p[3], p[4], p[5], p[6], s[0], s[1], s[2], s[3], s[4], s[5], s[6])
```

```python
import functools
import math

import numpy as np
import jax
import jax.numpy as jnp
from jax import lax
from jax.experimental import pallas as pl
from jax.experimental.pallas import tpu as pltpu

F32 = jnp.float32
BF16 = jnp.bfloat16
HIGHEST = lax.Precision.HIGHEST

PAGE = 128
GW = 512
HD = 64
SB_H = 8
NSA_G = 2
NSA_R = 4
CMP_BLOCK = 64
SEL_TOPN = 16
SEL_FORCE = float(NSA_R + 1)
WINDOW = 512
LRU_BLKS = 8
LRU_C = 8.0
CONV_W = 4
GDN_H = 4
GDN_D = 128
GDN_CHUNK = 64
LN_EPS = 1e-5
RMS_EPS = 1e-6
NEG = -1e30
LOG2E = 1.4426950408889634
QSCALE2 = (HD ** -0.5) * LOG2E

LANES = 128
SUBLANES = 8
VMEM_LIMIT = 56 * 1024 * 1024

C_SBK, C_SBV, C_SBQ, C_NSAQ, C_NSAKV = 0, 512, 1024, 1536, 2048
C_LRUX, C_LRUG, C_GZ, C_NSAW, C_SMALL, C_GQKV = 2560, 3072, 3584, 4096, 4352, 4608
P_IN_PAD = 6144
SM_GATE, SM_BETA, SM_DECAY = 0, 24, 28
CONV_PAD = 8


def _pick(n, pref, mult=SUBLANES):
    if n <= pref:
        return n
    for t in range(pref, 0, -1):
        if n % t == 0 and t % mult == 0:
            return t
    return n


def _cparams(sem):
    return pltpu.CompilerParams(dimension_semantics=sem, vmem_limit_bytes=VMEM_LIMIT)


def _sigmoid(x):
    return 1.0 / (1.0 + jnp.exp(-x))


def _silu(x):
    return x * _sigmoid(x)


def _softplus(x):
    return jnp.maximum(x, 0.0) + jnp.log1p(jnp.exp(-jnp.abs(x)))


def _log_sigmoid(x):
    return jnp.minimum(x, 0.0) - jnp.log1p(jnp.exp(-jnp.abs(x)))


def _gelu_tanh(x):
    return 0.5 * x * (1.0 + jnp.tanh(math.sqrt(2.0 / math.pi) * (x + 0.044715 * (x * x * x))))


def _layer_norm(y, g, b):
    mu = jnp.mean(y, axis=-1, keepdims=True)
    d = y - mu
    var = jnp.mean(d * d, axis=-1, keepdims=True)
    return d * lax.rsqrt(var + LN_EPS) * g + b


def _dot(a, b):
    return jnp.dot(a, b, preferred_element_type=F32)


def _dot_nt(a, b):
    return lax.dot_general(a, b, (((1,), (1,)), ((), ())), preferred_element_type=F32)


def _dot_tn(a, b):
    return lax.dot_general(a, b, (((0,), (0,)), ((), ())), preferred_element_type=F32)


def _split_bf16(x):
    hi = x.astype(BF16)
    return hi, (x - hi.astype(F32)).astype(BF16)


def _dot_hp(a, b):
    ah, al = _split_bf16(a)
    bh, bl = _split_bf16(b)
    return _dot(jnp.concatenate([ah, ah, al], axis=1), jnp.concatenate([bh, bl, bh], axis=0))


def _dot_hp_exact_lhs(a_bf16, b):
    bh, bl = _split_bf16(b)
    return _dot(jnp.concatenate([a_bf16, a_bf16], axis=1), jnp.concatenate([bh, bl], axis=0))


def _dot_hp_exact_rhs(a, b_bf16):
    ah, al = _split_bf16(a)
    return _dot(jnp.concatenate([ah, al], axis=1), jnp.concatenate([b_bf16, b_bf16], axis=0))


def _ffn_ln_kernel(x_ref, wg_ref, wu_ref, wd_ref, g_ref, b_ref, o_ref, xb_ref, *, alpha):
    j = pl.program_id(1)

    @pl.when(j == 0)
    def _():
        xb_ref[...] = x_ref[...].astype(BF16)
        o_ref[...] = jnp.zeros_like(o_ref)

    xb = xb_ref[...]
    gate = _dot(xb, wg_ref[...])
    up = _dot(xb, wu_ref[...])
    act = (_silu(gate) * up).astype(BF16)
    o_ref[...] += _dot(act, wd_ref[...])

    @pl.when(j == pl.num_programs(1) - 1)
    def _():
        y = alpha * x_ref[...] + 0.5 * o_ref[...]
        o_ref[...] = _layer_norm(y, g_ref[...], b_ref[...])


def _ffn_ln(x, wgu, wd, ln_g, ln_b, *, layer, which, ln_idx, alpha):
    n, d = x.shape
    f = wd.shape[2]
    tm = _pick(n, 1024)
    tf = _pick(f, 256, LANES)
    nf = f // tf
    return pl.pallas_call(
        functools.partial(_ffn_ln_kernel, alpha=alpha),
        grid=(n // tm, nf),
        in_specs=[
            pl.BlockSpec((tm, d), lambda i, j: (i, 0)),
            pl.BlockSpec((None, None, d, tf), lambda i, j: (layer, which, 0, j)),
            pl.BlockSpec((None, None, d, tf), lambda i, j: (layer, which, 0, j + nf)),
            pl.BlockSpec((None, None, tf, d), lambda i, j: (layer, which, j, 0)),
            pl.BlockSpec((None, None, 1, d), lambda i, j: (layer, ln_idx, 0, 0)),
            pl.BlockSpec((None, None, 1, d), lambda i, j: (layer, ln_idx, 0, 0)),
        ],
        out_specs=pl.BlockSpec((tm, d), lambda i, j: (i, 0)),
        out_shape=jax.ShapeDtypeStruct((n, d), F32),
        scratch_shapes=[pltpu.VMEM((tm, d), BF16)],
        compiler_params=_cparams(("parallel", "arbitrary")),
        name="ffn_ln",
    )(x, wgu, wgu, wd, ln_g, ln_b)


def _matmul_kernel(x_ref, w_ref, o_ref):
    k = pl.program_id(2)

    @pl.when(k == 0)
    def _():
        o_ref[...] = jnp.zeros_like(o_ref)

    o_ref[...] += _dot(x_ref[...].astype(BF16), w_ref[...])


def _matmul(x, w, *, layer, tm_pref=1024, tn_pref=512, tk_pref=2048, name="matmul"):
    m, kdim = x.shape
    nout = w.shape[2]
    tm = _pick(m, tm_pref)
    tn = _pick(nout, tn_pref, LANES)
    tk = _pick(kdim, tk_pref, LANES)
    return pl.pallas_call(
        _matmul_kernel,
        grid=(m // tm, nout // tn, kdim // tk),
        in_specs=[
            pl.BlockSpec((tm, tk), lambda i, j, k: (i, k)),
            pl.BlockSpec((None, tk, tn), lambda i, j, k: (layer, k, j)),
        ],
        out_specs=pl.BlockSpec((tm, tn), lambda i, j, k: (i, j)),
        out_shape=jax.ShapeDtypeStruct((m, nout), F32),
        compiler_params=_cparams(("parallel", "parallel", "arbitrary")),
        name=name,
    )(x, w)


def _gather_nsa_kernel(pt_ref, *refs, n_group, t_new):
    del pt_ref
    page_refs = refs[:n_group]
    new_ref = refs[n_group]
    cmp_ref, sel_ref = refs[n_group + 1:]
    s = pl.program_id(1)
    last = pl.num_programs(1) - 1

    @pl.when(s < last)
    def _():
        for g in range(n_group):
            rows = slice(g * PAGE, (g + 1) * PAGE)
            for kind in range(4):
                x = page_refs[g][kind].reshape(NSA_G * HD, PAGE).T
                o_ref = cmp_ref if kind < 2 else sel_ref
                o_ref[rows, (kind % 2) * LANES:(kind % 2 + 1) * LANES] = x

    @pl.when(s == last)
    def _():
        for o_ref, c0 in ((cmp_ref, 0), (sel_ref, 2 * LANES)):
            o_ref[...] = jnp.zeros_like(o_ref)
            o_ref[0:t_new, :] = new_ref[:, c0:c0 + 2 * LANES]


def _gather_nsa_pages(cache_t, page_table, new_arr, new_colblk, *, layer):
    b, n_pages = page_table.shape
    t_new = new_arr.shape[1]
    n_group = math.gcd(n_pages, 8)
    n_steps = n_pages // n_group + 1
    rows = n_group * PAGE

    def page_map(g):
        return lambda bi, s, pt: (layer, pt[bi, jnp.minimum(s * n_group + g, n_pages - 1)], 0, 0, 0, 0)

    in_specs = [pl.BlockSpec((None, None, 4, NSA_G, HD, PAGE), page_map(g)) for g in range(n_group)]
    in_specs.append(pl.BlockSpec((None, t_new, GW), lambda bi, s, pt: (bi, 0, new_colblk)))
    out_specs = [pl.BlockSpec((None, rows, 2 * LANES), lambda bi, s, pt: (bi, s, 0)) for _ in range(2)]
    out_shape = [jax.ShapeDtypeStruct((b, n_steps * rows, 2 * LANES), F32) for _ in range(2)]
    return pl.pallas_call(
        functools.partial(_gather_nsa_kernel, n_group=n_group, t_new=t_new),
        grid_spec=pltpu.PrefetchScalarGridSpec(
            num_scalar_prefetch=1, grid=(b, n_steps), in_specs=in_specs, out_specs=out_specs),
        out_shape=out_shape,
        compiler_params=_cparams(("parallel", "arbitrary")),
        name="gather_nsa_pages",
    )(page_table, *([cache_t] * n_group), new_arr)


def _suffix_matrix(n):
    j = lax.broadcasted_iota(jnp.int32, (2 * n, n), 0)
    s = lax.broadcasted_iota(jnp.int32, (2 * n, n), 1)
    return ((j > s) & ((j < n) | (j - n > s))).astype(BF16)


def _sb_neg_log(z2, mask):
    nl = jnp.maximum(z2, 0.0) + jnp.log2(1.0 + jnp.exp2(-jnp.abs(z2)))
    if mask is not None:
        nl = jnp.where(mask, nl, 0.0)
    hi = nl.astype(BF16)
    lo = (nl - hi.astype(F32)).astype(BF16)
    return nl, jnp.concatenate([hi, lo], axis=1)


def _sb_weight(z2, nl, tail, carry, mask):
    w = jnp.exp2(z2 - nl - tail - carry)
    if mask is not None:
        w = jnp.where(mask, w, 0.0)
    return w.astype(BF16)


def _sb_weights(z2, tri2, carry, mask):
    nl, hilo = _sb_neg_log(z2, mask)
    tail = _dot(hilo, tri2)
    return _sb_weight(z2, nl, tail, carry, mask), tail[:, 0:1] + nl[:, 0:1]


def _sb_kernel(q_ref, k_ref, v_ref, o_ref, acc_ref, c_ref, *, tq, tk, q0):
    i = pl.program_id(1)
    j = pl.program_id(2)
    jt = (q0 + (i + 1) * tq - 2) // tk - j
    qs0 = q0 + i * tq

    @pl.when(j == 0)
    def _():
        acc_ref[...] = jnp.zeros_like(acc_ref)
        c_ref[...] = jnp.zeros_like(c_ref)

    def tile(masked):
        mask = None
        if masked:
            qpos = qs0 + lax.broadcasted_iota(jnp.int32, (tq, tk), 0)
            kpos = jt * tk + lax.broadcasted_iota(jnp.int32, (tq, tk), 1)
            mask = kpos < qpos
        tri = _suffix_matrix(tk)
        lane_half = lax.broadcasted_iota(jnp.int32, (tq, LANES), 1) // HD
        cols = [slice((h // 2) * LANES, (h // 2 + 1) * LANES) for h in range(SB_H)]
        zs = []
        for h in range(SB_H):
            qc = jnp.where(lane_half == h % 2, q_ref[:, cols[h]] * QSCALE2, 0.0).astype(BF16)
            zs.append(_dot_nt(qc, k_ref[:, cols[h]].astype(BF16)))
        nls, hilos = [], []
        for h in range(SB_H):
            nl, hilo = _sb_neg_log(zs[h], mask)
            nls.append(nl)
            hilos.append(hilo)
        tails = [_dot(hilos[h], tri) for h in range(SB_H)]
        ws = []
        for h in range(SB_H):
            ws.append(_sb_weight(zs[h], nls[h], tails[h], c_ref[h], mask))
            c_ref[h] += tails[h][:, 0:1] + nls[h][:, 0:1]
        for h in range(SB_H):
            acc_ref[h] += _dot(ws[h], v_ref[:, cols[h]].astype(BF16))

    whole = (jt + 1) * tk <= qs0

    @pl.when((jt >= 0) & whole)
    def _():
        tile(False)

    @pl.when((jt >= 0) & jnp.logical_not(whole))
    def _():
        tile(True)

    @pl.when(j == pl.num_programs(2) - 1)
    def _():
        lane_half = lax.broadcasted_iota(jnp.int32, (tq, LANES), 1) // HD
        for c in range(SB_H // 2):
            o_ref[:, c * LANES:(c + 1) * LANES] = jnp.where(lane_half == 0, acc_ref[2 * c], acc_ref[2 * c + 1])


def _sb_attend(q_arr, q_blk, k_arr, k_blk, v_arr, v_blk, *, q0):
    b, t = q_arr.shape[:2]
    l = k_arr.shape[1]
    tq = _pick(t, 256)
    tk = _pick(l, 256, LANES)
    nq = t // tq
    nk = (q0 + t - 2) // tk + 1

    def kv_map(blk):
        return lambda bi, i, j: (bi, jnp.maximum((q0 + (i + 1) * tq - 2) // tk - j, 0), blk)

    return pl.pallas_call(
        functools.partial(_sb_kernel, tq=tq, tk=tk, q0=q0),
        grid=(b, nq, nk),
        in_specs=[
            pl.BlockSpec((None, tq, GW), lambda bi, i, j: (bi, i, q_blk)),
            pl.BlockSpec((None, tk, GW), kv_map(k_blk)),
            pl.BlockSpec((None, tk, GW), kv_map(v_blk)),
        ],
        out_specs=pl.BlockSpec((None, tq, GW), lambda bi, i, j: (bi, i, 0)),
        out_shape=jax.ShapeDtypeStruct((b, t, GW), F32),
        scratch_shapes=[pltpu.VMEM((SB_H, tq, LANES), F32), pltpu.VMEM((SB_H, tq, 1), F32)],
        compiler_params=_cparams(("parallel", "parallel", "arbitrary")),
        name="sb_attend",
    )(q_arr, k_arr, v_arr)


def _sb_paged_kernel(pt_ref, q_ref, new_ref, *refs, n_group, t):
    del pt_ref
    page_refs = refs[:n_group]
    o_ref, qbd_ref, acc_ref, c_ref = refs[n_group:]
    s = pl.program_id(1)
    m = SB_H * t
    tri = _suffix_matrix(PAGE)

    @pl.when(s == 0)
    def _():
        lane_head = lax.broadcasted_iota(jnp.int32, (t, GW), 1) // HD
        q = q_ref[...] * QSCALE2
        for h in range(SB_H):
            qbd_ref[h * t:(h + 1) * t, :] = jnp.where(lane_head == h, q, 0.0)
        pad = jnp.zeros((PAGE - t, GW), F32)
        k_new = jnp.concatenate([new_ref[:, 0:GW], pad], axis=0).astype(BF16)
        v_new = jnp.concatenate([new_ref[:, GW:2 * GW], pad], axis=0).astype(BF16)
        row = lax.broadcasted_iota(jnp.int32, (m, PAGE), 0)
        mask = lax.broadcasted_iota(jnp.int32, (m, PAGE), 1) < row - (row // t) * t
        w, rs = _sb_weights(_dot_nt(qbd_ref[...].astype(BF16), k_new), tri, 0.0, mask)
        acc_ref[...] = _dot(w, v_new)
        c_ref[...] = rs

    qb = qbd_ref[...].astype(BF16)
    pages = range(n_group)
    zs = [_dot(qb, page_refs[g][0].reshape(GW, PAGE).astype(BF16)) for g in pages]
    nls, hilos = zip(*[_sb_neg_log(zs[g], None) for g in pages])
    tails = [_dot(hilos[g], tri) for g in pages]
    carries = [None] * n_group
    carry = c_ref[...]
    for g in reversed(pages):
        carries[g] = carry
        carry = carry + tails[g][:, 0:1] + nls[g][:, 0:1]
    c_ref[...] = carry
    ws = [_sb_weight(zs[g], nls[g], tails[g], carries[g], None) for g in pages]
    acc = acc_ref[...]
    for g in pages:
        acc = acc + _dot_nt(ws[g], page_refs[g][1].reshape(GW, PAGE).astype(BF16))
    acc_ref[...] = acc

    @pl.when(s == pl.num_programs(1) - 1)
    def _():
        lane_head = lax.broadcasted_iota(jnp.int32, (t, GW), 1) // HD
        out = jnp.zeros((t, GW), F32)
        for h in range(SB_H):
            out = out + jnp.where(lane_head == h, acc_ref[h * t:(h + 1) * t, :], 0.0)
        o_ref[...] = out


def _sb_attend_paged(q_arr, q_blk, new_blk, cache_t, page_table, *, layer):
    b, t = q_arr.shape[:2]
    n_pages = page_table.shape[1]
    n_group = math.gcd(n_pages, 8)
    n_steps = n_pages // n_group
    m = SB_H * t

    def page_map(g):
        return lambda bi, s, pt: (layer, pt[bi, n_pages - (s + 1) * n_group + g], 0, 0, 0, 0)

    in_specs = [
        pl.BlockSpec((None, t, GW), lambda bi, s, pt: (bi, 0, q_blk)),
        pl.BlockSpec((None, t, 2 * GW), lambda bi, s, pt: (bi, 0, new_blk)),
    ] + [pl.BlockSpec((None, None, 2, SB_H, HD, PAGE), page_map(g)) for g in range(n_group)]
    return pl.pallas_call(
        functools.partial(_sb_paged_kernel, n_group=n_group, t=t),
        grid_spec=pltpu.PrefetchScalarGridSpec(
            num_scalar_prefetch=1, grid=(b, n_steps), in_specs=in_specs,
            out_specs=pl.BlockSpec((None, t, GW), lambda bi, s, pt: (bi, 0, 0)),
            scratch_shapes=[pltpu.VMEM((m, GW), F32), pltpu.VMEM((m, GW), F32), pltpu.VMEM((m, 1), F32)]),
        out_shape=jax.ShapeDtypeStruct((b, t, GW), F32),
        compiler_params=_cparams(("parallel", "arbitrary")),
        name="sb_attend_paged",
    )(page_table, q_arr, q_arr, *([cache_t] * n_group))


def _masked_softmax(s2, mask):
    sm = jnp.where(mask, s2, NEG)
    m = jnp.max(sm, axis=-1, keepdims=True)
    e = jnp.where(mask, jnp.exp2(sm - m), 0.0)
    return e / jnp.maximum(jnp.sum(e, axis=-1, keepdims=True), 1e-30)


def _compress_kernel(x_ref, w_ref, o_ref):
    width = 2 * LANES
    parts = [_dot(x_ref[:, j, :].astype(BF16), w_ref[j * width:(j + 1) * width, :]) for j in range(CMP_BLOCK)]
    while len(parts) > 1:
        parts = [parts[i] + parts[i + 1] for i in range(0, len(parts), 2)]
    o_ref[...] = parts[0]


def _nsa_compress(x_arr, x_blk, phi_big, *, layer):
    b, nb = x_arr.shape[:2]
    width = 2 * LANES
    tn = _pick(nb, 160)
    return pl.pallas_call(
        _compress_kernel,
        grid=(b, nb // tn),
        in_specs=[
            pl.BlockSpec((None, tn, CMP_BLOCK, width), lambda bi, i: (bi, i, 0, x_blk)),
            pl.BlockSpec((None, CMP_BLOCK * width, width), lambda bi, i: (layer, 0, 0)),
        ],
        out_specs=pl.BlockSpec((None, tn, width), lambda bi, i: (bi, i, 0)),
        out_shape=jax.ShapeDtypeStruct((b, nb, width), F32),
        compiler_params=_cparams(("parallel", "parallel")),
        name="nsa_compress",
    )(x_arr, phi_big)


def _select_blocks(imp, blk, nb, n_sel):
    tq, nbp = imp.shape
    if tq % LANES != 0:
        rank = jnp.zeros((tq, nbp), F32)
        for mblk in range(nb):
            col = imp[:, mblk:mblk + 1]
            beats = (col > imp) | ((col == imp) & (blk > mblk))
            rank = rank + jnp.where(beats, 1.0, 0.0)
        return jnp.where(rank < n_sel, 1.0, 0.0)
    imp_t = imp.T
    n_parts = -(-nb // SUBLANES)
    parts = [imp_t[p * SUBLANES:(p + 1) * SUBLANES, :] for p in range(n_parts)]
    row_in_part = lax.broadcasted_iota(jnp.int32, (SUBLANES, tq), 0)
    ranks = [jnp.zeros((SUBLANES, tq), F32) for _ in range(n_parts)]
    for mblk in range(nb):
        pm, rm = divmod(mblk, SUBLANES)
        row = parts[pm][rm:rm + 1, :]
        for p in range(n_parts):
            if p < pm:
                beats = row > parts[p]
            elif p > pm:
                beats = row >= parts[p]
            else:
                beats = (row > parts[p]) | ((row == parts[p]) & (row_in_part > rm))
            ranks[p] = ranks[p] + jnp.where(beats, 1.0, 0.0)
    sel_t = [jnp.where(r < n_sel, 1.0, 0.0) for r in ranks]
    if nbp > n_parts * SUBLANES:
        sel_t.append(jnp.zeros((nbp - n_parts * SUBLANES, tq), F32))
    return jnp.concatenate(sel_t, axis=0).T


def _nsa_kernel(q_ref, sm_ref, kc_ref, ks_ref, win_ref, o_ref,
                qs_ref, sel_ref, part_ref, m_ref, l_ref, acc_ref,
                *, tq, tk, q0, lw, nb, nbp, n_sel, wl):
    i = pl.program_id(1)
    j = pl.program_id(2)
    rq = NSA_R * tq
    qs0 = q0 + i * tq
    j_last = (qs0 + tq - 1) // tk
    scale = QSCALE2

    @pl.when(j == 0)
    def _():
        lane_half = lax.broadcasted_iota(jnp.int32, (tq, LANES), 1) // HD
        gates = _sigmoid(sm_ref[...])
        qpos_b = qs0 + lax.broadcasted_iota(jnp.int32, (tq, nbp), 0)
        blk = lax.broadcasted_iota(jnp.int32, (tq, nbp), 1)
        cmask = ((blk + 1) * CMP_BLOCK - 1 <= qpos_b) & (blk < nb)
        cur = qpos_b // CMP_BLOCK
        forced = (blk == 0) | (blk == cur)
        valid = blk <= cur
        start = pl.multiple_of(jnp.maximum(lw + (i + 1) * tq - wl, 0), SUBLANES)
        qpos_w = qs0 + lax.broadcasted_iota(jnp.int32, (tq, wl), 0)
        kpos_w = q0 - lw + start + lax.broadcasted_iota(jnp.int32, (tq, wl), 1)
        wmask = (kpos_w <= qpos_w) & (kpos_w > qpos_w - WINDOW)
        kw = win_ref[pl.ds(start, wl), 0:LANES].astype(BF16)
        vw = win_ref[pl.ds(start, wl), LANES:2 * LANES].astype(BF16)
        kc = kc_ref[:, 0:LANES].astype(BF16)
        vc = kc_ref[:, LANES:2 * LANES].astype(BF16)
        for g in range(NSA_G):
            for r in range(NSA_R):
                chunk, half = 2 * g + r // 2, r % 2
                qc = q_ref[:, chunk * LANES:(chunk + 1) * LANES]
                if half != g:
                    qc = pltpu.roll(qc, HD, 1)
                qs_ref[g, r * tq:(r + 1) * tq, :] = jnp.where(lane_half == g, qc * scale, 0.0)
            qs = qs_ref[g].astype(BF16)
            s_c = _dot_nt(qs, kc).reshape(NSA_R, tq, nbp)
            p_c = _masked_softmax(s_c, cmask[None])
            o_c = _dot(p_c.reshape(rq, nbp).astype(BF16), vc)
            imp = jnp.sum(p_c, axis=0)
            imp = jnp.where(forced, SEL_FORCE, jnp.where(valid, imp, -1.0))
            imp = jnp.where(blk < nb, imp, -2.0)
            sel_ref[g] = _select_blocks(imp, blk, nb, n_sel)
            s_w = _dot_nt(qs, kw).reshape(NSA_R, tq, wl)
            p_w = _masked_softmax(s_w, wmask[None])
            o_w = _dot(p_w.reshape(rq, wl).astype(BF16), vw)
            for r in range(NSA_R):
                rows = slice(r * tq, (r + 1) * tq)
                lane_c = SM_GATE + 0 * NSA_G * NSA_R + g * NSA_R + r
                lane_w = SM_GATE + 2 * NSA_G * NSA_R + g * NSA_R + r
                part_ref[g, rows, :] = (gates[:, lane_c:lane_c + 1] * o_c[rows]
                                        + gates[:, lane_w:lane_w + 1] * o_w[rows])
        m_ref[...] = jnp.full_like(m_ref, NEG)
        l_ref[...] = jnp.zeros_like(l_ref)
        acc_ref[...] = jnp.zeros_like(acc_ref)

    @pl.when(j <= j_last)
    def _():
        slab = tq if tq >= LANES else rq
        ks = ks_ref[:, 0:LANES].astype(BF16)
        vs = ks_ref[:, LANES:2 * LANES].astype(BF16)
        kblk = (j * tk + lax.broadcasted_iota(jnp.int32, (nbp, tk), 1)) // CMP_BLOCK
        expand = (lax.broadcasted_iota(jnp.int32, (nbp, tk), 0) == kblk).astype(BF16)
        qpos = qs0 + lax.broadcasted_iota(jnp.int32, (tq, tk), 0)
        kpos = j * tk + lax.broadcasted_iota(jnp.int32, (tq, tk), 1)
        causal = kpos <= qpos
        pieces = [(g, slice(r0, r0 + slab)) for g in range(NSA_G) for r0 in range(0, rq, slab)]
        biases = []
        for g in range(NSA_G):
            bias = jnp.where((_dot(sel_ref[g].astype(BF16), expand) > 0.5) & causal, 0.0, NEG)
            biases.append(bias if slab == tq else jnp.concatenate([bias] * (slab // tq), axis=0))
        scores = [_dot_nt(qs_ref[g, rows, :].astype(BF16), ks) + biases[g] for g, rows in pieces]
        probs, corrs = [], []
        for (g, rows), s in zip(pieces, scores):
            m_old = m_ref[g, rows, :]
            m_new = jnp.maximum(m_old, jnp.max(s, axis=-1, keepdims=True))
            p = jnp.exp2(s - m_new)
            corr = jnp.exp2(m_old - m_new)
            l_ref[g, rows, :] = corr * l_ref[g, rows, :] + jnp.sum(p, axis=-1, keepdims=True)
            m_ref[g, rows, :] = m_new
            probs.append(p.astype(BF16))
            corrs.append(corr)
        for (g, rows), p, corr in zip(pieces, probs, corrs):
            acc_ref[g, rows, :] = corr * acc_ref[g, rows, :] + _dot(p, vs)

    @pl.when(j == pl.num_programs(2) - 1)
    def _():
        gates = _sigmoid(sm_ref[...])
        lane_half = lax.broadcasted_iota(jnp.int32, (tq, LANES), 1) // HD
        for g in range(NSA_G):
            o_s = acc_ref[g] / jnp.maximum(l_ref[g], 1e-30)
            res = []
            for r in range(NSA_R):
                rows = slice(r * tq, (r + 1) * tq)
                lane_s = SM_GATE + 1 * NSA_G * NSA_R + g * NSA_R + r
                res.append(part_ref[g, rows, :] + gates[:, lane_s:lane_s + 1] * o_s[rows])
            for c2 in range(2):
                even, odd = res[2 * c2], res[2 * c2 + 1]
                if g == 1:
                    even = pltpu.roll(even, HD, 1)
                else:
                    odd = pltpu.roll(odd, HD, 1)
                chunk = 2 * g + c2
                o_ref[:, chunk * LANES:(chunk + 1) * LANES] = jnp.where(lane_half == 0, even, odd)


def _nsa_attend(q_arr, q_blk, sm_arr, sm_blk, kc_arr, ks_arr, ks_blk, win_arr, *, q0, lw, n_real_keys):
    b, t = q_arr.shape[:2]
    l = ks_arr.shape[1]
    nbp = kc_arr.shape[1]
    nb = -(-n_real_keys // CMP_BLOCK)
    n_sel = min(SEL_TOPN, nb)
    tq = _pick(t, 128)
    tk = _pick(l, 1024, LANES)
    nq = t // tq
    nk = (q0 + t - 1) // tk + 1
    wl = min(WINDOW + tq, lw + t)
    rq = NSA_R * tq

    def ks_map(bi, i, j):
        return (bi, jnp.minimum(j, (q0 + (i + 1) * tq - 1) // tk), ks_blk)

    return pl.pallas_call(
        functools.partial(_nsa_kernel, tq=tq, tk=tk, q0=q0, lw=lw, nb=nb, nbp=nbp, n_sel=n_sel, wl=wl),
        grid=(b, nq, nk),
        in_specs=[
            pl.BlockSpec((None, tq, GW), lambda bi, i, j: (bi, i, q_blk)),
            pl.BlockSpec((None, tq, LANES), lambda bi, i, j: (bi, i, sm_blk)),
            pl.BlockSpec((None, nbp, 2 * LANES), lambda bi, i, j: (bi, 0, 0)),
            pl.BlockSpec((None, tk, 2 * LANES), ks_map),
            pl.BlockSpec((None, lw + t, 2 * LANES), lambda bi, i, j: (bi, 0, 0)),
        ],
        out_specs=pl.BlockSpec((None, tq, GW), lambda bi, i, j: (bi, i, 0)),
        out_shape=jax.ShapeDtypeStruct((b, t, GW), F32),
        scratch_shapes=[
            pltpu.VMEM((NSA_G, rq, LANES), F32),
            pltpu.VMEM((NSA_G, tq, nbp), F32),
            pltpu.VMEM((NSA_G, rq, LANES), F32),
            pltpu.VMEM((NSA_G, rq, 1), F32),
            pltpu.VMEM((NSA_G, rq, 1), F32),
            pltpu.VMEM((NSA_G, rq, LANES), F32),
        ],
        compiler_params=_cparams(("parallel", "parallel", "arbitrary")),
        name="nsa_attend",
    )(q_arr, sm_arr, kc_arr, ks_arr, win_arr)


def _lru_kernel(x_ref, gt_ref, buf_ref, h0_ref, cw_ref, cb_ref, wbd_ref, gb_ref, lam_ref,
                o_ref, hl_ref, nb_ref, xp_ref, a_ref, b_ref, h_ref, *, tt):
    s = pl.program_id(1)
    hist = CONV_PAD - (CONV_W - 1)

    @pl.when(s == 0)
    def _():
        xp_ref[hist:CONV_PAD, :] = buf_ref[...]
        h_ref[...] = h0_ref[...]

    xp_ref[CONV_PAD:CONV_PAD + tt, :] = x_ref[...]
    u = cb_ref[...]
    for jw in range(CONV_W):
        u = u + cw_ref[jw:jw + 1, :] * xp_ref[hist + jw:hist + jw + tt, :]
    gts = _dot(u.astype(BF16), wbd_ref[...]) + gb_ref[...]
    r = _sigmoid(gts[:, :GW])
    ig = _sigmoid(gts[:, GW:])
    log_a = -LRU_C * r * _softplus(-lam_ref[...])
    a = jnp.exp(log_a)
    a_ref[...] = a
    b_ref[...] = jnp.sqrt(-jnp.tanh(log_a) * (a * a + 1.0)) * (ig * u)

    def step(t, h):
        h = a_ref[pl.ds(t, 1), :] * h + b_ref[pl.ds(t, 1), :]
        b_ref[pl.ds(t, 1), :] = h
        return h

    h = lax.fori_loop(0, tt, step, h_ref[...], unroll=8)
    h_ref[...] = h
    o_ref[...] = b_ref[...] * _gelu_tanh(gt_ref[...])
    tail = xp_ref[tt + hist:tt + CONV_PAD, :]
    xp_ref[hist:CONV_PAD, :] = tail

    @pl.when(s == pl.num_programs(1) - 1)
    def _():
        hl_ref[...] = h
        nb_ref[...] = tail


def _rg_lru(h_arr, buf, h0, conv_w, conv_b, wbd, gate_b, lam, *, layer):
    b, t = h_arr.shape[:2]
    tt = _pick(t, 512)
    x_blk, g_blk = C_LRUX // GW, C_LRUG // GW
    wspec = lambda shape: pl.BlockSpec((None,) + shape, lambda bi, s: (layer,) + (0,) * len(shape))
    return pl.pallas_call(
        functools.partial(_lru_kernel, tt=tt),
        grid=(b, t // tt),
        in_specs=[
            pl.BlockSpec((None, tt, GW), lambda bi, s: (bi, s, x_blk)),
            pl.BlockSpec((None, tt, GW), lambda bi, s: (bi, s, g_blk)),
            pl.BlockSpec((None, CONV_W - 1, GW), lambda bi, s: (bi, 0, 0)),
            pl.BlockSpec((None, 1, GW), lambda bi, s: (bi, 0, 0)),
            wspec((CONV_W, GW)), wspec((1, GW)), wspec((GW, 2 * GW)), wspec((1, 2 * GW)), wspec((1, GW)),
        ],
        out_specs=[
            pl.BlockSpec((None, tt, GW), lambda bi, s: (bi, s, 0)),
            pl.BlockSpec((None, 1, GW), lambda bi, s: (bi, 0, 0)),
            pl.BlockSpec((None, CONV_W - 1, GW), lambda bi, s: (bi, 0, 0)),
        ],
        out_shape=[
            jax.ShapeDtypeStruct((b, t, GW), F32),
            jax.ShapeDtypeStruct((b, 1, GW), F32),
            jax.ShapeDtypeStruct((b, CONV_W - 1, GW), F32),
        ],
        scratch_shapes=[
            pltpu.VMEM((CONV_PAD + tt, GW), F32),
            pltpu.VMEM((tt, GW), F32),
            pltpu.VMEM((tt, GW), F32),
            pltpu.VMEM((1, GW), F32),
        ],
        compiler_params=_cparams(("parallel", "arbitrary")),
        name="rg_lru",
    )(h_arr, h_arr, buf, h0, conv_w, conv_b, wbd, gate_b, lam)


def _gdn_kernel(qkv_ref, z_ref, sm_ref, buf_ref, s0_ref, cw_ref, al_ref, dtb_ref, ng_ref,
                o_ref, sout_ref, nb_ref, xp_ref, smp_ref, st_ref, *, tt, tc):
    s = pl.program_id(1)
    hist = CONV_PAD - (CONV_W - 1)
    n_chunks = tc // GDN_CHUNK

    @pl.when(s == 0)
    def _():
        xp_ref[...] = jnp.zeros_like(xp_ref)
        smp_ref[...] = jnp.zeros_like(smp_ref)
        xp_ref[hist:CONV_PAD, :] = buf_ref[...]
        st_ref[...] = s0_ref[...]

    xp_ref[CONV_PAD:CONV_PAD + tt, :] = qkv_ref[...]
    smp_ref[0:tt, :] = sm_ref[...]
    row_ok = lax.broadcasted_iota(jnp.int32, (tc, 1), 0) < tt
    y = cw_ref[0:1, :] * xp_ref[hist:hist + tc, :]
    for jw in range(1, CONV_W):
        y = y + cw_ref[jw:jw + 1, :] * xp_ref[hist + jw:hist + jw + tc, :]
    y = jnp.where(row_ok, _silu(y), 0.0)
    sm = smp_ref[...]
    beta_all = jnp.where(row_ok, _sigmoid(sm), 0.0)
    g_all = jnp.where(row_ok, -jnp.exp(al_ref[...]) * _softplus(sm + dtb_ref[...]), 0.0)

    ri = lax.broadcasted_iota(jnp.int32, (tc, tc), 0)
    ci = lax.broadcasted_iota(jnp.int32, (tc, tc), 1)
    same = (ri // GDN_CHUNK) == (ci // GDN_CHUNK)
    incl = same & (ci <= ri)
    strict = same & (ci < ri)
    ones_where = lambda m: jnp.where(m, 1.0, 0.0).astype(BF16)
    gcum = _dot_hp_exact_lhs(ones_where(incl), g_all)
    gcum_t = _dot_hp_exact_rhs(g_all.T, ones_where(same & (ri <= ci)))
    gtot = _dot_hp_exact_lhs(ones_where(same), g_all)
    eye = jnp.where(ri == ci, 1.0, 0.0)

    heads = range(GDN_H)
    pws, tinvs, qks, rhss, qgs, kds, gts = [], [], [], [], [], [], []
    for h in heads:
        q = y[:, h * GDN_D:(h + 1) * GDN_D]
        k = y[:, GW + h * GDN_D:GW + (h + 1) * GDN_D]
        v = y[:, 2 * GW + h * GDN_D:2 * GW + (h + 1) * GDN_D]
        q = q * lax.rsqrt(jnp.sum(q * q, axis=-1, keepdims=True) + RMS_EPS) * (GDN_D ** -0.5)
        k = k * lax.rsqrt(jnp.sum(k * k, axis=-1, keepdims=True) + RMS_EPS)
        beta = beta_all[:, SM_BETA + h:SM_BETA + h + 1]
        gc = gcum[:, SM_DECAY + h:SM_DECAY + h + 1]
        gr = gcum_t[SM_DECAY + h:SM_DECAY + h + 1, :]
        gt = gtot[:, SM_DECAY + h:SM_DECAY + h + 1]
        decay = jnp.exp(jnp.where(incl, gc - gr, NEG))
        kb = k * beta
        kbf = k.astype(BF16)
        a_mat = jnp.where(strict, _dot_nt(kb.astype(BF16), kbf) * decay, 0.0)
        qks.append(jnp.where(incl, _dot_nt(q.astype(BF16), kbf) * decay, 0.0).astype(BF16))
        pws.append(-a_mat)
        tinvs.append(eye - a_mat)
        eg = jnp.exp(gc)
        rhss.append(jnp.concatenate([v * beta, kb * eg], axis=1))
        qgs.append((q * eg).astype(BF16))
        kds.append((k * jnp.exp(gt - gc)).astype(BF16))
        gts.append(gt)
    for _ in range(int(math.log2(GDN_CHUNK)) - 1):
        pws = [_dot_hp(pws[h], pws[h]) for h in heads]
        tinvs = [tinvs[h] + _dot_hp(tinvs[h], pws[h]) for h in heads]
    sols = [_dot_hp(tinvs[h], rhss[h]) for h in heads]
    us = [sols[h][:, :GDN_D] for h in heads]
    ws = [sols[h][:, GDN_D:].astype(BF16) for h in heads]
    sts = [st_ref[h] for h in heads]
    outs = [[] for _ in heads]
    for c in range(n_chunks):
        rows = slice(c * GDN_CHUNK, (c + 1) * GDN_CHUNK)
        stbs = [sts[h].astype(BF16) for h in heads]
        v_news = [(us[h][rows] - _dot(ws[h][rows], stbs[h])).astype(BF16) for h in heads]
        for h in heads:
            outs[h].append(_dot(qgs[h][rows], stbs[h]) + _dot(qks[h][rows, rows], v_news[h]))
        for h in heads:
            g_last = jnp.exp(gts[h][c * GDN_CHUNK:c * GDN_CHUNK + 1, :])
            sts[h] = sts[h] * g_last + _dot_tn(kds[h][rows], v_news[h])
    for h in heads:
        st_ref[h] = sts[h]
        o = outs[h][0] if n_chunks == 1 else jnp.concatenate(outs[h], axis=0)
        o = o * lax.rsqrt(jnp.mean(o * o, axis=-1, keepdims=True) + RMS_EPS) * ng_ref[...]
        o_ref[:, h * GDN_D:(h + 1) * GDN_D] = o[0:tt] * _silu(z_ref[:, h * GDN_D:(h + 1) * GDN_D])

    tail = xp_ref[tt + hist:tt + CONV_PAD, :]
    xp_ref[hist:CONV_PAD, :] = tail

    @pl.when(s == pl.num_programs(1) - 1)
    def _():
        sout_ref[...] = st_ref[...]
        nb_ref[...] = tail


def _gated_deltanet(h_arr, buf, s0, conv_w, a_log_row, dt_bias_row, norm_g, *, layer):
    b, t = h_arr.shape[:2]
    tt = _pick(t, 256)
    tc = -(-tt // GDN_CHUNK) * GDN_CHUNK
    assert tc == tt or t == tt, "sequence length must be a multiple of the chunk unless it fits one block"
    w3 = 3 * GW
    wspec = lambda shape: pl.BlockSpec((None,) + shape, lambda bi, s: (layer,) + (0,) * len(shape))
    return pl.pallas_call(
        functools.partial(_gdn_kernel, tt=tt, tc=tc),
        grid=(b, t // tt),
        in_specs=[
            pl.BlockSpec((None, tt, w3), lambda bi, s: (bi, s, C_GQKV // w3)),
            pl.BlockSpec((None, tt, GW), lambda bi, s: (bi, s, C_GZ // GW)),
            pl.BlockSpec((None, tt, LANES), lambda bi, s: (bi, s, C_SMALL // LANES)),
            pl.BlockSpec((None, CONV_W - 1, w3), lambda bi, s: (bi, 0, 0)),
            pl.BlockSpec((None, GDN_H, GDN_D, GDN_D), lambda bi, s: (bi, 0, 0, 0)),
            wspec((CONV_W, w3)), wspec((1, LANES)), wspec((1, LANES)), wspec((1, GDN_D)),
        ],
        out_specs=[
            pl.BlockSpec((None, tt, GW), lambda bi, s: (bi, s, 0)),
            pl.BlockSpec((None, GDN_H, GDN_D, GDN_D), lambda bi, s: (bi, 0, 0, 0)),
            pl.BlockSpec((None, CONV_W - 1, w3), lambda bi, s: (bi, 0, 0)),
        ],
        out_shape=[
            jax.ShapeDtypeStruct((b, t, GW), F32),
            jax.ShapeDtypeStruct((b, GDN_H, GDN_D, GDN_D), F32),
            jax.ShapeDtypeStruct((b, CONV_W - 1, w3), F32),
        ],
        scratch_shapes=[
            pltpu.VMEM((CONV_PAD + tc, w3), F32),
            pltpu.VMEM((tc, LANES), F32),
            pltpu.VMEM((GDN_H, GDN_D, GDN_D), F32),
        ],
        compiler_params=_cparams(("parallel", "arbitrary")),
        name="gated_deltanet",
    )(h_arr, h_arr, h_arr, buf, s0, conv_w, a_log_row, dt_bias_row, norm_g)


def _outproj_kernel(oa_ref, ob_ref, oc_ref, od_ref, x_ref, w_ref, gg_ref, g_ref, b_ref, o_ref, *, alpha):
    def rms(v, gain):
        return (v * lax.rsqrt(jnp.mean(v * v, axis=-1, keepdims=True) + RMS_EPS) * gain).astype(BF16)

    acc = _dot(rms(oa_ref[...], gg_ref[0:1, :]), w_ref[0:GW, :])
    acc += _dot(rms(ob_ref[...], gg_ref[1:2, :]), w_ref[GW:2 * GW, :])
    acc += _dot(rms(oc_ref[...], gg_ref[2:3, :]), w_ref[2 * GW:3 * GW, :])
    acc += _dot(od_ref[...].astype(BF16), w_ref[3 * GW:4 * GW, :])
    o_ref[...] = _layer_norm(alpha * x_ref[...] + acc, g_ref[...], b_ref[...])


def _outproj_ln(oa, ob, oc, od, x, w_out, grp_g, ln_g, ln_b, *, layer, alpha):
    n, d = x.shape
    tm = _pick(n, 512)
    mix_spec = pl.BlockSpec((tm, GW), lambda i: (i, 0))
    return pl.pallas_call(
        functools.partial(_outproj_kernel, alpha=alpha),
        grid=(n // tm,),
        in_specs=[
            mix_spec, mix_spec, mix_spec, mix_spec,
            pl.BlockSpec((tm, d), lambda i: (i, 0)),
            pl.BlockSpec((None, 4 * GW, d), lambda i: (layer, 0, 0)),
            pl.BlockSpec((None, 3, GW), lambda i: (layer, 0, 0)),
            pl.BlockSpec((None, None, 1, d), lambda i: (layer, 1, 0, 0)),
            pl.BlockSpec((None, None, 1, d), lambda i: (layer, 1, 0, 0)),
        ],
        out_specs=pl.BlockSpec((tm, d), lambda i: (i, 0)),
        out_shape=jax.ShapeDtypeStruct((n, d), F32),
        compiler_params=_cparams(("parallel",)),
        name="outproj_ln",
    )(oa, ob, oc, od, x, w_out, grp_g, ln_g, ln_b)


def _prep_weights(ffn_gu, ffn_down, w_in, w_out, nsa_phi, lru_gate_w, lru_gate_b, gdn_A_log, gdn_dt_bias):
    depth, d_model, _ = w_in.shape
    o_nsakv, o_gate, o_lrux, o_gqkv, o_gz, o_gb = 2048, 2816, 2840, 3864, 5400, 5912
    zeros = jnp.zeros((depth, d_model, P_IN_PAD - 5920), w_in.dtype)
    w_in_p = jnp.concatenate([
        w_in[..., 512:1536],
        w_in[..., 0:512],
        w_in[..., 1536:2048],
        w_in[..., o_nsakv:o_nsakv + 512],
        w_in[..., o_lrux:o_lrux + 1024],
        w_in[..., o_gz:o_gz + 512],
        w_in[..., o_nsakv + 512:o_gate],
        w_in[..., o_gate:o_lrux],
        w_in[..., o_gb:o_gb + 8],
        zeros,
        w_in[..., o_gqkv:o_gz],
    ], axis=-1).astype(BF16)
    phi_b = jnp.transpose(nsa_phi, (0, 2, 1, 3, 4)).astype(BF16)
    col_blocks = []
    for c in range(2):
        for g in range(NSA_G):
            hot = jnp.zeros((2, NSA_G, 1, 1), BF16).at[c, g].set(1)
            col_blocks.append(phi_b[:, :, :, None, :, :] * hot[None, None])
    phi_big = jnp.concatenate(col_blocks, axis=-1).reshape(depth, CMP_BLOCK * 2 * NSA_G * HD, 2 * NSA_G * HD)
    eye_b = jnp.eye(LRU_BLKS, dtype=F32)
    wbd = jnp.einsum("lknce,nm->lnckme", lru_gate_w, eye_b).reshape(depth, GW, 2 * GW).astype(BF16)
    gate_b = lru_gate_b.reshape(depth, 1, 2 * GW)
    pad_l = jnp.zeros((depth, SM_DECAY), F32)
    pad_r = jnp.zeros((depth, LANES - SM_DECAY - GDN_H), F32)
    a_log_row = jnp.concatenate([pad_l, gdn_A_log, pad_r], axis=1)[:, None, :]
    dt_bias_row = jnp.concatenate([pad_l, gdn_dt_bias, pad_r], axis=1)[:, None, :]
    return dict(
        wgu=ffn_gu.astype(BF16), wd=ffn_down.astype(BF16), w_in=w_in_p, w_out=w_out.astype(BF16),
        phi_big=phi_big, wbd=wbd, gate_b=gate_b, a_log_row=a_log_row, dt_bias_row=dt_bias_row)


def _trunk_layer(x, b, t, q0, past, wts, layer, alpha):
    ln_g, ln_b = wts["ln_g"], wts["ln_b"]
    x = _ffn_ln(x, wts["wgu"], wts["wd"], ln_g, ln_b, layer=layer, which=0, ln_idx=0, alpha=alpha)
    h = _matmul(x, wts["w_in"], layer=layer, name="in_proj")
    h3 = h.reshape(b, t, P_IN_PAD)

    sb_rows = h3[:, :, C_SBK:C_SBK + 2 * GW]
    nsa_rows = h3[:, :, C_NSAKV:C_NSAKV + GW]
    nsa_win_new = h3[:, :, C_NSAW:C_NSAW + 2 * LANES]

    if past["page_table"] is None:
        lw = 0
        o_a = _sb_attend(h3, C_SBQ // GW, h3, C_SBK // GW, h3, C_SBV // GW, q0=q0)
        cmp_rows, cmp_blk = h3, C_NSAKV // (2 * LANES)
        ks_arr, ks_blk = h3, (C_NSAKV + 2 * LANES) // (2 * LANES)
        win_all = nsa_win_new
    else:
        pt = past["page_table"]
        lw = past["nsa_win"].shape[2]
        o_a = _sb_attend_paged(h3, C_SBQ // GW, C_SBK // (2 * GW), past["sb_kv"], pt, layer=layer)
        cmp_rows, ks_arr = _gather_nsa_pages(past["nsa_kv"], pt, h3, C_NSAKV // GW, layer=layer)
        cmp_blk = ks_blk = 0
        win_all = jnp.concatenate([past["nsa_win"][layer], nsa_win_new], axis=1)

    nb_arr = cmp_rows.shape[1] // CMP_BLOCK
    kc = _nsa_compress(cmp_rows.reshape(b, nb_arr, CMP_BLOCK, cmp_rows.shape[2]), cmp_blk, wts["phi_big"],
                       layer=layer)
    nbp = -(-nb_arr // LANES) * LANES
    kc = jnp.pad(kc, ((0, 0), (0, nbp - nb_arr), (0, 0)))
    o_b = _nsa_attend(h3, C_NSAQ // GW, h3, C_SMALL // LANES, kc, ks_arr, ks_blk, win_all,
                      q0=q0, lw=lw, n_real_keys=q0 + t)
    new_win = win_all[:, -min(WINDOW, lw + t):]

    o_c, h_last, new_lru_buf = _rg_lru(h3, past["lru_conv"], past["lru_h"], wts["lru_conv_w"], wts["lru_conv_b"],
                                       wts["wbd"], wts["gate_b"], wts["lru_lambda"], layer=layer)
    o_d, s_new, new_gdn_buf = _gated_deltanet(h3, past["gdn_conv"], past["gdn_S"], wts["gdn_conv_w"],
                                              wts["a_log_row"], wts["dt_bias_row"], wts["gdn_norm_g"], layer=layer)

    n = b * t
    x = _outproj_ln(o_a.reshape(n, GW), o_b.reshape(n, GW), o_c.reshape(n, GW), o_d.reshape(n, GW), x,
                    wts["w_out"], wts["grp_norm_g"], ln_g, ln_b, layer=layer, alpha=alpha)
    x = _ffn_ln(x, wts["wgu"], wts["wd"], ln_g, ln_b, layer=layer, which=1, ln_idx=2, alpha=alpha)
    new_state = (
        sb_rows.reshape(b, t, 2, SB_H, HD),
        nsa_rows.reshape(b, t, 4, NSA_G, HD),
        new_win.reshape(b, new_win.shape[1], 2, NSA_G, HD),
        h_last.reshape(b, GW),
        new_lru_buf,
        s_new,
        new_gdn_buf,
    )
    return x, new_state


def kernel(x_prompt, x_sample, cache_sb_kv, cache_nsa_kv, cache_nsa_win, state_lru_h, state_lru_conv,
           state_gdn_S, state_gdn_conv, page_table, ln_g, ln_b, ffn_gu, ffn_down, w_in, w_out, grp_norm_g,
           nsa_phi, lru_conv_w, lru_conv_b, lru_gate_w, lru_gate_b, lru_lambda, gdn_conv_w, gdn_A_log,
           gdn_dt_bias, gdn_norm_g):
    depth, d_model = w_in.shape[0], w_in.shape[1]
    alpha = (2 * depth) ** 0.25
    n_b, seq = x_prompt.shape[:2]
    n_db, dec_seq = x_sample.shape[:2]
    past_len = page_table.shape[1] * PAGE
    n_pool = cache_sb_kv.shape[1]

    wts = _prep_weights(ffn_gu, ffn_down, w_in, w_out, nsa_phi, lru_gate_w, lru_gate_b, gdn_A_log, gdn_dt_bias)
    wts.update(
        ln_g=ln_g.reshape(depth, 3, 1, d_model), ln_b=ln_b.reshape(depth, 3, 1, d_model),
        grp_norm_g=grp_norm_g, lru_conv_w=lru_conv_w, lru_conv_b=lru_conv_b.reshape(depth, 1, GW),
        lru_lambda=lru_lambda.reshape(depth, 1, GW), gdn_conv_w=gdn_conv_w,
        gdn_norm_g=gdn_norm_g.reshape(depth, 1, GDN_D))

    sb_cache = jnp.transpose(cache_sb_kv, (0, 1, 3, 4, 5, 2))
    nsa_cache = jnp.transpose(cache_nsa_kv, (0, 1, 3, 4, 5, 2))
    nsa_win = cache_nsa_win.reshape(depth, n_db, cache_nsa_win.shape[2], 2 * LANES)

    y_p = x_prompt.reshape(n_b * seq, d_model)
    y_s = x_sample.reshape(n_db * dec_seq, d_model)
    st_p, st_s = [], []
    for l in range(depth):
        past_p = dict(page_table=None,
                      lru_h=jnp.zeros((n_b, 1, GW), F32), lru_conv=jnp.zeros((n_b, CONV_W - 1, GW), F32),
                      gdn_S=jnp.zeros((n_b, GDN_H, GDN_D, GDN_D), F32),
                      gdn_conv=jnp.zeros((n_b, CONV_W - 1, 3 * GW), F32))
        past_s = dict(page_table=page_table, sb_kv=sb_cache, nsa_kv=nsa_cache, nsa_win=nsa_win,
                      lru_h=state_lru_h[l].reshape(n_db, 1, GW), lru_conv=state_lru_conv[l],
                      gdn_S=state_gdn_S[l], gdn_conv=state_gdn_conv[l])
        y_p, new_p = _trunk_layer(y_p, n_b, seq, 0, past_p, wts, l, alpha)
        y_s, new_s = _trunk_layer(y_s, n_db, dec_seq, past_len, past_s, wts, l, alpha)
        st_p.append(new_p)
        st_s.append(new_s)
    p = [jnp.stack(a) for a in zip(*st_p)]
    s = [jnp.stack(a) for a in zip(*st_s)]
    return (y_p.reshape(n_b, seq, d_model), y_s.reshape(n_db, dec_seq, d_model),
            p[0], p[1], p[2], p[3], p[4], p[5], p[6], s[0], s[1], s[2], s[3], s[4], s[5], s[6])
```

```python
import functools
import math

import jax
import jax.numpy as jnp
from jax import lax
from jax.experimental import pallas as pl
from jax.experimental.pallas import tpu as pltpu

F32 = jnp.float32
BF16 = jnp.bfloat16

PAGE = 128
GW = 512
HD = 64
SB_H = 8
NSA_G = 2
NSA_R = 4
CMP_BLOCK = 64
SEL_TOPN = 16
SEL_FORCE = float(NSA_R + 1)
WINDOW = 512
LRU_BLKS = 8
LRU_C = 8.0
CONV_W = 4
GDN_H = 4
GDN_D = 128
GDN_CHUNK = 64
LN_EPS = 1e-5
RMS_EPS = 1e-6
NEG = -1e30
LOG2E = 1.4426950408889634
QSCALE2 = (HD ** -0.5) * LOG2E

LANES = 128
SUBLANES = 8
VMEM_LIMIT = 56 * 1024 * 1024

C_SBK, C_SBV, C_SBQ, C_NSAQ, C_NSAKV = 0, 512, 1024, 1536, 2048
C_LRUX, C_LRUG, C_GZ, C_NSAW, C_SMALL, C_GQKV = 2560, 3072, 3584, 4096, 4352, 4608
P_IN_PAD = 6144
SM_GATE, SM_BETA, SM_DECAY = 0, 24, 28
CONV_PAD = 8


def _pick(n, pref, mult=SUBLANES):
    if n <= pref:
        return n
    for t in range(pref, 0, -1):
        if n % t == 0 and t % mult == 0:
            return t
    return n


def _cparams(sem):
    return pltpu.CompilerParams(dimension_semantics=sem, vmem_limit_bytes=VMEM_LIMIT)


def _sigmoid(x):
    return 1.0 / (1.0 + jnp.exp(-x))


def _silu(x):
    return x * _sigmoid(x)


def _softplus(x):
    return jnp.maximum(x, 0.0) + jnp.log1p(jnp.exp(-jnp.abs(x)))


def _gelu_tanh(x):
    return 0.5 * x * (1.0 + jnp.tanh(math.sqrt(2.0 / math.pi) * (x + 0.044715 * (x * x * x))))


def _layer_norm(y, g, b):
    mu = jnp.mean(y, axis=-1, keepdims=True)
    d = y - mu
    var = jnp.mean(d * d, axis=-1, keepdims=True)
    return d * lax.rsqrt(var + LN_EPS) * g + b


def _dot(a, b):
    return jnp.dot(a, b, preferred_element_type=F32)


def _dot_nt(a, b):
    return lax.dot_general(a, b, (((1,), (1,)), ((), ())), preferred_element_type=F32)


def _dot_tn(a, b):
    return lax.dot_general(a, b, (((0,), (0,)), ((), ())), preferred_element_type=F32)


def _split_bf16(x):
    hi = x.astype(BF16)
    return hi, (x - hi.astype(F32)).astype(BF16)


def _dot_hp(a, b):
    ah, al = _split_bf16(a)
    bh, bl = _split_bf16(b)
    return _dot(jnp.concatenate([ah, ah, al], axis=1), jnp.concatenate([bh, bl, bh], axis=0))


def _dot_hp_exact_lhs(a_bf16, b):
    bh, bl = _split_bf16(b)
    return _dot(jnp.concatenate([a_bf16, a_bf16], axis=1), jnp.concatenate([bh, bl], axis=0))


def _dot_hp_exact_rhs(a, b_bf16):
    ah, al = _split_bf16(a)
    return _dot(jnp.concatenate([ah, al], axis=1), jnp.concatenate([b_bf16, b_bf16], axis=0))


def _ffn_ln_kernel(x_ref, wg_ref, wu_ref, wd_ref, g_ref, b_ref, o_ref, xb_ref, *, alpha):
    j = pl.program_id(1)

    @pl.when(j == 0)
    def _():
        xb_ref[...] = x_ref[...].astype(BF16)
        o_ref[...] = jnp.zeros_like(o_ref)

    xb = xb_ref[...]
    gate = _dot(xb, wg_ref[...].astype(BF16))
    up = _dot(xb, wu_ref[...].astype(BF16))
    act = (_silu(gate) * up).astype(BF16)
    o_ref[...] += _dot(act, wd_ref[...].astype(BF16))

    @pl.when(j == pl.num_programs(1) - 1)
    def _():
        y = alpha * x_ref[...] + 0.5 * o_ref[...]
        o_ref[...] = _layer_norm(y, g_ref[...], b_ref[...])


def _ffn_ln(x, wgu, wd, ln_g, ln_b, *, layer, which, ln_idx, alpha):
    n, d = x.shape
    f = wd.shape[2]
    tm = _pick(n, 1024)
    tf = _pick(f, 256, LANES)
    nf = f // tf
    return pl.pallas_call(
        functools.partial(_ffn_ln_kernel, alpha=alpha),
        grid=(n // tm, nf),
        in_specs=[
            pl.BlockSpec((tm, d), lambda i, j: (i, 0), pipeline_mode=pl.Buffered(1)),
            pl.BlockSpec((None, None, d, tf), lambda i, j: (layer, which, 0, j)),
            pl.BlockSpec((None, None, d, tf), lambda i, j: (layer, which, 0, j + nf)),
            pl.BlockSpec((None, None, tf, d), lambda i, j: (layer, which, j, 0)),
            pl.BlockSpec((None, None, 1, d), lambda i, j: (layer, ln_idx, 0, 0)),
            pl.BlockSpec((None, None, 1, d), lambda i, j: (layer, ln_idx, 0, 0)),
        ],
        out_specs=pl.BlockSpec((tm, d), lambda i, j: (i, 0)),
        out_shape=jax.ShapeDtypeStruct((n, d), F32),
        scratch_shapes=[pltpu.VMEM((tm, d), BF16)],
        compiler_params=_cparams(("parallel", "arbitrary")),
        name="ffn_ln",
    )(x, wgu, wgu, wd, ln_g, ln_b)


def _matmul_kernel(x_ref, w_ref, o_ref):
    k = pl.program_id(2)

    @pl.when(k == 0)
    def _():
        o_ref[...] = jnp.zeros_like(o_ref)

    o_ref[...] += _dot(x_ref[...].astype(BF16), w_ref[...])


def _matmul(x, w, *, layer, tm_pref=1024, tn_pref=512, tk_pref=2048, name="matmul"):
    m, kdim = x.shape
    nout = w.shape[2]
    tm = _pick(m, tm_pref)
    tn = _pick(nout, tn_pref, LANES)
    tk = _pick(kdim, tk_pref, LANES)
    return pl.pallas_call(
        _matmul_kernel,
        grid=(m // tm, nout // tn, kdim // tk),
        in_specs=[
            pl.BlockSpec((tm, tk), lambda i, j, k: (i, k)),
            pl.BlockSpec((None, tk, tn), lambda i, j, k: (layer, k, j)),
        ],
        out_specs=pl.BlockSpec((tm, tn), lambda i, j, k: (i, j)),
        out_shape=jax.ShapeDtypeStruct((m, nout), F32),
        compiler_params=_cparams(("parallel", "parallel", "arbitrary")),
        name=name,
    )(x, w)


def _gather_nsa_kernel(pt_ref, *refs, n_group, t_new):
    del pt_ref
    page_refs = refs[:n_group]
    new_ref = refs[n_group]
    cmp_ref, sel_ref = refs[n_group + 1:]
    s = pl.program_id(1)
    last = pl.num_programs(1) - 1

    @pl.when(s < last)
    def _():
        for g in range(n_group):
            rows = slice(g * PAGE, (g + 1) * PAGE)
            for kind in range(4):
                x = page_refs[g][kind].reshape(NSA_G * HD, PAGE).T
                o_ref = cmp_ref if kind < 2 else sel_ref
                o_ref[rows, (kind % 2) * LANES:(kind % 2 + 1) * LANES] = x

    @pl.when(s == last)
    def _():
        for o_ref, c0 in ((cmp_ref, 0), (sel_ref, 2 * LANES)):
            o_ref[...] = jnp.zeros_like(o_ref)
            o_ref[0:t_new, :] = new_ref[:, c0:c0 + 2 * LANES]


def _gather_nsa_pages(cache_t, page_table, new_arr, new_colblk, *, layer):
    b, n_pages = page_table.shape
    t_new = new_arr.shape[1]
    n_group = math.gcd(n_pages, 8)
    n_steps = n_pages // n_group + 1
    rows = n_group * PAGE

    def page_map(g):
        return lambda bi, s, pt: (layer, pt[bi, jnp.minimum(s * n_group + g, n_pages - 1)], 0, 0, 0, 0)

    in_specs = [pl.BlockSpec((None, None, 4, NSA_G, HD, PAGE), page_map(g)) for g in range(n_group)]
    in_specs.append(pl.BlockSpec((None, t_new, GW), lambda bi, s, pt: (bi, 0, new_colblk)))
    out_specs = [pl.BlockSpec((None, rows, 2 * LANES), lambda bi, s, pt: (bi, s, 0)) for _ in range(2)]
    out_shape = [jax.ShapeDtypeStruct((b, n_steps * rows, 2 * LANES), F32) for _ in range(2)]
    return pl.pallas_call(
        functools.partial(_gather_nsa_kernel, n_group=n_group, t_new=t_new),
        grid_spec=pltpu.PrefetchScalarGridSpec(
            num_scalar_prefetch=1, grid=(b, n_steps), in_specs=in_specs, out_specs=out_specs),
        out_shape=out_shape,
        compiler_params=_cparams(("parallel", "arbitrary")),
        name="gather_nsa_pages",
    )(page_table, *([cache_t] * n_group), new_arr)


def _suffix_matrix(n):
    j = lax.broadcasted_iota(jnp.int32, (2 * n, n), 0)
    s = lax.broadcasted_iota(jnp.int32, (2 * n, n), 1)
    return ((j > s) & ((j < n) | (j - n > s))).astype(BF16)


def _sb_neg_log(z2, mask):
    nl = jnp.maximum(z2, 0.0) + jnp.log2(1.0 + jnp.exp2(-jnp.abs(z2)))
    if mask is not None:
        nl = jnp.where(mask, nl, 0.0)
    hi = nl.astype(BF16)
    lo = (nl - hi.astype(F32)).astype(BF16)
    return nl, jnp.concatenate([hi, lo], axis=1)


def _sb_weight(z2, nl, tail, carry, mask):
    w = jnp.exp2(z2 - nl - tail - carry)
    if mask is not None:
        w = jnp.where(mask, w, 0.0)
    return w.astype(BF16)


def _sb_weights(z2, tri2, carry, mask):
    nl, hilo = _sb_neg_log(z2, mask)
    tail = _dot(hilo, tri2)
    return _sb_weight(z2, nl, tail, carry, mask), tail[:, 0:1] + nl[:, 0:1]


def _sb_kernel(qi_ref, ki_ref, q_ref, k_ref, v_ref, o_ref, acc_ref, c_ref, *, tq, tk, q0):
    p = pl.program_id(1)
    i = qi_ref[p]
    jt = ki_ref[p]
    qs0 = q0 + i * tq

    @pl.when(jt == (qs0 + tq - 2) // tk)
    def _():
        acc_ref[...] = jnp.zeros_like(acc_ref)
        c_ref[...] = jnp.zeros_like(c_ref)

    def tile(masked):
        mask = None
        if masked:
            qpos = qs0 + lax.broadcasted_iota(jnp.int32, (tq, tk), 0)
            kpos = jt * tk + lax.broadcasted_iota(jnp.int32, (tq, tk), 1)
            mask = kpos < qpos
        tri = _suffix_matrix(tk)
        lane_half = lax.broadcasted_iota(jnp.int32, (tq, LANES), 1) // HD
        cols = [slice((h // 2) * LANES, (h // 2 + 1) * LANES) for h in range(SB_H)]
        zs = []
        for h in range(SB_H):
            qc = jnp.where(lane_half == h % 2, q_ref[:, cols[h]] * QSCALE2, 0.0).astype(BF16)
            zs.append(_dot_nt(qc, k_ref[:, cols[h]].astype(BF16)))
        nls, hilos = [], []
        for h in range(SB_H):
            nl, hilo = _sb_neg_log(zs[h], mask)
            nls.append(nl)
            hilos.append(hilo)
        tails = [_dot(hilos[h], tri) for h in range(SB_H)]
        ws = []
        for h in range(SB_H):
            ws.append(_sb_weight(zs[h], nls[h], tails[h], c_ref[h], mask))
            c_ref[h] += tails[h][:, 0:1] + nls[h][:, 0:1]
        for h in range(SB_H):
            acc_ref[h] += _dot(ws[h], v_ref[:, cols[h]].astype(BF16))

    whole = (jt + 1) * tk <= qs0

    @pl.when(whole)
    def _():
        tile(False)

    @pl.when(jnp.logical_not(whole))
    def _():
        tile(True)

    @pl.when(jt == 0)
    def _():
        lane_half = lax.broadcasted_iota(jnp.int32, (tq, LANES), 1) // HD
        for c in range(SB_H // 2):
            o_ref[:, c * LANES:(c + 1) * LANES] = jnp.where(lane_half == 0, acc_ref[2 * c], acc_ref[2 * c + 1])


def _sb_attend(q_arr, q_blk, k_arr, k_blk, v_arr, v_blk, *, q0):
    b, t = q_arr.shape[:2]
    l = k_arr.shape[1]
    tq = _pick(t, 256)
    tk = _pick(l, 256, LANES)
    pairs = [(i, jt) for i in range(t // tq) for jt in range((q0 + (i + 1) * tq - 2) // tk, -1, -1)]
    qi = jnp.asarray([pr[0] for pr in pairs], jnp.int32)
    ki = jnp.asarray([pr[1] for pr in pairs], jnp.int32)
    return pl.pallas_call(
        functools.partial(_sb_kernel, tq=tq, tk=tk, q0=q0),
        grid_spec=pltpu.PrefetchScalarGridSpec(
            num_scalar_prefetch=2, grid=(b, len(pairs)),
            in_specs=[
                pl.BlockSpec((None, tq, GW), lambda bi, p, qi_r, ki_r: (bi, qi_r[p], q_blk)),
                pl.BlockSpec((None, tk, GW), lambda bi, p, qi_r, ki_r: (bi, ki_r[p], k_blk)),
                pl.BlockSpec((None, tk, GW), lambda bi, p, qi_r, ki_r: (bi, ki_r[p], v_blk)),
            ],
            out_specs=pl.BlockSpec((None, tq, GW), lambda bi, p, qi_r, ki_r: (bi, qi_r[p], 0)),
            scratch_shapes=[pltpu.VMEM((SB_H, tq, LANES), F32), pltpu.VMEM((SB_H, tq, 1), F32)]),
        out_shape=jax.ShapeDtypeStruct((b, t, GW), F32),
        compiler_params=_cparams(("parallel", "arbitrary")),
        name="sb_attend",
    )(qi, ki, q_arr, k_arr, v_arr)


def _sb_paged_kernel(pt_ref, q_ref, new_ref, *refs, n_group, t):
    del pt_ref
    page_refs = refs[:n_group]
    o_ref, qbd_ref, acc_ref, c_ref = refs[n_group:]
    s = pl.program_id(1)
    m = SB_H * t
    tri = _suffix_matrix(PAGE)

    @pl.when(s == 0)
    def _():
        lane_head = lax.broadcasted_iota(jnp.int32, (t, GW), 1) // HD
        q = q_ref[...] * QSCALE2
        for h in range(SB_H):
            qbd_ref[h * t:(h + 1) * t, :] = jnp.where(lane_head == h, q, 0.0)
        pad = jnp.zeros((PAGE - t, GW), F32)
        k_new = jnp.concatenate([new_ref[:, 0:GW], pad], axis=0).astype(BF16)
        v_new = jnp.concatenate([new_ref[:, GW:2 * GW], pad], axis=0).astype(BF16)
        row = lax.broadcasted_iota(jnp.int32, (m, PAGE), 0)
        mask = lax.broadcasted_iota(jnp.int32, (m, PAGE), 1) < row - (row // t) * t
        w, rs = _sb_weights(_dot_nt(qbd_ref[...].astype(BF16), k_new), tri, 0.0, mask)
        acc_ref[...] = _dot(w, v_new)
        c_ref[...] = rs

    qb = qbd_ref[...].astype(BF16)
    pages = range(n_group)
    zs = [_dot(qb, page_refs[g][0].reshape(GW, PAGE).astype(BF16)) for g in pages]
    nls, hilos = zip(*[_sb_neg_log(zs[g], None) for g in pages])
    tails = [_dot(hilos[g], tri) for g in pages]
    carries = [None] * n_group
    carry = c_ref[...]
    for g in reversed(pages):
        carries[g] = carry
        carry = carry + tails[g][:, 0:1] + nls[g][:, 0:1]
    c_ref[...] = carry
    ws = [_sb_weight(zs[g], nls[g], tails[g], carries[g], None) for g in pages]
    acc = acc_ref[...]
    for g in pages:
        acc = acc + _dot_nt(ws[g], page_refs[g][1].reshape(GW, PAGE).astype(BF16))
    acc_ref[...] = acc

    @pl.when(s == pl.num_programs(1) - 1)
    def _():
        lane_head = lax.broadcasted_iota(jnp.int32, (t, GW), 1) // HD
        out = jnp.zeros((t, GW), F32)
        for h in range(SB_H):
            out = out + jnp.where(lane_head == h, acc_ref[h * t:(h + 1) * t, :], 0.0)
        o_ref[...] = out


def _sb_attend_paged(q_arr, q_blk, new_blk, cache_t, page_table, *, layer):
    b, t = q_arr.shape[:2]
    n_pages = page_table.shape[1]
    n_group = math.gcd(n_pages, 8)
    n_steps = n_pages // n_group
    m = SB_H * t

    def page_map(g):
        return lambda bi, s, pt: (layer, pt[bi, n_pages - (s + 1) * n_group + g], 0, 0, 0, 0)

    in_specs = [
        pl.BlockSpec((None, t, GW), lambda bi, s, pt: (bi, 0, q_blk)),
        pl.BlockSpec((None, t, 2 * GW), lambda bi, s, pt: (bi, 0, new_blk)),
    ] + [pl.BlockSpec((None, None, 2, SB_H, HD, PAGE), page_map(g)) for g in range(n_group)]
    return pl.pallas_call(
        functools.partial(_sb_paged_kernel, n_group=n_group, t=t),
        grid_spec=pltpu.PrefetchScalarGridSpec(
            num_scalar_prefetch=1, grid=(b, n_steps), in_specs=in_specs,
            out_specs=pl.BlockSpec((None, t, GW), lambda bi, s, pt: (bi, 0, 0)),
            scratch_shapes=[pltpu.VMEM((m, GW), F32), pltpu.VMEM((m, GW), F32), pltpu.VMEM((m, 1), F32)]),
        out_shape=jax.ShapeDtypeStruct((b, t, GW), F32),
        compiler_params=_cparams(("parallel", "arbitrary")),
        name="sb_attend_paged",
    )(page_table, q_arr, q_arr, *([cache_t] * n_group))


def _masked_softmax(s2, mask):
    sm = jnp.where(mask, s2, NEG)
    m = jnp.max(sm, axis=-1, keepdims=True)
    e = jnp.where(mask, jnp.exp2(sm - m), 0.0)
    return e / jnp.maximum(jnp.sum(e, axis=-1, keepdims=True), 1e-30)


def _compress_kernel(x_ref, w_ref, o_ref):
    width = 2 * LANES
    parts = [_dot(x_ref[:, j, :].astype(BF16), w_ref[j * width:(j + 1) * width, :]) for j in range(CMP_BLOCK)]
    while len(parts) > 1:
        parts = [parts[i] + parts[i + 1] for i in range(0, len(parts), 2)]
    o_ref[...] = parts[0]


def _nsa_compress(x_arr, x_blk, phi_big, *, layer):
    b, nb = x_arr.shape[:2]
    width = 2 * LANES
    tn = _pick(nb, 160)
    return pl.pallas_call(
        _compress_kernel,
        grid=(b, nb // tn),
        in_specs=[
            pl.BlockSpec((None, tn, CMP_BLOCK, width), lambda bi, i: (bi, i, 0, x_blk)),
            pl.BlockSpec((None, CMP_BLOCK * width, width), lambda bi, i: (layer, 0, 0)),
        ],
        out_specs=pl.BlockSpec((None, tn, width), lambda bi, i: (bi, i, 0)),
        out_shape=jax.ShapeDtypeStruct((b, nb, width), F32),
        compiler_params=_cparams(("parallel", "parallel")),
        name="nsa_compress",
    )(x_arr, phi_big)


def _select_blocks(imp, blk, nb, n_sel):
    tq, nbp = imp.shape
    if tq % LANES != 0:
        rank = jnp.zeros((tq, nbp), F32)
        for mblk in range(nb):
            col = imp[:, mblk:mblk + 1]
            beats = (col > imp) | ((col == imp) & (blk > mblk))
            rank = rank + jnp.where(beats, 1.0, 0.0)
        return jnp.where(rank < n_sel, 1.0, 0.0)
    imp_t = imp.T
    n_parts = -(-nb // SUBLANES)
    parts = [imp_t[p * SUBLANES:(p + 1) * SUBLANES, :] for p in range(n_parts)]
    row_in_part = lax.broadcasted_iota(jnp.int32, (SUBLANES, tq), 0)
    ranks = [jnp.zeros((SUBLANES, tq), F32) for _ in range(n_parts)]
    for mblk in range(nb):
        pm, rm = divmod(mblk, SUBLANES)
        row = parts[pm][rm:rm + 1, :]
        for p in range(n_parts):
            if p < pm:
                beats = row > parts[p]
            elif p > pm:
                beats = row >= parts[p]
            else:
                beats = (row > parts[p]) | ((row == parts[p]) & (row_in_part > rm))
            ranks[p] = ranks[p] + jnp.where(beats, 1.0, 0.0)
    sel_t = [jnp.where(r < n_sel, 1.0, 0.0) for r in ranks]
    if nbp > n_parts * SUBLANES:
        sel_t.append(jnp.zeros((nbp - n_parts * SUBLANES, tq), F32))
    return jnp.concatenate(sel_t, axis=0).T


def _nsa_kernel(qi_ref, ki_ref, q_ref, sm_ref, kc_ref, ks_ref, win_ref, o_ref,
                qs_ref, sel_ref, part_ref, m_ref, l_ref, acc_ref,
                *, tq, tk, q0, lw, nb, nbp, n_sel, wl):
    p = pl.program_id(1)
    i = qi_ref[p]
    j = ki_ref[p]
    rq = NSA_R * tq
    qs0 = q0 + i * tq
    j_last = (qs0 + tq - 1) // tk
    scale = QSCALE2

    @pl.when(j == 0)
    def _():
        lane_half = lax.broadcasted_iota(jnp.int32, (tq, LANES), 1) // HD
        gates = _sigmoid(sm_ref[...])
        qpos_b = qs0 + lax.broadcasted_iota(jnp.int32, (tq, nbp), 0)
        blk = lax.broadcasted_iota(jnp.int32, (tq, nbp), 1)
        cmask = ((blk + 1) * CMP_BLOCK - 1 <= qpos_b) & (blk < nb)
        cur = qpos_b // CMP_BLOCK
        forced = (blk == 0) | (blk == cur)
        valid = blk <= cur
        start = pl.multiple_of(jnp.maximum(lw + (i + 1) * tq - wl, 0), SUBLANES)
        qpos_w = qs0 + lax.broadcasted_iota(jnp.int32, (tq, wl), 0)
        kpos_w = q0 - lw + start + lax.broadcasted_iota(jnp.int32, (tq, wl), 1)
        wmask = (kpos_w <= qpos_w) & (kpos_w > qpos_w - WINDOW)
        kw = win_ref[pl.ds(start, wl), 0:LANES].astype(BF16)
        vw = win_ref[pl.ds(start, wl), LANES:2 * LANES].astype(BF16)
        kc = kc_ref[:, 0:LANES].astype(BF16)
        vc = kc_ref[:, LANES:2 * LANES].astype(BF16)
        for g in range(NSA_G):
            for r in range(NSA_R):
                chunk, half = 2 * g + r // 2, r % 2
                qc = q_ref[:, chunk * LANES:(chunk + 1) * LANES]
                if half != g:
                    qc = pltpu.roll(qc, HD, 1)
                qs_ref[g, r * tq:(r + 1) * tq, :] = jnp.where(lane_half == g, qc * scale, 0.0)
            qs = qs_ref[g].astype(BF16)
            s_c = _dot_nt(qs, kc).reshape(NSA_R, tq, nbp)
            p_c = _masked_softmax(s_c, cmask[None])
            o_c = _dot(p_c.reshape(rq, nbp).astype(BF16), vc)
            imp = jnp.sum(p_c, axis=0)
            imp = jnp.where(forced, SEL_FORCE, jnp.where(valid, imp, -1.0))
            imp = jnp.where(blk < nb, imp, -2.0)
            sel_ref[g] = _select_blocks(imp, blk, nb, n_sel)
            s_w = _dot_nt(qs, kw).reshape(NSA_R, tq, wl)
            p_w = _masked_softmax(s_w, wmask[None])
            o_w = _dot(p_w.reshape(rq, wl).astype(BF16), vw)
            for r in range(NSA_R):
                rows = slice(r * tq, (r + 1) * tq)
                lane_c = SM_GATE + 0 * NSA_G * NSA_R + g * NSA_R + r
                lane_w = SM_GATE + 2 * NSA_G * NSA_R + g * NSA_R + r
                part_ref[g, rows, :] = (gates[:, lane_c:lane_c + 1] * o_c[rows]
                                        + gates[:, lane_w:lane_w + 1] * o_w[rows])
        m_ref[...] = jnp.full_like(m_ref, NEG)
        l_ref[...] = jnp.zeros_like(l_ref)
        acc_ref[...] = jnp.zeros_like(acc_ref)

    slab = tq if tq >= LANES else rq
    ks = ks_ref[:, 0:LANES].astype(BF16)
    vs = ks_ref[:, LANES:2 * LANES].astype(BF16)
    kblk = (j * tk + lax.broadcasted_iota(jnp.int32, (nbp, tk), 1)) // CMP_BLOCK
    expand = (lax.broadcasted_iota(jnp.int32, (nbp, tk), 0) == kblk).astype(BF16)
    qpos = qs0 + lax.broadcasted_iota(jnp.int32, (tq, tk), 0)
    kpos = j * tk + lax.broadcasted_iota(jnp.int32, (tq, tk), 1)
    causal = kpos <= qpos
    pieces = [(g, slice(r0, r0 + slab)) for g in range(NSA_G) for r0 in range(0, rq, slab)]
    biases = []
    for g in range(NSA_G):
        bias = jnp.where((_dot(sel_ref[g].astype(BF16), expand) > 0.5) & causal, 0.0, NEG)
        biases.append(bias if slab == tq else jnp.concatenate([bias] * (slab // tq), axis=0))
    scores = [_dot_nt(qs_ref[g, rows, :].astype(BF16), ks) + biases[g] for g, rows in pieces]
    probs, corrs = [], []
    for (g, rows), s in zip(pieces, scores):
        m_old = m_ref[g, rows, :]
        m_new = jnp.maximum(m_old, jnp.max(s, axis=-1, keepdims=True))
        p = jnp.exp2(s - m_new)
        corr = jnp.exp2(m_old - m_new)
        l_ref[g, rows, :] = corr * l_ref[g, rows, :] + jnp.sum(p, axis=-1, keepdims=True)
        m_ref[g, rows, :] = m_new
        probs.append(p.astype(BF16))
        corrs.append(corr)
    for (g, rows), p, corr in zip(pieces, probs, corrs):
        acc_ref[g, rows, :] = corr * acc_ref[g, rows, :] + _dot(p, vs)

    @pl.when(j == j_last)
    def _():
        gates = _sigmoid(sm_ref[...])
        lane_half = lax.broadcasted_iota(jnp.int32, (tq, LANES), 1) // HD
        for g in range(NSA_G):
            o_s = acc_ref[g] / jnp.maximum(l_ref[g], 1e-30)
            res = []
            for r in range(NSA_R):
                rows = slice(r * tq, (r + 1) * tq)
                lane_s = SM_GATE + 1 * NSA_G * NSA_R + g * NSA_R + r
                res.append(part_ref[g, rows, :] + gates[:, lane_s:lane_s + 1] * o_s[rows])
            for c2 in range(2):
                even, odd = res[2 * c2], res[2 * c2 + 1]
                if g == 1:
                    even = pltpu.roll(even, HD, 1)
                else:
                    odd = pltpu.roll(odd, HD, 1)
                chunk = 2 * g + c2
                o_ref[:, chunk * LANES:(chunk + 1) * LANES] = jnp.where(lane_half == 0, even, odd)


def _nsa_attend(q_arr, q_blk, sm_arr, sm_blk, kc_arr, ks_arr, ks_blk, win_arr, *, q0, lw, n_real_keys):
    b, t = q_arr.shape[:2]
    l = ks_arr.shape[1]
    nbp = kc_arr.shape[1]
    nb = -(-n_real_keys // CMP_BLOCK)
    n_sel = min(SEL_TOPN, nb)
    tq = _pick(t, 128)
    tk = _pick(l, 1024, LANES)
    wl = min(WINDOW + tq, lw + t)
    rq = NSA_R * tq
    pairs = [(i, j) for i in range(t // tq) for j in range((q0 + (i + 1) * tq - 1) // tk + 1)]
    qi = jnp.asarray([pr[0] for pr in pairs], jnp.int32)
    ki = jnp.asarray([pr[1] for pr in pairs], jnp.int32)
    return pl.pallas_call(
        functools.partial(_nsa_kernel, tq=tq, tk=tk, q0=q0, lw=lw, nb=nb, nbp=nbp, n_sel=n_sel, wl=wl),
        grid_spec=pltpu.PrefetchScalarGridSpec(
            num_scalar_prefetch=2, grid=(b, len(pairs)),
            in_specs=[
                pl.BlockSpec((None, tq, GW), lambda bi, p, qi_r, ki_r: (bi, qi_r[p], q_blk)),
                pl.BlockSpec((None, tq, LANES), lambda bi, p, qi_r, ki_r: (bi, qi_r[p], sm_blk)),
                pl.BlockSpec((None, nbp, 2 * LANES), lambda bi, p, qi_r, ki_r: (bi, 0, 0)),
                pl.BlockSpec((None, tk, 2 * LANES), lambda bi, p, qi_r, ki_r: (bi, ki_r[p], ks_blk)),
                pl.BlockSpec((None, lw + t, 2 * LANES), lambda bi, p, qi_r, ki_r: (bi, 0, 0)),
            ],
            out_specs=pl.BlockSpec((None, tq, GW), lambda bi, p, qi_r, ki_r: (bi, qi_r[p], 0)),
            scratch_shapes=[
                pltpu.VMEM((NSA_G, rq, LANES), F32),
                pltpu.VMEM((NSA_G, tq, nbp), F32),
                pltpu.VMEM((NSA_G, rq, LANES), F32),
                pltpu.VMEM((NSA_G, rq, 1), F32),
                pltpu.VMEM((NSA_G, rq, 1), F32),
                pltpu.VMEM((NSA_G, rq, LANES), F32),
            ]),
        out_shape=jax.ShapeDtypeStruct((b, t, GW), F32),
        compiler_params=_cparams(("parallel", "arbitrary")),
        name="nsa_attend",
    )(qi, ki, q_arr, sm_arr, kc_arr, ks_arr, win_arr)


def _lru_kernel(x_ref, gt_ref, buf_ref, h0_ref, cw_ref, cb_ref, wbd_ref, gb_ref, lam_ref,
                o_ref, hl_ref, nb_ref, xp_ref, a_ref, b_ref, h_ref, *, tt):
    s = pl.program_id(1)
    hist = CONV_PAD - (CONV_W - 1)

    @pl.when(s == 0)
    def _():
        xp_ref[hist:CONV_PAD, :] = buf_ref[...]
        h_ref[...] = h0_ref[...]

    xp_ref[CONV_PAD:CONV_PAD + tt, :] = x_ref[...]
    u = cb_ref[...]
    for jw in range(CONV_W):
        u = u + cw_ref[jw:jw + 1, :] * xp_ref[hist + jw:hist + jw + tt, :]
    gts = _dot(u.astype(BF16), wbd_ref[...]) + gb_ref[...]
    r = _sigmoid(gts[:, :GW])
    ig = _sigmoid(gts[:, GW:])
    log_a = -LRU_C * r * _softplus(-lam_ref[...])
    a = jnp.exp(log_a)
    a_ref[...] = a
    b_ref[...] = jnp.sqrt(-jnp.tanh(log_a) * (a * a + 1.0)) * (ig * u)

    def step(t, h):
        h = a_ref[pl.ds(t, 1), :] * h + b_ref[pl.ds(t, 1), :]
        b_ref[pl.ds(t, 1), :] = h
        return h

    h = lax.fori_loop(0, tt, step, h_ref[...], unroll=8)
    h_ref[...] = h
    o_ref[...] = b_ref[...] * _gelu_tanh(gt_ref[...])
    tail = xp_ref[tt + hist:tt + CONV_PAD, :]
    xp_ref[hist:CONV_PAD, :] = tail

    @pl.when(s == pl.num_programs(1) - 1)
    def _():
        hl_ref[...] = h
        nb_ref[...] = tail


def _rg_lru(h_arr, buf, h0, conv_w, conv_b, wbd, gate_b, lam, *, layer):
    b, t = h_arr.shape[:2]
    tt = _pick(t, 512)
    x_blk, g_blk = C_LRUX // GW, C_LRUG // GW
    wspec = lambda shape: pl.BlockSpec((None,) + shape, lambda bi, s: (layer,) + (0,) * len(shape))
    return pl.pallas_call(
        functools.partial(_lru_kernel, tt=tt),
        grid=(b, t // tt),
        in_specs=[
            pl.BlockSpec((None, tt, GW), lambda bi, s: (bi, s, x_blk)),
            pl.BlockSpec((None, tt, GW), lambda bi, s: (bi, s, g_blk)),
            pl.BlockSpec((None, CONV_W - 1, GW), lambda bi, s: (bi, 0, 0)),
            pl.BlockSpec((None, 1, GW), lambda bi, s: (bi, 0, 0)),
            wspec((CONV_W, GW)), wspec((1, GW)), wspec((GW, 2 * GW)), wspec((1, 2 * GW)), wspec((1, GW)),
        ],
        out_specs=[
            pl.BlockSpec((None, tt, GW), lambda bi, s: (bi, s, 0)),
            pl.BlockSpec((None, 1, GW), lambda bi, s: (bi, 0, 0)),
            pl.BlockSpec((None, CONV_W - 1, GW), lambda bi, s: (bi, 0, 0)),
        ],
        out_shape=[
            jax.ShapeDtypeStruct((b, t, GW), F32),
            jax.ShapeDtypeStruct((b, 1, GW), F32),
            jax.ShapeDtypeStruct((b, CONV_W - 1, GW), F32),
        ],
        scratch_shapes=[
            pltpu.VMEM((CONV_PAD + tt, GW), F32),
            pltpu.VMEM((tt, GW), F32),
            pltpu.VMEM((tt, GW), F32),
            pltpu.VMEM((1, GW), F32),
        ],
        compiler_params=_cparams(("parallel", "arbitrary")),
        name="rg_lru",
    )(h_arr, h_arr, buf, h0, conv_w, conv_b, wbd, gate_b, lam)


def _gdn_kernel(qkv_ref, z_ref, sm_ref, buf_ref, s0_ref, cw_ref, al_ref, dtb_ref, ng_ref,
                o_ref, sout_ref, nb_ref, xp_ref, smp_ref, st_ref, *, tt, tc):
    s = pl.program_id(1)
    hist = CONV_PAD - (CONV_W - 1)
    n_chunks = tc // GDN_CHUNK

    @pl.when(s == 0)
    def _():
        xp_ref[...] = jnp.zeros_like(xp_ref)
        smp_ref[...] = jnp.zeros_like(smp_ref)
        xp_ref[hist:CONV_PAD, :] = buf_ref[...]
        st_ref[...] = s0_ref[...]

    xp_ref[CONV_PAD:CONV_PAD + tt, :] = qkv_ref[...]
    smp_ref[0:tt, :] = sm_ref[...]
    row_ok = lax.broadcasted_iota(jnp.int32, (tc, 1), 0) < tt
    y = cw_ref[0:1, :] * xp_ref[hist:hist + tc, :]
    for jw in range(1, CONV_W):
        y = y + cw_ref[jw:jw + 1, :] * xp_ref[hist + jw:hist + jw + tc, :]
    y = jnp.where(row_ok, _silu(y), 0.0)
    sm = smp_ref[...]
    beta_all = jnp.where(row_ok, _sigmoid(sm), 0.0)
    g_all = jnp.where(row_ok, -jnp.exp(al_ref[...]) * _softplus(sm + dtb_ref[...]), 0.0)

    ri = lax.broadcasted_iota(jnp.int32, (tc, tc), 0)
    ci = lax.broadcasted_iota(jnp.int32, (tc, tc), 1)
    same = (ri // GDN_CHUNK) == (ci // GDN_CHUNK)
    incl = same & (ci <= ri)
    strict = same & (ci < ri)
    ones_where = lambda m: jnp.where(m, 1.0, 0.0).astype(BF16)
    gcum = _dot_hp_exact_lhs(ones_where(incl), g_all)
    gcum_t = _dot_hp_exact_rhs(g_all.T, ones_where(same & (ri <= ci)))
    gtot = _dot_hp_exact_lhs(ones_where(same), g_all)
    eye = jnp.where(ri == ci, 1.0, 0.0)

    heads = range(GDN_H)
    pws, tinvs, qks, rhss, qgs, kds, gts = [], [], [], [], [], [], []
    for h in heads:
        q = y[:, h * GDN_D:(h + 1) * GDN_D]
        k = y[:, GW + h * GDN_D:GW + (h + 1) * GDN_D]
        v = y[:, 2 * GW + h * GDN_D:2 * GW + (h + 1) * GDN_D]
        q = q * lax.rsqrt(jnp.sum(q * q, axis=-1, keepdims=True) + RMS_EPS) * (GDN_D ** -0.5)
        k = k * lax.rsqrt(jnp.sum(k * k, axis=-1, keepdims=True) + RMS_EPS)
        beta = beta_all[:, SM_BETA + h:SM_BETA + h + 1]
        gc = gcum[:, SM_DECAY + h:SM_DECAY + h + 1]
        gr = gcum_t[SM_DECAY + h:SM_DECAY + h + 1, :]
        gt = gtot[:, SM_DECAY + h:SM_DECAY + h + 1]
        decay = jnp.exp(jnp.where(incl, gc - gr, NEG))
        kb = k * beta
        kbf = k.astype(BF16)
        a_mat = jnp.where(strict, _dot_nt(kb.astype(BF16), kbf) * decay, 0.0)
        qks.append(jnp.where(incl, _dot_nt(q.astype(BF16), kbf) * decay, 0.0).astype(BF16))
        pws.append(-a_mat)
        tinvs.append(eye - a_mat)
        eg = jnp.exp(gc)
        rhss.append(jnp.concatenate([v * beta, kb * eg], axis=1))
        qgs.append((q * eg).astype(BF16))
        kds.append((k * jnp.exp(gt - gc)).astype(BF16))
        gts.append(gt)
    for _ in range(int(math.log2(GDN_CHUNK)) - 1):
        pws = [_dot_hp(pws[h], pws[h]) for h in heads]
        tinvs = [tinvs[h] + _dot_hp(tinvs[h], pws[h]) for h in heads]
    sols = [_dot_hp(tinvs[h], rhss[h]) for h in heads]
    us = [sols[h][:, :GDN_D] for h in heads]
    ws = [sols[h][:, GDN_D:].astype(BF16) for h in heads]
    sts = [st_ref[h] for h in heads]
    outs = [[] for _ in heads]
    for c in range(n_chunks):
        rows = slice(c * GDN_CHUNK, (c + 1) * GDN_CHUNK)
        stbs = [sts[h].astype(BF16) for h in heads]
        v_news = [(us[h][rows] - _dot(ws[h][rows], stbs[h])).astype(BF16) for h in heads]
        for h in heads:
            outs[h].append(_dot(qgs[h][rows], stbs[h]) + _dot(qks[h][rows, rows], v_news[h]))
        for h in heads:
            g_last = jnp.exp(gts[h][c * GDN_CHUNK:c * GDN_CHUNK + 1, :])
            sts[h] = sts[h] * g_last + _dot_tn(kds[h][rows], v_news[h])
    for h in heads:
        st_ref[h] = sts[h]
        o = outs[h][0] if n_chunks == 1 else jnp.concatenate(outs[h], axis=0)
        o = o * lax.rsqrt(jnp.mean(o * o, axis=-1, keepdims=True) + RMS_EPS) * ng_ref[...]
        o_ref[:, h * GDN_D:(h + 1) * GDN_D] = o[0:tt] * _silu(z_ref[:, h * GDN_D:(h + 1) * GDN_D])

    tail = xp_ref[tt + hist:tt + CONV_PAD, :]
    xp_ref[hist:CONV_PAD, :] = tail

    @pl.when(s == pl.num_programs(1) - 1)
    def _():
        sout_ref[...] = st_ref[...]
        nb_ref[...] = tail


def _gated_deltanet(h_arr, buf, s0, conv_w, a_log_row, dt_bias_row, norm_g, *, layer):
    b, t = h_arr.shape[:2]
    tt = _pick(t, 128)
    tc = -(-tt // GDN_CHUNK) * GDN_CHUNK
    assert tc == tt or t == tt, "sequence length must be a multiple of the chunk unless it fits one block"
    w3 = 3 * GW
    wspec = lambda shape: pl.BlockSpec((None,) + shape, lambda bi, s: (layer,) + (0,) * len(shape))
    return pl.pallas_call(
        functools.partial(_gdn_kernel, tt=tt, tc=tc),
        grid=(b, t // tt),
        in_specs=[
            pl.BlockSpec((None, tt, w3), lambda bi, s: (bi, s, C_GQKV // w3)),
            pl.BlockSpec((None, tt, GW), lambda bi, s: (bi, s, C_GZ // GW)),
            pl.BlockSpec((None, tt, LANES), lambda bi, s: (bi, s, C_SMALL // LANES)),
            pl.BlockSpec((None, CONV_W - 1, w3), lambda bi, s: (bi, 0, 0)),
            pl.BlockSpec((None, GDN_H, GDN_D, GDN_D), lambda bi, s: (bi, 0, 0, 0)),
            wspec((CONV_W, w3)), wspec((1, LANES)), wspec((1, LANES)), wspec((1, GDN_D)),
        ],
        out_specs=[
            pl.BlockSpec((None, tt, GW), lambda bi, s: (bi, s, 0)),
            pl.BlockSpec((None, GDN_H, GDN_D, GDN_D), lambda bi, s: (bi, 0, 0, 0)),
            pl.BlockSpec((None, CONV_W - 1, w3), lambda bi, s: (bi, 0, 0)),
        ],
        out_shape=[
            jax.ShapeDtypeStruct((b, t, GW), F32),
            jax.ShapeDtypeStruct((b, GDN_H, GDN_D, GDN_D), F32),
            jax.ShapeDtypeStruct((b, CONV_W - 1, w3), F32),
        ],
        scratch_shapes=[
            pltpu.VMEM((CONV_PAD + tc, w3), F32),
            pltpu.VMEM((tc, LANES), F32),
            pltpu.VMEM((GDN_H, GDN_D, GDN_D), F32),
        ],
        compiler_params=_cparams(("parallel", "arbitrary")),
        name="gated_deltanet",
    )(h_arr, h_arr, h_arr, buf, s0, conv_w, a_log_row, dt_bias_row, norm_g)


def _outproj_kernel(oa_ref, ob_ref, oc_ref, od_ref, x_ref, w_ref, gg_ref, g_ref, b_ref, o_ref, *, alpha):
    def rms(v, gain):
        return (v * lax.rsqrt(jnp.mean(v * v, axis=-1, keepdims=True) + RMS_EPS) * gain).astype(BF16)

    acc = _dot(rms(oa_ref[...], gg_ref[0:1, :]), w_ref[0:GW, :])
    acc += _dot(rms(ob_ref[...], gg_ref[1:2, :]), w_ref[GW:2 * GW, :])
    acc += _dot(rms(oc_ref[...], gg_ref[2:3, :]), w_ref[2 * GW:3 * GW, :])
    acc += _dot(od_ref[...].astype(BF16), w_ref[3 * GW:4 * GW, :])
    o_ref[...] = _layer_norm(alpha * x_ref[...] + acc, g_ref[...], b_ref[...])


def _outproj_ln(oa, ob, oc, od, x, w_out, grp_g, ln_g, ln_b, *, layer, alpha):
    n, d = x.shape
    tm = _pick(n, 512)
    mix_spec = pl.BlockSpec((tm, GW), lambda i: (i, 0))
    return pl.pallas_call(
        functools.partial(_outproj_kernel, alpha=alpha),
        grid=(n // tm,),
        in_specs=[
            mix_spec, mix_spec, mix_spec, mix_spec,
            pl.BlockSpec((tm, d), lambda i: (i, 0)),
            pl.BlockSpec((None, 4 * GW, d), lambda i: (layer, 0, 0)),
            pl.BlockSpec((None, 3, GW), lambda i: (layer, 0, 0)),
            pl.BlockSpec((None, None, 1, d), lambda i: (layer, 1, 0, 0)),
            pl.BlockSpec((None, None, 1, d), lambda i: (layer, 1, 0, 0)),
        ],
        out_specs=pl.BlockSpec((tm, d), lambda i: (i, 0)),
        out_shape=jax.ShapeDtypeStruct((n, d), F32),
        compiler_params=_cparams(("parallel",)),
        name="outproj_ln",
    )(oa, ob, oc, od, x, w_out, grp_g, ln_g, ln_b)


def _prep_weights(ffn_gu, ffn_down, w_in, w_out, nsa_phi, lru_gate_w, lru_gate_b, gdn_A_log, gdn_dt_bias):
    depth, d_model, _ = w_in.shape
    o_nsakv, o_gate, o_lrux, o_gqkv, o_gz, o_gb = 2048, 2816, 2840, 3864, 5400, 5912
    zeros = jnp.zeros((depth, d_model, P_IN_PAD - 5920), w_in.dtype)
    w_in_p = jnp.concatenate([
        w_in[..., 512:1536],
        w_in[..., 0:512],
        w_in[..., 1536:2048],
        w_in[..., o_nsakv:o_nsakv + 512],
        w_in[..., o_lrux:o_lrux + 1024],
        w_in[..., o_gz:o_gz + 512],
        w_in[..., o_nsakv + 512:o_gate],
        w_in[..., o_gate:o_lrux],
        w_in[..., o_gb:o_gb + 8],
        zeros,
        w_in[..., o_gqkv:o_gz],
    ], axis=-1).astype(BF16)
    phi_b = jnp.transpose(nsa_phi, (0, 2, 1, 3, 4)).astype(BF16)
    col_blocks = []
    for c in range(2):
        for g in range(NSA_G):
            hot = jnp.zeros((2, NSA_G, 1, 1), BF16).at[c, g].set(1)
            col_blocks.append(phi_b[:, :, :, None, :, :] * hot[None, None])
    phi_big = jnp.concatenate(col_blocks, axis=-1).reshape(depth, CMP_BLOCK * 2 * NSA_G * HD, 2 * NSA_G * HD)
    eye_b = jnp.eye(LRU_BLKS, dtype=F32)
    wbd = jnp.einsum("lknce,nm->lnckme", lru_gate_w, eye_b).reshape(depth, GW, 2 * GW).astype(BF16)
    gate_b = lru_gate_b.reshape(depth, 1, 2 * GW)
    pad_l = jnp.zeros((depth, SM_DECAY), F32)
    pad_r = jnp.zeros((depth, LANES - SM_DECAY - GDN_H), F32)
    a_log_row = jnp.concatenate([pad_l, gdn_A_log, pad_r], axis=1)[:, None, :]
    dt_bias_row = jnp.concatenate([pad_l, gdn_dt_bias, pad_r], axis=1)[:, None, :]
    return dict(
        wgu=ffn_gu, wd=ffn_down, w_in=w_in_p, w_out=w_out.astype(BF16),
        phi_big=phi_big, wbd=wbd, gate_b=gate_b, a_log_row=a_log_row, dt_bias_row=dt_bias_row)


def _trunk_layer(x, b, t, q0, past, wts, layer, alpha):
    ln_g, ln_b = wts["ln_g"], wts["ln_b"]
    x = _ffn_ln(x, wts["wgu"], wts["wd"], ln_g, ln_b, layer=layer, which=0, ln_idx=0, alpha=alpha)
    h = _matmul(x, wts["w_in"], layer=layer, name="in_proj")
    h3 = h.reshape(b, t, P_IN_PAD)

    sb_rows = h3[:, :, C_SBK:C_SBK + 2 * GW]
    nsa_rows = h3[:, :, C_NSAKV:C_NSAKV + GW]
    nsa_win_new = h3[:, :, C_NSAW:C_NSAW + 2 * LANES]

    if past["page_table"] is None:
        lw = 0
        o_a = _sb_attend(h3, C_SBQ // GW, h3, C_SBK // GW, h3, C_SBV // GW, q0=q0)
        cmp_rows, cmp_blk = h3, C_NSAKV // (2 * LANES)
        ks_arr, ks_blk = h3, (C_NSAKV + 2 * LANES) // (2 * LANES)
        win_all = nsa_win_new
    else:
        pt = past["page_table"]
        lw = past["nsa_win"].shape[2]
        o_a = _sb_attend_paged(h3, C_SBQ // GW, C_SBK // (2 * GW), past["sb_kv"], pt, layer=layer)
        cmp_rows, ks_arr = _gather_nsa_pages(past["nsa_kv"], pt, h3, C_NSAKV // GW, layer=layer)
        cmp_blk = ks_blk = 0
        win_all = jnp.concatenate([past["nsa_win"][layer], nsa_win_new], axis=1)

    nb_arr = cmp_rows.shape[1] // CMP_BLOCK
    kc = _nsa_compress(cmp_rows.reshape(b, nb_arr, CMP_BLOCK, cmp_rows.shape[2]), cmp_blk, wts["phi_big"],
                       layer=layer)
    nbp = -(-nb_arr // LANES) * LANES
    kc = jnp.pad(kc, ((0, 0), (0, nbp - nb_arr), (0, 0)))
    o_b = _nsa_attend(h3, C_NSAQ // GW, h3, C_SMALL // LANES, kc, ks_arr, ks_blk, win_all,
                      q0=q0, lw=lw, n_real_keys=q0 + t)
    new_win = win_all[:, -min(WINDOW, lw + t):]

    o_c, h_last, new_lru_buf = _rg_lru(h3, past["lru_conv"], past["lru_h"], wts["lru_conv_w"], wts["lru_conv_b"],
                                       wts["wbd"], wts["gate_b"], wts["lru_lambda"], layer=layer)
    o_d, s_new, new_gdn_buf = _gated_deltanet(h3, past["gdn_conv"], past["gdn_S"], wts["gdn_conv_w"],
                                              wts["a_log_row"], wts["dt_bias_row"], wts["gdn_norm_g"], layer=layer)

    n = b * t
    x = _outproj_ln(o_a.reshape(n, GW), o_b.reshape(n, GW), o_c.reshape(n, GW), o_d.reshape(n, GW), x,
                    wts["w_out"], wts["grp_norm_g"], ln_g, ln_b, layer=layer, alpha=alpha)
    x = _ffn_ln(x, wts["wgu"], wts["wd"], ln_g, ln_b, layer=layer, which=1, ln_idx=2, alpha=alpha)
    new_state = (
        sb_rows.reshape(b, t, 2, SB_H, HD),
        nsa_rows.reshape(b, t, 4, NSA_G, HD),
        new_win.reshape(b, new_win.shape[1], 2, NSA_G, HD),
        h_last.reshape(b, GW),
        new_lru_buf,
        s_new,
        new_gdn_buf,
    )
    return x, new_state


def kernel(x_prompt, x_sample, cache_sb_kv, cache_nsa_kv, cache_nsa_win, state_lru_h, state_lru_conv,
           state_gdn_S, state_gdn_conv, page_table, ln_g, ln_b, ffn_gu, ffn_down, w_in, w_out, grp_norm_g,
           nsa_phi, lru_conv_w, lru_conv_b, lru_gate_w, lru_gate_b, lru_lambda, gdn_conv_w, gdn_A_log,
           gdn_dt_bias, gdn_norm_g):
    depth, d_model = w_in.shape[0], w_in.shape[1]
    alpha = (2 * depth) ** 0.25
    n_b, seq = x_prompt.shape[:2]
    n_db, dec_seq = x_sample.shape[:2]
    past_len = page_table.shape[1] * PAGE
    n_pool = cache_sb_kv.shape[1]

    wts = _prep_weights(ffn_gu, ffn_down, w_in, w_out, nsa_phi, lru_gate_w, lru_gate_b, gdn_A_log, gdn_dt_bias)
    wts.update(
        ln_g=ln_g.reshape(depth, 3, 1, d_model), ln_b=ln_b.reshape(depth, 3, 1, d_model),
        grp_norm_g=grp_norm_g, lru_conv_w=lru_conv_w, lru_conv_b=lru_conv_b.reshape(depth, 1, GW),
        lru_lambda=lru_lambda.reshape(depth, 1, GW), gdn_conv_w=gdn_conv_w,
        gdn_norm_g=gdn_norm_g.reshape(depth, 1, GDN_D))

    sb_cache = jnp.transpose(cache_sb_kv, (0, 1, 3, 4, 5, 2))
    nsa_cache = jnp.transpose(cache_nsa_kv, (0, 1, 3, 4, 5, 2))
    nsa_win = cache_nsa_win.reshape(depth, n_db, cache_nsa_win.shape[2], 2 * LANES)

    y_p = x_prompt.reshape(n_b * seq, d_model)
    y_s = x_sample.reshape(n_db * dec_seq, d_model)
    st_p, st_s = [], []
    for l in range(depth):
        past_p = dict(page_table=None,
                      lru_h=jnp.zeros((n_b, 1, GW), F32), lru_conv=jnp.zeros((n_b, CONV_W - 1, GW), F32),
                      gdn_S=jnp.zeros((n_b, GDN_H, GDN_D, GDN_D), F32),
                      gdn_conv=jnp.zeros((n_b, CONV_W - 1, 3 * GW), F32))
        past_s = dict(page_table=page_table, sb_kv=sb_cache, nsa_kv=nsa_cache, nsa_win=nsa_win,
                      lru_h=state_lru_h[l].reshape(n_db, 1, GW), lru_conv=state_lru_conv[l],
                      gdn_S=state_gdn_S[l], gdn_conv=state_gdn_conv[l])
        y_p, new_p = _trunk_layer(y_p, n_b, seq, 0, past_p, wts, l, alpha)
        y_s, new_s = _trunk_layer(y_s, n_db, dec_seq, past_len, past_s, wts, l, alpha)
        st_p.append(new_p)
        st_s.append(new_s)
    p = [jnp.stack(a) for a in zip(*st_p)]
    s = [jnp.stack(a) for a in zip(*st_s)]
    return (y_p.reshape(n_b, seq, d_model), y_s.reshape(n_db, dec_seq, d_model),
            p[0], p[1], p[2], p[3], p[4], p[5], p[6], s[0], s[1], s[2], s[3], s[4], s[5], s[6])
```

```python
import functools
import math

import jax
import jax.numpy as jnp
from jax import lax
from jax.experimental import pallas as pl
from jax.experimental.pallas import tpu as pltpu

F32 = jnp.float32
BF16 = jnp.bfloat16

PAGE = 128
GW = 512
HD = 64
SB_H = 8
NSA_G = 2
NSA_R = 4
CMP_BLOCK = 64
CMP_PITCH = 72
SEL_TOPN = 16
SEL_FORCE = float(NSA_R + 1)
WINDOW = 512
LRU_BLKS = 8
LRU_C = 8.0
CONV_W = 4
GDN_H = 4
GDN_D = 128
GDN_CHUNK = 64
LN_EPS = 1e-5
RMS_EPS = 1e-6
NEG = -1e30
LOG2E = 1.4426950408889634
QSCALE2 = (HD ** -0.5) * LOG2E

LANES = 128
SUBLANES = 8
VMEM_LIMIT = 56 * 1024 * 1024

C_SBK, C_SBV, C_SBQ, C_NSAQ, C_NSAKV = 0, 512, 1024, 1536, 2048
C_LRUX, C_LRUG, C_GZ, C_NSAW, C_SMALL, C_GQKV = 2560, 3072, 3584, 4096, 4352, 4608
P_IN_PAD = 6144
SM_GATE, SM_BETA, SM_DECAY = 0, 24, 28
CONV_PAD = 8


def _pick(n, pref, mult=SUBLANES):
    if n <= pref:
        return n
    for t in range(pref, 0, -1):
        if n % t == 0 and t % mult == 0:
            return t
    return n


def _cparams(sem):
    return pltpu.CompilerParams(dimension_semantics=sem, vmem_limit_bytes=VMEM_LIMIT)


def _sigmoid(x):
    return 1.0 / (1.0 + jnp.exp(-x))


def _silu(x):
    return x * _sigmoid(x)


def _softplus(x):
    return jnp.maximum(x, 0.0) + jnp.log1p(jnp.exp(-jnp.abs(x)))


def _gelu_tanh(x):
    return 0.5 * x * (1.0 + jnp.tanh(math.sqrt(2.0 / math.pi) * (x + 0.044715 * (x * x * x))))


def _layer_norm(y, g, b):
    mu = jnp.mean(y, axis=-1, keepdims=True)
    d = y - mu
    var = jnp.mean(d * d, axis=-1, keepdims=True)
    return d * lax.rsqrt(var + LN_EPS) * g + b


def _dot(a, b):
    return jnp.dot(a, b, preferred_element_type=F32)


def _dot_nt(a, b):
    return lax.dot_general(a, b, (((1,), (1,)), ((), ())), preferred_element_type=F32)


def _dot_tn(a, b):
    return lax.dot_general(a, b, (((0,), (0,)), ((), ())), preferred_element_type=F32)


def _split_bf16(x):
    hi = x.astype(BF16)
    return hi, (x - hi.astype(F32)).astype(BF16)


def _dot_hp(a, b):
    ah, al = _split_bf16(a)
    bh, bl = _split_bf16(b)
    return _dot(jnp.concatenate([ah, ah, al], axis=1), jnp.concatenate([bh, bl, bh], axis=0))


def _dot_hp_exact_lhs(a_bf16, b):
    bh, bl = _split_bf16(b)
    return _dot(jnp.concatenate([a_bf16, a_bf16], axis=1), jnp.concatenate([bh, bl], axis=0))


def _dot_hp_exact_rhs(a, b_bf16):
    ah, al = _split_bf16(a)
    return _dot(jnp.concatenate([ah, al], axis=1), jnp.concatenate([b_bf16, b_bf16], axis=0))


def _ffn_ln_kernel(x_ref, wg_ref, wu_ref, wd_ref, g_ref, b_ref, o_ref, *rest, alpha, emit_bf16):
    xb_ref = rest[-1]
    j = pl.program_id(1)

    @pl.when(j == 0)
    def _():
        xb_ref[...] = x_ref[...].astype(BF16)
        o_ref[...] = jnp.zeros_like(o_ref)

    xb = xb_ref[...]
    wg, wu, wd = wg_ref[...].astype(BF16), wu_ref[...].astype(BF16), wd_ref[...].astype(BF16)
    if emit_bf16:
        rest[0][...], rest[1][...], rest[2][...] = wg, wu, wd
    act = (_silu(_dot(xb, wg)) * _dot(xb, wu)).astype(BF16)
    o_ref[...] += _dot(act, wd)

    @pl.when(j == pl.num_programs(1) - 1)
    def _():
        y = alpha * x_ref[...] + 0.5 * o_ref[...]
        o_ref[...] = _layer_norm(y, g_ref[...], b_ref[...])


def _ffn_ln(x, weights, ln_g, ln_b, *, layer, which, ln_idx, alpha, emit_bf16=False):
    n, d = x.shape
    tm = _pick(n, 1024)
    if len(weights) == 2:
        wgu, wdn = weights
        f = wdn.shape[2]
        tf = _pick(f, 256, LANES)
        nf = f // tf
        w_args = (wgu, wgu, wdn)
        w_specs = [
            pl.BlockSpec((None, None, d, tf), lambda i, j: (layer, which, 0, j)),
            pl.BlockSpec((None, None, d, tf), lambda i, j: (layer, which, 0, j + nf)),
            pl.BlockSpec((None, None, tf, d), lambda i, j: (layer, which, j, 0)),
        ]
    else:
        w_args = weights
        f = weights[2].shape[0]
        tf = _pick(f, 256, LANES)
        nf = f // tf
        w_specs = [
            pl.BlockSpec((d, tf), lambda i, j: (0, j)),
            pl.BlockSpec((d, tf), lambda i, j: (0, j)),
            pl.BlockSpec((tf, d), lambda i, j: (j, 0)),
        ]
    out_specs = [pl.BlockSpec((tm, d), lambda i, j: (i, 0))]
    out_shape = [jax.ShapeDtypeStruct((n, d), F32)]
    if emit_bf16:
        assert n == tm, "weight tiles are written once only when there is a single row block"
        out_specs += [pl.BlockSpec((d, tf), lambda i, j: (0, j)), pl.BlockSpec((d, tf), lambda i, j: (0, j)),
                      pl.BlockSpec((tf, d), lambda i, j: (j, 0))]
        out_shape += [jax.ShapeDtypeStruct((d, f), BF16), jax.ShapeDtypeStruct((d, f), BF16),
                      jax.ShapeDtypeStruct((f, d), BF16)]
    outs = pl.pallas_call(
        functools.partial(_ffn_ln_kernel, alpha=alpha, emit_bf16=emit_bf16),
        grid=(n // tm, nf),
        in_specs=[
            pl.BlockSpec((tm, d), lambda i, j: (i, 0), pipeline_mode=pl.Buffered(1)),
            *w_specs,
            pl.BlockSpec((None, None, 1, d), lambda i, j: (layer, ln_idx, 0, 0)),
            pl.BlockSpec((None, None, 1, d), lambda i, j: (layer, ln_idx, 0, 0)),
        ],
        out_specs=out_specs,
        out_shape=out_shape,
        scratch_shapes=[pltpu.VMEM((tm, d), BF16)],
        compiler_params=_cparams(("parallel", "arbitrary")),
        name="ffn_ln",
    )(x, *w_args, ln_g, ln_b)
    return (outs[0], tuple(outs[1:])) if emit_bf16 else outs[0]


def _matmul_kernel(x_ref, w_ref, o_ref):
    k = pl.program_id(2)

    @pl.when(k == 0)
    def _():
        o_ref[...] = jnp.zeros_like(o_ref)

    o_ref[...] += _dot(x_ref[...].astype(BF16), w_ref[...])


def _matmul(x, w, *, layer, tm_pref=1024, tn_pref=512, tk_pref=2048, name="matmul"):
    m, kdim = x.shape
    nout = w.shape[2]
    tm = _pick(m, tm_pref)
    tn = _pick(nout, tn_pref, LANES)
    tk = _pick(kdim, tk_pref, LANES)
    return pl.pallas_call(
        _matmul_kernel,
        grid=(m // tm, nout // tn, kdim // tk),
        in_specs=[
            pl.BlockSpec((tm, tk), lambda i, j, k: (i, k)),
            pl.BlockSpec((None, tk, tn), lambda i, j, k: (layer, k, j)),
        ],
        out_specs=pl.BlockSpec((tm, tn), lambda i, j, k: (i, j)),
        out_shape=jax.ShapeDtypeStruct((m, nout), F32),
        compiler_params=_cparams(("parallel", "parallel", "arbitrary")),
        name=name,
    )(x, w)


def _gather_nsa_kernel(pt_ref, *refs, n_group, t_new):
    del pt_ref
    page_refs = refs[:n_group]
    new_ref, cmp_ref = refs[n_group:]
    s = pl.program_id(1)
    last = pl.num_programs(1) - 1
    blocks_per_page = PAGE // CMP_BLOCK

    @pl.when(s < last)
    def _():
        cmp_ref[:, CMP_BLOCK:CMP_PITCH, :] = jnp.zeros((cmp_ref.shape[0], CMP_PITCH - CMP_BLOCK, 2 * LANES), F32)
        for g in range(n_group):
            for kind in range(2):
                x = page_refs[g][kind].reshape(NSA_G * HD, PAGE).T
                cols = slice(kind * LANES, (kind + 1) * LANES)
                for nl in range(blocks_per_page):
                    cmp_ref[g * blocks_per_page + nl, 0:CMP_BLOCK, cols] = x[nl * CMP_BLOCK:(nl + 1) * CMP_BLOCK]

    @pl.when(s == last)
    def _():
        cmp_ref[...] = jnp.zeros_like(cmp_ref)
        cmp_ref[0, 0:t_new, :] = new_ref[...]


def _gather_nsa_pages(cache_t, page_table, new_arr, new_colblk, *, layer):
    b, n_pages = page_table.shape
    t_new = new_arr.shape[1]
    assert t_new <= CMP_BLOCK
    n_group = math.gcd(n_pages, 8)
    n_steps = n_pages // n_group + 1
    blocks = n_group * PAGE // CMP_BLOCK

    def page_map(g):
        return lambda bi, s, pt: (layer, pt[bi, jnp.minimum(s * n_group + g, n_pages - 1)], 0, 0, 0, 0)

    in_specs = [pl.BlockSpec((None, None, 2, NSA_G, HD, PAGE), page_map(g)) for g in range(n_group)]
    in_specs.append(pl.BlockSpec((None, t_new, 2 * LANES), lambda bi, s, pt: (bi, 0, new_colblk)))
    return pl.pallas_call(
        functools.partial(_gather_nsa_kernel, n_group=n_group, t_new=t_new),
        grid_spec=pltpu.PrefetchScalarGridSpec(
            num_scalar_prefetch=1, grid=(b, n_steps), in_specs=in_specs,
            out_specs=pl.BlockSpec((None, blocks, CMP_PITCH, 2 * LANES), lambda bi, s, pt: (bi, s, 0, 0))),
        out_shape=jax.ShapeDtypeStruct((b, n_steps * blocks, CMP_PITCH, 2 * LANES), F32),
        compiler_params=_cparams(("parallel", "arbitrary")),
        name="gather_nsa_pages",
    )(page_table, *([cache_t] * n_group), new_arr)


def _suffix_matrix(n):
    j = lax.broadcasted_iota(jnp.int32, (2 * n, n), 0)
    s = lax.broadcasted_iota(jnp.int32, (2 * n, n), 1)
    return ((j > s) & ((j < n) | (j - n > s))).astype(BF16)


def _sb_neg_log(z2, mask):
    nl = jnp.maximum(z2, 0.0) + jnp.log2(1.0 + jnp.exp2(-jnp.abs(z2)))
    if mask is not None:
        nl = jnp.where(mask, nl, 0.0)
    hi = nl.astype(BF16)
    lo = (nl - hi.astype(F32)).astype(BF16)
    return nl, jnp.concatenate([hi, lo], axis=1)


def _sb_weight(z2, nl, tail, carry, mask):
    w = jnp.exp2(z2 - nl - tail - carry)
    if mask is not None:
        w = jnp.where(mask, w, 0.0)
    return w.astype(BF16)


def _sb_weights(z2, tri2, carry, mask):
    nl, hilo = _sb_neg_log(z2, mask)
    tail = _dot(hilo, tri2)
    return _sb_weight(z2, nl, tail, carry, mask), tail[:, 0:1] + nl[:, 0:1]


def _sb_kernel(qi_ref, ki_ref, q_ref, k_ref, v_ref, o_ref, acc_ref, c_ref, *, tq, tk, q0):
    p = pl.program_id(1)
    i = qi_ref[p]
    jt = ki_ref[p]
    qs0 = q0 + i * tq

    @pl.when(jt == (qs0 + tq - 2) // tk)
    def _():
        acc_ref[...] = jnp.zeros_like(acc_ref)
        c_ref[...] = jnp.zeros_like(c_ref)

    def tile(masked):
        mask = None
        if masked:
            qpos = qs0 + lax.broadcasted_iota(jnp.int32, (tq, tk), 0)
            kpos = jt * tk + lax.broadcasted_iota(jnp.int32, (tq, tk), 1)
            mask = kpos < qpos
        tri = _suffix_matrix(tk)
        lane_half = lax.broadcasted_iota(jnp.int32, (tq, LANES), 1) // HD
        cols = [slice((h // 2) * LANES, (h // 2 + 1) * LANES) for h in range(SB_H)]
        zs = []
        for h in range(SB_H):
            qc = jnp.where(lane_half == h % 2, q_ref[:, cols[h]] * QSCALE2, 0.0).astype(BF16)
            zs.append(_dot_nt(qc, k_ref[:, cols[h]].astype(BF16)))
        nls, hilos = [], []
        for h in range(SB_H):
            nl, hilo = _sb_neg_log(zs[h], mask)
            nls.append(nl)
            hilos.append(hilo)
        tails = [_dot(hilos[h], tri) for h in range(SB_H)]
        ws = []
        for h in range(SB_H):
            ws.append(_sb_weight(zs[h], nls[h], tails[h], c_ref[h], mask))
            c_ref[h] += tails[h][:, 0:1] + nls[h][:, 0:1]
        for h in range(SB_H):
            acc_ref[h] += _dot(ws[h], v_ref[:, cols[h]].astype(BF16))

    whole = (jt + 1) * tk <= qs0

    @pl.when(whole)
    def _():
        tile(False)

    @pl.when(jnp.logical_not(whole))
    def _():
        tile(True)

    @pl.when(jt == 0)
    def _():
        lane_half = lax.broadcasted_iota(jnp.int32, (tq, LANES), 1) // HD
        for c in range(SB_H // 2):
            o_ref[:, c * LANES:(c + 1) * LANES] = jnp.where(lane_half == 0, acc_ref[2 * c], acc_ref[2 * c + 1])


def _sb_attend(q_arr, q_blk, k_arr, k_blk, v_arr, v_blk, *, q0):
    b, t = q_arr.shape[:2]
    l = k_arr.shape[1]
    tq = _pick(t, 256)
    tk = _pick(l, 256, LANES)
    pairs = [(i, jt) for i in range(t // tq) for jt in range((q0 + (i + 1) * tq - 2) // tk, -1, -1)]
    qi = jnp.asarray([pr[0] for pr in pairs], jnp.int32)
    ki = jnp.asarray([pr[1] for pr in pairs], jnp.int32)
    return pl.pallas_call(
        functools.partial(_sb_kernel, tq=tq, tk=tk, q0=q0),
        grid_spec=pltpu.PrefetchScalarGridSpec(
            num_scalar_prefetch=2, grid=(b, len(pairs)),
            in_specs=[
                pl.BlockSpec((None, tq, GW), lambda bi, p, qi_r, ki_r: (bi, qi_r[p], q_blk)),
                pl.BlockSpec((None, tk, GW), lambda bi, p, qi_r, ki_r: (bi, ki_r[p], k_blk)),
                pl.BlockSpec((None, tk, GW), lambda bi, p, qi_r, ki_r: (bi, ki_r[p], v_blk)),
            ],
            out_specs=pl.BlockSpec((None, tq, GW), lambda bi, p, qi_r, ki_r: (bi, qi_r[p], 0)),
            scratch_shapes=[pltpu.VMEM((SB_H, tq, LANES), F32), pltpu.VMEM((SB_H, tq, 1), F32)]),
        out_shape=jax.ShapeDtypeStruct((b, t, GW), F32),
        compiler_params=_cparams(("parallel", "arbitrary")),
        name="sb_attend",
    )(qi, ki, q_arr, k_arr, v_arr)


def _sb_paged_kernel(pt_ref, q_ref, new_ref, *refs, n_group, t):
    del pt_ref
    page_refs = refs[:n_group]
    o_ref, qbd_ref, acc_ref, c_ref = refs[n_group:]
    s = pl.program_id(1)
    m = SB_H * t
    tri = _suffix_matrix(PAGE)

    @pl.when(s == 0)
    def _():
        lane_head = lax.broadcasted_iota(jnp.int32, (t, GW), 1) // HD
        q = q_ref[...] * QSCALE2
        for h in range(SB_H):
            qbd_ref[h * t:(h + 1) * t, :] = jnp.where(lane_head == h, q, 0.0)
        pad = jnp.zeros((PAGE - t, GW), F32)
        k_new = jnp.concatenate([new_ref[:, 0:GW], pad], axis=0).astype(BF16)
        v_new = jnp.concatenate([new_ref[:, GW:2 * GW], pad], axis=0).astype(BF16)
        row = lax.broadcasted_iota(jnp.int32, (m, PAGE), 0)
        mask = lax.broadcasted_iota(jnp.int32, (m, PAGE), 1) < row - (row // t) * t
        w, rs = _sb_weights(_dot_nt(qbd_ref[...].astype(BF16), k_new), tri, 0.0, mask)
        acc_ref[...] = _dot(w, v_new)
        c_ref[...] = rs

    qb = qbd_ref[...].astype(BF16)
    pages = range(n_group)
    zs = [_dot(qb, page_refs[g][0].reshape(GW, PAGE).astype(BF16)) for g in pages]
    nls, hilos = zip(*[_sb_neg_log(zs[g], None) for g in pages])
    tails = [_dot(hilos[g], tri) for g in pages]
    carries = [None] * n_group
    carry = c_ref[...]
    for g in reversed(pages):
        carries[g] = carry
        carry = carry + tails[g][:, 0:1] + nls[g][:, 0:1]
    c_ref[...] = carry
    ws = [_sb_weight(zs[g], nls[g], tails[g], carries[g], None) for g in pages]
    acc = acc_ref[...]
    for g in pages:
        acc = acc + _dot_nt(ws[g], page_refs[g][1].reshape(GW, PAGE).astype(BF16))
    acc_ref[...] = acc

    @pl.when(s == pl.num_programs(1) - 1)
    def _():
        lane_head = lax.broadcasted_iota(jnp.int32, (t, GW), 1) // HD
        out = jnp.zeros((t, GW), F32)
        for h in range(SB_H):
            out = out + jnp.where(lane_head == h, acc_ref[h * t:(h + 1) * t, :], 0.0)
        o_ref[...] = out


def _sb_attend_paged(q_arr, q_blk, new_blk, cache_t, page_table, *, layer):
    b, t = q_arr.shape[:2]
    n_pages = page_table.shape[1]
    n_group = math.gcd(n_pages, 8)
    n_steps = n_pages // n_group
    m = SB_H * t

    def page_map(g):
        return lambda bi, s, pt: (layer, pt[bi, n_pages - (s + 1) * n_group + g], 0, 0, 0, 0)

    in_specs = [
        pl.BlockSpec((None, t, GW), lambda bi, s, pt: (bi, 0, q_blk)),
        pl.BlockSpec((None, t, 2 * GW), lambda bi, s, pt: (bi, 0, new_blk)),
    ] + [pl.BlockSpec((None, None, 2, SB_H, HD, PAGE), page_map(g)) for g in range(n_group)]
    return pl.pallas_call(
        functools.partial(_sb_paged_kernel, n_group=n_group, t=t),
        grid_spec=pltpu.PrefetchScalarGridSpec(
            num_scalar_prefetch=1, grid=(b, n_steps), in_specs=in_specs,
            out_specs=pl.BlockSpec((None, t, GW), lambda bi, s, pt: (bi, 0, 0)),
            scratch_shapes=[pltpu.VMEM((m, GW), F32), pltpu.VMEM((m, GW), F32), pltpu.VMEM((m, 1), F32)]),
        out_shape=jax.ShapeDtypeStruct((b, t, GW), F32),
        compiler_params=_cparams(("parallel", "arbitrary")),
        name="sb_attend_paged",
    )(page_table, q_arr, q_arr, *([cache_t] * n_group))


def _masked_softmax(s2, mask):
    sm = jnp.where(mask, s2, NEG)
    m = jnp.max(sm, axis=-1, keepdims=True)
    e = jnp.where(mask, jnp.exp2(sm - m), 0.0)
    return e / jnp.maximum(jnp.sum(e, axis=-1, keepdims=True), 1e-30)


def _compress_kernel(xk_ref, xv_ref, w_ref, o_ref, *, pitch):
    width = 2 * LANES
    n = o_ref.shape[0]
    parts = []
    for j in range(CMP_BLOCK):
        rows = pl.ds(j, n, stride=pitch)
        xj = jnp.concatenate([xk_ref[rows, :], xv_ref[rows, :]], axis=1).astype(BF16)
        parts.append(_dot(xj, w_ref[j * width:(j + 1) * width, :]))
    while len(parts) > 1:
        parts = [parts[i] + parts[i + 1] for i in range(0, len(parts), 2)]
    o_ref[...] = parts[0]


def _nsa_compress(x_arr, x_blk, phi_big, *, layer):
    b, nb, pitch = x_arr.shape[:3]
    width = 2 * LANES
    tn = _pick(nb, 160)
    return pl.pallas_call(
        functools.partial(_compress_kernel, pitch=pitch),
        grid=(b, nb // tn),
        in_specs=[
            pl.BlockSpec((None, tn * pitch, LANES), lambda bi, i: (bi, i, 2 * x_blk)),
            pl.BlockSpec((None, tn * pitch, LANES), lambda bi, i: (bi, i, 2 * x_blk + 1)),
            pl.BlockSpec((None, CMP_BLOCK * width, width), lambda bi, i: (layer, 0, 0)),
        ],
        out_specs=pl.BlockSpec((None, tn, width), lambda bi, i: (bi, i, 0)),
        out_shape=jax.ShapeDtypeStruct((b, nb, width), F32),
        compiler_params=_cparams(("parallel", "parallel")),
        name="nsa_compress",
    )(*([x_arr.reshape(b, nb * pitch, x_arr.shape[3])] * 2), phi_big)


def _select_blocks(imp, blk, nb, n_sel):
    tq, nbp = imp.shape
    if tq % LANES != 0:
        rank = jnp.zeros((tq, nbp), F32)
        for mblk in range(nb):
            col = imp[:, mblk:mblk + 1]
            beats = (col > imp) | ((col == imp) & (blk > mblk))
            rank = rank + jnp.where(beats, 1.0, 0.0)
        return jnp.where(rank < n_sel, 1.0, 0.0)
    imp_t = imp.T
    n_parts = -(-nb // SUBLANES)
    parts = [imp_t[p * SUBLANES:(p + 1) * SUBLANES, :] for p in range(n_parts)]
    row_in_part = lax.broadcasted_iota(jnp.int32, (SUBLANES, tq), 0)
    ranks = [jnp.zeros((SUBLANES, tq), F32) for _ in range(n_parts)]
    for mblk in range(nb):
        pm, rm = divmod(mblk, SUBLANES)
        row = parts[pm][rm:rm + 1, :]
        for p in range(n_parts):
            if p < pm:
                beats = row > parts[p]
            elif p > pm:
                beats = row >= parts[p]
            else:
                beats = (row > parts[p]) | ((row == parts[p]) & (row_in_part > rm))
            ranks[p] = ranks[p] + jnp.where(beats, 1.0, 0.0)
    sel_t = [jnp.where(r < n_sel, 1.0, 0.0) for r in ranks]
    if nbp > n_parts * SUBLANES:
        sel_t.append(jnp.zeros((nbp - n_parts * SUBLANES, tq), F32))
    return jnp.concatenate(sel_t, axis=0).T


def _nsa_kernel(*refs, n_page_refs, tq, tk, q0, lw, nb, nbp, n_sel, wl):
    n_prefetch = 3 if n_page_refs else 2
    qi_ref, ki_ref = refs[0], refs[1]
    q_ref, sm_ref, kc_ref = refs[n_prefetch:n_prefetch + 3]
    key_refs = refs[n_prefetch + 3:n_prefetch + 4 + n_page_refs]
    win_ref, o_ref, qs_ref, sel_ref, part_ref, m_ref, l_ref, acc_ref = refs[n_prefetch + 4 + n_page_refs:]
    p = pl.program_id(1)
    i = qi_ref[p]
    j = ki_ref[p]
    rq = NSA_R * tq
    qs0 = q0 + i * tq
    j_last = (qs0 + tq - 1) // tk
    scale = QSCALE2

    @pl.when(j == 0)
    def _():
        lane_half = lax.broadcasted_iota(jnp.int32, (tq, LANES), 1) // HD
        gates = _sigmoid(sm_ref[...])
        qpos_b = qs0 + lax.broadcasted_iota(jnp.int32, (tq, nbp), 0)
        blk = lax.broadcasted_iota(jnp.int32, (tq, nbp), 1)
        cmask = ((blk + 1) * CMP_BLOCK - 1 <= qpos_b) & (blk < nb)
        cur = qpos_b // CMP_BLOCK
        forced = (blk == 0) | (blk == cur)
        valid = blk <= cur
        start = pl.multiple_of(jnp.maximum(lw + (i + 1) * tq - wl, 0), SUBLANES)
        qpos_w = qs0 + lax.broadcasted_iota(jnp.int32, (tq, wl), 0)
        kpos_w = q0 - lw + start + lax.broadcasted_iota(jnp.int32, (tq, wl), 1)
        wmask = (kpos_w <= qpos_w) & (kpos_w > qpos_w - WINDOW)
        kw = win_ref[pl.ds(start, wl), 0:LANES].astype(BF16)
        vw = win_ref[pl.ds(start, wl), LANES:2 * LANES].astype(BF16)
        kc = kc_ref[:, 0:LANES].astype(BF16)
        vc = kc_ref[:, LANES:2 * LANES].astype(BF16)
        for g in range(NSA_G):
            for r in range(NSA_R):
                chunk, half = 2 * g + r // 2, r % 2
                qc = q_ref[:, chunk * LANES:(chunk + 1) * LANES]
                if half != g:
                    qc = pltpu.roll(qc, HD, 1)
                qs_ref[g, r * tq:(r + 1) * tq, :] = jnp.where(lane_half == g, qc * scale, 0.0)
            qs = qs_ref[g].astype(BF16)
            s_c = _dot_nt(qs, kc).reshape(NSA_R, tq, nbp)
            p_c = _masked_softmax(s_c, cmask[None])
            o_c = _dot(p_c.reshape(rq, nbp).astype(BF16), vc)
            imp = jnp.sum(p_c, axis=0)
            imp = jnp.where(forced, SEL_FORCE, jnp.where(valid, imp, -1.0))
            imp = jnp.where(blk < nb, imp, -2.0)
            sel_ref[g] = _select_blocks(imp, blk, nb, n_sel)
            s_w = _dot_nt(qs, kw).reshape(NSA_R, tq, wl)
            p_w = _masked_softmax(s_w, wmask[None])
            o_w = _dot(p_w.reshape(rq, wl).astype(BF16), vw)
            for r in range(NSA_R):
                rows = slice(r * tq, (r + 1) * tq)
                lane_c = SM_GATE + 0 * NSA_G * NSA_R + g * NSA_R + r
                lane_w = SM_GATE + 2 * NSA_G * NSA_R + g * NSA_R + r
                part_ref[g, rows, :] = (gates[:, lane_c:lane_c + 1] * o_c[rows]
                                        + gates[:, lane_w:lane_w + 1] * o_w[rows])
        m_ref[...] = jnp.full_like(m_ref, NEG)
        l_ref[...] = jnp.zeros_like(l_ref)
        acc_ref[...] = jnp.zeros_like(acc_ref)

    slab = tq if tq >= LANES else rq
    pieces = [(g, slice(r0, r0 + slab)) for g in range(NSA_G) for r0 in range(0, rq, slab)]

    def sel_tile(n, kpos0, score_fn, pv_fn):
        kblk = (kpos0 + lax.broadcasted_iota(jnp.int32, (nbp, n), 1)) // CMP_BLOCK
        expand = (lax.broadcasted_iota(jnp.int32, (nbp, n), 0) == kblk).astype(BF16)
        qpos = qs0 + lax.broadcasted_iota(jnp.int32, (tq, n), 0)
        kpos = kpos0 + lax.broadcasted_iota(jnp.int32, (tq, n), 1)
        causal = kpos <= qpos
        biases = []
        for g in range(NSA_G):
            bias = jnp.where((_dot(sel_ref[g].astype(BF16), expand) > 0.5) & causal, 0.0, NEG)
            biases.append(bias if slab == tq else jnp.concatenate([bias] * (slab // tq), axis=0))
        scores = [score_fn(qs_ref[g, rows, :].astype(BF16)) + biases[g] for g, rows in pieces]
        probs, corrs = [], []
        for (g, rows), s in zip(pieces, scores):
            m_old = m_ref[g, rows, :]
            m_new = jnp.maximum(m_old, jnp.max(s, axis=-1, keepdims=True))
            pr = jnp.exp2(s - m_new)
            corr = jnp.exp2(m_old - m_new)
            l_ref[g, rows, :] = corr * l_ref[g, rows, :] + jnp.sum(pr, axis=-1, keepdims=True)
            m_ref[g, rows, :] = m_new
            probs.append(pr.astype(BF16))
            corrs.append(corr)
        for (g, rows), pr, corr in zip(pieces, probs, corrs):
            acc_ref[g, rows, :] = corr * acc_ref[g, rows, :] + pv_fn(pr)

    if n_page_refs == 0:
        ks = key_refs[0][:, 0:LANES].astype(BF16)
        vs = key_refs[0][:, LANES:2 * LANES].astype(BF16)
        sel_tile(tk, j * tk, lambda q: _dot_nt(q, ks), lambda pr: _dot(pr, vs))
    else:
        n_past_tiles = q0 // tk

        @pl.when(j < n_past_tiles)
        def _():
            kst = jnp.concatenate([r[0].reshape(NSA_G * HD, PAGE) for r in key_refs[:-1]], axis=1).astype(BF16)
            vst = jnp.concatenate([r[1].reshape(NSA_G * HD, PAGE) for r in key_refs[:-1]], axis=1).astype(BF16)
            sel_tile(tk, j * tk, lambda q: _dot(q, kst), lambda pr: _dot_nt(pr, vst))

        @pl.when(j == n_past_tiles)
        def _():
            pad = jnp.zeros((PAGE - tq, LANES), F32)
            ks = jnp.concatenate([key_refs[-1][:, 0:LANES], pad], axis=0).astype(BF16)
            vs = jnp.concatenate([key_refs[-1][:, LANES:2 * LANES], pad], axis=0).astype(BF16)
            sel_tile(PAGE, q0, lambda q: _dot_nt(q, ks), lambda pr: _dot(pr, vs))

    @pl.when(j == j_last)
    def _():
        gates = _sigmoid(sm_ref[...])
        lane_half = lax.broadcasted_iota(jnp.int32, (tq, LANES), 1) // HD
        for g in range(NSA_G):
            o_s = acc_ref[g] / jnp.maximum(l_ref[g], 1e-30)
            res = []
            for r in range(NSA_R):
                rows = slice(r * tq, (r + 1) * tq)
                lane_s = SM_GATE + 1 * NSA_G * NSA_R + g * NSA_R + r
                res.append(part_ref[g, rows, :] + gates[:, lane_s:lane_s + 1] * o_s[rows])
            for c2 in range(2):
                even, odd = res[2 * c2], res[2 * c2 + 1]
                if g == 1:
                    even = pltpu.roll(even, HD, 1)
                else:
                    odd = pltpu.roll(odd, HD, 1)
                chunk = 2 * g + c2
                o_ref[:, chunk * LANES:(chunk + 1) * LANES] = jnp.where(lane_half == 0, even, odd)


def _nsa_attend(q_arr, q_blk, sm_arr, sm_blk, kc_arr, ks_arr, ks_blk, win_arr, *, q0, lw, n_real_keys,
                paged=None):
    b, t = q_arr.shape[:2]
    nbp = kc_arr.shape[1]
    nb = -(-n_real_keys // CMP_BLOCK)
    n_sel = min(SEL_TOPN, nb)
    tq = _pick(t, 128)
    wl = min(WINDOW + tq, lw + t)
    rq = NSA_R * tq
    if paged is None:
        tk = _pick(ks_arr.shape[1], 1024, LANES)
        n_page_refs, prefetch = 0, ()
        key_specs = [pl.BlockSpec((None, tk, 2 * LANES), lambda bi, p, qi_r, ki_r: (bi, ki_r[p], ks_blk))]
        key_args = (ks_arr,)
    else:
        cache_t, page_table, layer = paged
        n_pages = page_table.shape[1]
        n_page_refs = math.gcd(n_pages, 8)
        tk = n_page_refs * PAGE
        assert tq == t and q0 == n_pages * PAGE and q0 % tk == 0 and t <= PAGE
        prefetch = (page_table,)

        def page_map(g):
            def index_map(bi, p, qi_r, ki_r, pt):
                return (layer, pt[bi, jnp.minimum(ki_r[p] * n_page_refs + g, n_pages - 1)], 1, 0, 0, 0)
            return index_map

        key_specs = [pl.BlockSpec((None, None, 2, NSA_G, HD, PAGE), page_map(g)) for g in range(n_page_refs)]
        key_specs.append(pl.BlockSpec((None, t, 2 * LANES), lambda bi, p, qi_r, ki_r, pt: (bi, 0, ks_blk)))
        key_args = (cache_t,) * n_page_refs + (ks_arr,)
    pairs = [(i, j) for i in range(t // tq) for j in range((q0 + (i + 1) * tq - 1) // tk + 1)]
    qi = jnp.asarray([pr[0] for pr in pairs], jnp.int32)
    ki = jnp.asarray([pr[1] for pr in pairs], jnp.int32)
    return pl.pallas_call(
        functools.partial(_nsa_kernel, n_page_refs=n_page_refs, tq=tq, tk=tk, q0=q0, lw=lw, nb=nb, nbp=nbp,
                          n_sel=n_sel, wl=wl),
        grid_spec=pltpu.PrefetchScalarGridSpec(
            num_scalar_prefetch=2 + len(prefetch), grid=(b, len(pairs)),
            in_specs=[
                pl.BlockSpec((None, tq, GW), lambda bi, p, qi_r, ki_r, *_: (bi, qi_r[p], q_blk)),
                pl.BlockSpec((None, tq, LANES), lambda bi, p, qi_r, ki_r, *_: (bi, qi_r[p], sm_blk)),
                pl.BlockSpec((None, nbp, 2 * LANES), lambda bi, p, qi_r, ki_r, *_: (bi, 0, 0)),
                *key_specs,
                pl.BlockSpec((None, lw + t, 2 * LANES), lambda bi, p, qi_r, ki_r, *_: (bi, 0, 0)),
            ],
            out_specs=pl.BlockSpec((None, tq, GW), lambda bi, p, qi_r, ki_r, *_: (bi, qi_r[p], 0)),
            scratch_shapes=[
                pltpu.VMEM((NSA_G, rq, LANES), F32),
                pltpu.VMEM((NSA_G, tq, nbp), F32),
                pltpu.VMEM((NSA_G, rq, LANES), F32),
                pltpu.VMEM((NSA_G, rq, 1), F32),
                pltpu.VMEM((NSA_G, rq, 1), F32),
                pltpu.VMEM((NSA_G, rq, LANES), F32),
            ]),
        out_shape=jax.ShapeDtypeStruct((b, t, GW), F32),
        compiler_params=_cparams(("parallel", "arbitrary")),
        name="nsa_attend",
    )(qi, ki, *prefetch, q_arr, sm_arr, kc_arr, *key_args, win_arr)


def _lru_kernel(x_ref, gt_ref, buf_ref, h0_ref, cw_ref, cb_ref, wbd_ref, gb_ref, lam_ref,
                o_ref, hl_ref, nb_ref, xp_ref, a_ref, b_ref, h_ref, *, tt):
    s = pl.program_id(1)
    hist = CONV_PAD - (CONV_W - 1)

    @pl.when(s == 0)
    def _():
        xp_ref[hist:CONV_PAD, :] = buf_ref[...]
        h_ref[...] = h0_ref[...]

    xp_ref[CONV_PAD:CONV_PAD + tt, :] = x_ref[...]
    u = cb_ref[...]
    for jw in range(CONV_W):
        u = u + cw_ref[jw:jw + 1, :] * xp_ref[hist + jw:hist + jw + tt, :]
    gts = _dot(u.astype(BF16), wbd_ref[...]) + gb_ref[...]
    r = _sigmoid(gts[:, :GW])
    ig = _sigmoid(gts[:, GW:])
    log_a = -LRU_C * r * _softplus(-lam_ref[...])
    a = jnp.exp(log_a)
    a_ref[...] = a
    b_ref[...] = jnp.sqrt(-jnp.tanh(log_a) * (a * a + 1.0)) * (ig * u)

    def step(t, h):
        h = a_ref[pl.ds(t, 1), :] * h + b_ref[pl.ds(t, 1), :]
        b_ref[pl.ds(t, 1), :] = h
        return h

    h = lax.fori_loop(0, tt, step, h_ref[...], unroll=8)
    h_ref[...] = h
    o_ref[...] = b_ref[...] * _gelu_tanh(gt_ref[...])
    tail = xp_ref[tt + hist:tt + CONV_PAD, :]
    xp_ref[hist:CONV_PAD, :] = tail

    @pl.when(s == pl.num_programs(1) - 1)
    def _():
        hl_ref[...] = h
        nb_ref[...] = tail


def _rg_lru(h_arr, buf, h0, conv_w, conv_b, wbd, gate_b, lam, *, layer):
    b, t = h_arr.shape[:2]
    tt = _pick(t, 512)
    x_blk, g_blk = C_LRUX // GW, C_LRUG // GW
    wspec = lambda shape: pl.BlockSpec((None,) + shape, lambda bi, s: (layer,) + (0,) * len(shape))
    return pl.pallas_call(
        functools.partial(_lru_kernel, tt=tt),
        grid=(b, t // tt),
        in_specs=[
            pl.BlockSpec((None, tt, GW), lambda bi, s: (bi, s, x_blk)),
            pl.BlockSpec((None, tt, GW), lambda bi, s: (bi, s, g_blk)),
            pl.BlockSpec((None, CONV_W - 1, GW), lambda bi, s: (bi, 0, 0)),
            pl.BlockSpec((None, 1, GW), lambda bi, s: (bi, 0, 0)),
            wspec((CONV_W, GW)), wspec((1, GW)), wspec((GW, 2 * GW)), wspec((1, 2 * GW)), wspec((1, GW)),
        ],
        out_specs=[
            pl.BlockSpec((None, tt, GW), lambda bi, s: (bi, s, 0)),
            pl.BlockSpec((None, 1, GW), lambda bi, s: (bi, 0, 0)),
            pl.BlockSpec((None, CONV_W - 1, GW), lambda bi, s: (bi, 0, 0)),
        ],
        out_shape=[
            jax.ShapeDtypeStruct((b, t, GW), F32),
            jax.ShapeDtypeStruct((b, 1, GW), F32),
            jax.ShapeDtypeStruct((b, CONV_W - 1, GW), F32),
        ],
        scratch_shapes=[
            pltpu.VMEM((CONV_PAD + tt, GW), F32),
            pltpu.VMEM((tt, GW), F32),
            pltpu.VMEM((tt, GW), F32),
            pltpu.VMEM((1, GW), F32),
        ],
        compiler_params=_cparams(("parallel", "arbitrary")),
        name="rg_lru",
    )(h_arr, h_arr, buf, h0, conv_w, conv_b, wbd, gate_b, lam)


def _gdn_kernel(qkv_ref, z_ref, sm_ref, buf_ref, s0_ref, cw_ref, al_ref, dtb_ref, ng_ref,
                o_ref, sout_ref, nb_ref, xp_ref, smp_ref, st_ref, *, tt, tc):
    s = pl.program_id(1)
    hist = CONV_PAD - (CONV_W - 1)
    n_chunks = tc // GDN_CHUNK

    @pl.when(s == 0)
    def _():
        xp_ref[...] = jnp.zeros_like(xp_ref)
        smp_ref[...] = jnp.zeros_like(smp_ref)
        xp_ref[hist:CONV_PAD, :] = buf_ref[...]
        st_ref[...] = s0_ref[...]

    xp_ref[CONV_PAD:CONV_PAD + tt, :] = qkv_ref[...]
    smp_ref[0:tt, :] = sm_ref[...]
    row_ok = lax.broadcasted_iota(jnp.int32, (tc, 1), 0) < tt
    y = cw_ref[0:1, :] * xp_ref[hist:hist + tc, :]
    for jw in range(1, CONV_W):
        y = y + cw_ref[jw:jw + 1, :] * xp_ref[hist + jw:hist + jw + tc, :]
    y = jnp.where(row_ok, _silu(y), 0.0)
    sm = smp_ref[...]
    beta_all = jnp.where(row_ok, _sigmoid(sm), 0.0)
    g_all = jnp.where(row_ok, -jnp.exp(al_ref[...]) * _softplus(sm + dtb_ref[...]), 0.0)

    ri = lax.broadcasted_iota(jnp.int32, (tc, tc), 0)
    ci = lax.broadcasted_iota(jnp.int32, (tc, tc), 1)
    same = (ri // GDN_CHUNK) == (ci // GDN_CHUNK)
    incl = same & (ci <= ri)
    strict = same & (ci < ri)
    ones_where = lambda m: jnp.where(m, 1.0, 0.0).astype(BF16)
    gcum = _dot_hp_exact_lhs(ones_where(incl), g_all)
    gcum_t = _dot_hp_exact_rhs(g_all.T, ones_where(same & (ri <= ci)))
    gtot = _dot_hp_exact_lhs(ones_where(same), g_all)
    eye = jnp.where(ri == ci, 1.0, 0.0)

    heads = range(GDN_H)
    pws, tinvs, qks, rhss, qgs, kds, gts = [], [], [], [], [], [], []
    for h in heads:
        q = y[:, h * GDN_D:(h + 1) * GDN_D]
        k = y[:, GW + h * GDN_D:GW + (h + 1) * GDN_D]
        v = y[:, 2 * GW + h * GDN_D:2 * GW + (h + 1) * GDN_D]
        q = q * lax.rsqrt(jnp.sum(q * q, axis=-1, keepdims=True) + RMS_EPS) * (GDN_D ** -0.5)
        k = k * lax.rsqrt(jnp.sum(k * k, axis=-1, keepdims=True) + RMS_EPS)
        beta = beta_all[:, SM_BETA + h:SM_BETA + h + 1]
        gc = gcum[:, SM_DECAY + h:SM_DECAY + h + 1]
        gr = gcum_t[SM_DECAY + h:SM_DECAY + h + 1, :]
        gt = gtot[:, SM_DECAY + h:SM_DECAY + h + 1]
        decay = jnp.exp(jnp.where(incl, gc - gr, NEG))
        kb = k * beta
        kbf = k.astype(BF16)
        a_mat = jnp.where(strict, _dot_nt(kb.astype(BF16), kbf) * decay, 0.0)
        qks.append(jnp.where(incl, _dot_nt(q.astype(BF16), kbf) * decay, 0.0).astype(BF16))
        pws.append(-a_mat)
        tinvs.append(eye - a_mat)
        eg = jnp.exp(gc)
        rhss.append(jnp.concatenate([v * beta, kb * eg], axis=1))
        qgs.append((q * eg).astype(BF16))
        kds.append((k * jnp.exp(gt - gc)).astype(BF16))
        gts.append(gt)
    for _ in range(int(math.log2(GDN_CHUNK)) - 1):
        pws = [_dot_hp(pws[h], pws[h]) for h in heads]
        tinvs = [tinvs[h] + _dot_hp(tinvs[h], pws[h]) for h in heads]
    sols = [_dot_hp(tinvs[h], rhss[h]) for h in heads]
    us = [sols[h][:, :GDN_D] for h in heads]
    ws = [sols[h][:, GDN_D:].astype(BF16) for h in heads]
    sts = [st_ref[h] for h in heads]
    outs = [[] for _ in heads]
    for c in range(n_chunks):
        rows = slice(c * GDN_CHUNK, (c + 1) * GDN_CHUNK)
        stbs = [sts[h].astype(BF16) for h in heads]
        v_news = [(us[h][rows] - _dot(ws[h][rows], stbs[h])).astype(BF16) for h in heads]
        for h in heads:
            outs[h].append(_dot(qgs[h][rows], stbs[h]) + _dot(qks[h][rows, rows], v_news[h]))
        for h in heads:
            g_last = jnp.exp(gts[h][c * GDN_CHUNK:c * GDN_CHUNK + 1, :])
            sts[h] = sts[h] * g_last + _dot_tn(kds[h][rows], v_news[h])
    for h in heads:
        st_ref[h] = sts[h]
        o = outs[h][0] if n_chunks == 1 else jnp.concatenate(outs[h], axis=0)
        o = o * lax.rsqrt(jnp.mean(o * o, axis=-1, keepdims=True) + RMS_EPS) * ng_ref[...]
        o_ref[:, h * GDN_D:(h + 1) * GDN_D] = o[0:tt] * _silu(z_ref[:, h * GDN_D:(h + 1) * GDN_D])

    tail = xp_ref[tt + hist:tt + CONV_PAD, :]
    xp_ref[hist:CONV_PAD, :] = tail

    @pl.when(s == pl.num_programs(1) - 1)
    def _():
        sout_ref[...] = st_ref[...]
        nb_ref[...] = tail


def _gated_deltanet(h_arr, buf, s0, conv_w, a_log_row, dt_bias_row, norm_g, *, layer):
    b, t = h_arr.shape[:2]
    tt = _pick(t, 128)
    tc = -(-tt // GDN_CHUNK) * GDN_CHUNK
    assert tc == tt or t == tt, "sequence length must be a multiple of the chunk unless it fits one block"
    w3 = 3 * GW
    wspec = lambda shape: pl.BlockSpec((None,) + shape, lambda bi, s: (layer,) + (0,) * len(shape))
    return pl.pallas_call(
        functools.partial(_gdn_kernel, tt=tt, tc=tc),
        grid=(b, t // tt),
        in_specs=[
            pl.BlockSpec((None, tt, w3), lambda bi, s: (bi, s, C_GQKV // w3)),
            pl.BlockSpec((None, tt, GW), lambda bi, s: (bi, s, C_GZ // GW)),
            pl.BlockSpec((None, tt, LANES), lambda bi, s: (bi, s, C_SMALL // LANES)),
            pl.BlockSpec((None, CONV_W - 1, w3), lambda bi, s: (bi, 0, 0)),
            pl.BlockSpec((None, GDN_H, GDN_D, GDN_D), lambda bi, s: (bi, 0, 0, 0)),
            wspec((CONV_W, w3)), wspec((1, LANES)), wspec((1, LANES)), wspec((1, GDN_D)),
        ],
        out_specs=[
            pl.BlockSpec((None, tt, GW), lambda bi, s: (bi, s, 0)),
            pl.BlockSpec((None, GDN_H, GDN_D, GDN_D), lambda bi, s: (bi, 0, 0, 0)),
            pl.BlockSpec((None, CONV_W - 1, w3), lambda bi, s: (bi, 0, 0)),
        ],
        out_shape=[
            jax.ShapeDtypeStruct((b, t, GW), F32),
            jax.ShapeDtypeStruct((b, GDN_H, GDN_D, GDN_D), F32),
            jax.ShapeDtypeStruct((b, CONV_W - 1, w3), F32),
        ],
        scratch_shapes=[
            pltpu.VMEM((CONV_PAD + tc, w3), F32),
            pltpu.VMEM((tc, LANES), F32),
            pltpu.VMEM((GDN_H, GDN_D, GDN_D), F32),
        ],
        compiler_params=_cparams(("parallel", "arbitrary")),
        name="gated_deltanet",
    )(h_arr, h_arr, h_arr, buf, s0, conv_w, a_log_row, dt_bias_row, norm_g)


def _outproj_kernel(oa_ref, ob_ref, oc_ref, od_ref, x_ref, w_ref, gg_ref, g_ref, b_ref, o_ref, *, alpha):
    def rms(v, gain):
        return (v * lax.rsqrt(jnp.mean(v * v, axis=-1, keepdims=True) + RMS_EPS) * gain).astype(BF16)

    acc = _dot(rms(oa_ref[...], gg_ref[0:1, :]), w_ref[0:GW, :])
    acc += _dot(rms(ob_ref[...], gg_ref[1:2, :]), w_ref[GW:2 * GW, :])
    acc += _dot(rms(oc_ref[...], gg_ref[2:3, :]), w_ref[2 * GW:3 * GW, :])
    acc += _dot(od_ref[...].astype(BF16), w_ref[3 * GW:4 * GW, :])
    o_ref[...] = _layer_norm(alpha * x_ref[...] + acc, g_ref[...], b_ref[...])


def _outproj_ln(oa, ob, oc, od, x, w_out, grp_g, ln_g, ln_b, *, layer, alpha):
    n, d = x.shape
    tm = _pick(n, 512)
    mix_spec = pl.BlockSpec((tm, GW), lambda i: (i, 0))
    return pl.pallas_call(
        functools.partial(_outproj_kernel, alpha=alpha),
        grid=(n // tm,),
        in_specs=[
            mix_spec, mix_spec, mix_spec, mix_spec,
            pl.BlockSpec((tm, d), lambda i: (i, 0)),
            pl.BlockSpec((None, 4 * GW, d), lambda i: (layer, 0, 0)),
            pl.BlockSpec((None, 3, GW), lambda i: (layer, 0, 0)),
            pl.BlockSpec((None, None, 1, d), lambda i: (layer, 1, 0, 0)),
            pl.BlockSpec((None, None, 1, d), lambda i: (layer, 1, 0, 0)),
        ],
        out_specs=pl.BlockSpec((tm, d), lambda i: (i, 0)),
        out_shape=jax.ShapeDtypeStruct((n, d), F32),
        compiler_params=_cparams(("parallel",)),
        name="outproj_ln",
    )(oa, ob, oc, od, x, w_out, grp_g, ln_g, ln_b)


def _prep_weights(ffn_gu, ffn_down, w_in, w_out, nsa_phi, lru_gate_w, lru_gate_b, gdn_A_log, gdn_dt_bias):
    depth, d_model, _ = w_in.shape
    o_nsakv, o_gate, o_lrux, o_gqkv, o_gz, o_gb = 2048, 2816, 2840, 3864, 5400, 5912
    zeros = jnp.zeros((depth, d_model, P_IN_PAD - 5920), w_in.dtype)
    w_in_p = jnp.concatenate([
        w_in[..., 512:1536],
        w_in[..., 0:512],
        w_in[..., 1536:2048],
        w_in[..., o_nsakv:o_nsakv + 512],
        w_in[..., o_lrux:o_lrux + 1024],
        w_in[..., o_gz:o_gz + 512],
        w_in[..., o_nsakv + 512:o_gate],
        w_in[..., o_gate:o_lrux],
        w_in[..., o_gb:o_gb + 8],
        zeros,
        w_in[..., o_gqkv:o_gz],
    ], axis=-1).astype(BF16)
    phi_b = jnp.transpose(nsa_phi, (0, 2, 1, 3, 4)).astype(BF16)
    col_blocks = []
    for c in range(2):
        for g in range(NSA_G):
            hot = jnp.zeros((2, NSA_G, 1, 1), BF16).at[c, g].set(1)
            col_blocks.append(phi_b[:, :, :, None, :, :] * hot[None, None])
    phi_big = jnp.concatenate(col_blocks, axis=-1).reshape(depth, CMP_BLOCK * 2 * NSA_G * HD, 2 * NSA_G * HD)
    eye_b = jnp.eye(LRU_BLKS, dtype=F32)
    wbd = jnp.einsum("lknce,nm->lnckme", lru_gate_w, eye_b).reshape(depth, GW, 2 * GW).astype(BF16)
    gate_b = lru_gate_b.reshape(depth, 1, 2 * GW)
    pad_l = jnp.zeros((depth, SM_DECAY), F32)
    pad_r = jnp.zeros((depth, LANES - SM_DECAY - GDN_H), F32)
    a_log_row = jnp.concatenate([pad_l, gdn_A_log, pad_r], axis=1)[:, None, :]
    dt_bias_row = jnp.concatenate([pad_l, gdn_dt_bias, pad_r], axis=1)[:, None, :]
    return dict(
        wgu=ffn_gu, wd=ffn_down, w_in=w_in_p, w_out=w_out.astype(BF16),
        phi_big=phi_big, wbd=wbd, gate_b=gate_b, a_log_row=a_log_row, dt_bias_row=dt_bias_row)


def _trunk_layer(x, b, t, q0, past, wts, layer, alpha, ffn_bf16=None):
    ln_g, ln_b = wts["ln_g"], wts["ln_b"]
    emit = ffn_bf16 is None
    ffn_w = [(wts["wgu"], wts["wd"])] * 2 if emit else ffn_bf16
    x = _ffn_ln(x, ffn_w[0], ln_g, ln_b, layer=layer, which=0, ln_idx=0, alpha=alpha, emit_bf16=emit)
    if emit:
        x, emitted0 = x
    h = _matmul(x, wts["w_in"], layer=layer, name="in_proj")
    h3 = h.reshape(b, t, P_IN_PAD)

    sb_rows = h3[:, :, C_SBK:C_SBK + 2 * GW]
    nsa_rows = h3[:, :, C_NSAKV:C_NSAKV + GW]
    nsa_win_new = h3[:, :, C_NSAW:C_NSAW + 2 * LANES]

    if past["page_table"] is None:
        lw = 0
        o_a = _sb_attend(h3, C_SBQ // GW, h3, C_SBK // GW, h3, C_SBV // GW, q0=q0)
        cmp_rows, cmp_blk = h3.reshape(b, t // CMP_BLOCK, CMP_BLOCK, P_IN_PAD), C_NSAKV // (2 * LANES)
        win_all = nsa_win_new
        paged = None
    else:
        pt = past["page_table"]
        lw = past["nsa_win"].shape[2]
        o_a = _sb_attend_paged(h3, C_SBQ // GW, C_SBK // (2 * GW), past["sb_kv"], pt, layer=layer)
        cmp_rows = _gather_nsa_pages(past["nsa_kv"], pt, h3, C_NSAKV // (2 * LANES), layer=layer)
        cmp_blk = 0
        win_all = jnp.concatenate([past["nsa_win"][layer], nsa_win_new], axis=1)
        paged = (past["nsa_kv"], pt, layer)
    ks_blk = (C_NSAKV + 2 * LANES) // (2 * LANES)

    kc = _nsa_compress(cmp_rows, cmp_blk, wts["phi_big"], layer=layer)
    nb_arr = kc.shape[1]
    nbp = -(-nb_arr // LANES) * LANES
    kc = jnp.pad(kc, ((0, 0), (0, nbp - nb_arr), (0, 0)))
    o_b = _nsa_attend(h3, C_NSAQ // GW, h3, C_SMALL // LANES, kc, h3, ks_blk, win_all,
                      q0=q0, lw=lw, n_real_keys=q0 + t, paged=paged)
    new_win = win_all[:, -min(WINDOW, lw + t):]

    o_c, h_last, new_lru_buf = _rg_lru(h3, past["lru_conv"], past["lru_h"], wts["lru_conv_w"], wts["lru_conv_b"],
                                       wts["wbd"], wts["gate_b"], wts["lru_lambda"], layer=layer)
    o_d, s_new, new_gdn_buf = _gated_deltanet(h3, past["gdn_conv"], past["gdn_S"], wts["gdn_conv_w"],
                                              wts["a_log_row"], wts["dt_bias_row"], wts["gdn_norm_g"], layer=layer)

    n = b * t
    x = _outproj_ln(o_a.reshape(n, GW), o_b.reshape(n, GW), o_c.reshape(n, GW), o_d.reshape(n, GW), x,
                    wts["w_out"], wts["grp_norm_g"], ln_g, ln_b, layer=layer, alpha=alpha)
    x = _ffn_ln(x, ffn_w[1], ln_g, ln_b, layer=layer, which=1, ln_idx=2, alpha=alpha, emit_bf16=emit)
    if emit:
        x, emitted1 = x
    new_state = (
        sb_rows.reshape(b, t, 2, SB_H, HD),
        nsa_rows.reshape(b, t, 4, NSA_G, HD),
        new_win.reshape(b, new_win.shape[1], 2, NSA_G, HD),
        h_last.reshape(b, GW),
        new_lru_buf,
        s_new,
        new_gdn_buf,
    )
    return x, new_state, ([emitted0, emitted1] if emit else None)


def kernel(x_prompt, x_sample, cache_sb_kv, cache_nsa_kv, cache_nsa_win, state_lru_h, state_lru_conv,
           state_gdn_S, state_gdn_conv, page_table, ln_g, ln_b, ffn_gu, ffn_down, w_in, w_out, grp_norm_g,
           nsa_phi, lru_conv_w, lru_conv_b, lru_gate_w, lru_gate_b, lru_lambda, gdn_conv_w, gdn_A_log,
           gdn_dt_bias, gdn_norm_g):
    depth, d_model = w_in.shape[0], w_in.shape[1]
    alpha = (2 * depth) ** 0.25
    n_b, seq = x_prompt.shape[:2]
    n_db, dec_seq = x_sample.shape[:2]
    past_len = page_table.shape[1] * PAGE
    n_pool = cache_sb_kv.shape[1]

    wts = _prep_weights(ffn_gu, ffn_down, w_in, w_out, nsa_phi, lru_gate_w, lru_gate_b, gdn_A_log, gdn_dt_bias)
    wts.update(
        ln_g=ln_g.reshape(depth, 3, 1, d_model), ln_b=ln_b.reshape(depth, 3, 1, d_model),
        grp_norm_g=grp_norm_g, lru_conv_w=lru_conv_w, lru_conv_b=lru_conv_b.reshape(depth, 1, GW),
        lru_lambda=lru_lambda.reshape(depth, 1, GW), gdn_conv_w=gdn_conv_w,
        gdn_norm_g=gdn_norm_g.reshape(depth, 1, GDN_D))

    sb_cache = jnp.transpose(cache_sb_kv, (0, 1, 3, 4, 5, 2))
    nsa_cache = jnp.transpose(cache_nsa_kv, (0, 1, 3, 4, 5, 2))
    nsa_win = cache_nsa_win.reshape(depth, n_db, cache_nsa_win.shape[2], 2 * LANES)

    y_p = x_prompt.reshape(n_b * seq, d_model)
    y_s = x_sample.reshape(n_db * dec_seq, d_model)
    st_p, st_s = [], []
    for l in range(depth):
        past_p = dict(page_table=None,
                      lru_h=jnp.zeros((n_b, 1, GW), F32), lru_conv=jnp.zeros((n_b, CONV_W - 1, GW), F32),
                      gdn_S=jnp.zeros((n_b, GDN_H, GDN_D, GDN_D), F32),
                      gdn_conv=jnp.zeros((n_b, CONV_W - 1, 3 * GW), F32))
        past_s = dict(page_table=page_table, sb_kv=sb_cache, nsa_kv=nsa_cache, nsa_win=nsa_win,
                      lru_h=state_lru_h[l].reshape(n_db, 1, GW), lru_conv=state_lru_conv[l],
                      gdn_S=state_gdn_S[l], gdn_conv=state_gdn_conv[l])
        y_s, new_s, ffn_bf16 = _trunk_layer(y_s, n_db, dec_seq, past_len, past_s, wts, l, alpha)
        y_p, new_p, _ = _trunk_layer(y_p, n_b, seq, 0, past_p, wts, l, alpha, ffn_bf16=ffn_bf16)
        st_p.append(new_p)
        st_s.append(new_s)
    p = [jnp.stack(a) for a in zip(*st_p)]
    s = [jnp.stack(a) for a in zip(*st_s)]
    return (y_p.reshape(n_b, seq, d_model), y_s.reshape(n_db, dec_seq, d_model),
            p[0], p[1], p[2], p[3], p[4], p[5], p[6], s[0], s[1], s[2], s[3], s[4], s[5], s[6])
```

```python
import functools
import math

import jax
import jax.numpy as jnp
from jax import lax
from jax.experimental import pallas as pl
from jax.experimental.pallas import tpu as pltpu

F32 = jnp.float32
BF16 = jnp.bfloat16

PAGE = 128
GW = 512
HD = 64
SB_H = 8
NSA_G = 2
NSA_R = 4
CMP_BLOCK = 64
CMP_PITCH = 72
SEL_TOPN = 16
SEL_FORCE = float(NSA_R + 1)
WINDOW = 512
LRU_BLKS = 8
LRU_C = 8.0
CONV_W = 4
GDN_H = 4
GDN_D = 128
GDN_CHUNK = 64
LN_EPS = 1e-5
RMS_EPS = 1e-6
NEG = -1e30
LOG2E = 1.4426950408889634
QSCALE2 = (HD ** -0.5) * LOG2E

LANES = 128
SUBLANES = 8
VMEM_LIMIT = 56 * 1024 * 1024

C_SBK, C_SBV, C_SBQ, C_NSAQ, C_NSAKV = 0, 512, 1024, 1536, 2048
C_LRUX, C_LRUG, C_GZ, C_NSAW, C_SMALL, C_GQKV = 2560, 3072, 3584, 4096, 4352, 4608
P_IN_PAD = 6144
SM_GATE, SM_BETA, SM_DECAY = 0, 24, 28
CONV_PAD = 8
PAGES_PER_STEP = 16


def _pick(n, pref, mult=SUBLANES):
    if n <= pref:
        return n
    for t in range(pref, 0, -1):
        if n % t == 0 and t % mult == 0:
            return t
    return n


def _cparams(sem):
    return pltpu.CompilerParams(dimension_semantics=sem, vmem_limit_bytes=VMEM_LIMIT)


def _sigmoid(x):
    return 1.0 / (1.0 + jnp.exp(-x))


def _silu(x):
    return x * _sigmoid(x)


def _softplus(x):
    return jnp.maximum(x, 0.0) + jnp.log1p(jnp.exp(-jnp.abs(x)))


def _gelu_tanh(x):
    return 0.5 * x * (1.0 + jnp.tanh(math.sqrt(2.0 / math.pi) * (x + 0.044715 * (x * x * x))))


def _layer_norm(y, g, b):
    mu = jnp.mean(y, axis=-1, keepdims=True)
    d = y - mu
    var = jnp.mean(d * d, axis=-1, keepdims=True)
    return d * lax.rsqrt(var + LN_EPS) * g + b


def _dot(a, b):
    return jnp.dot(a, b, preferred_element_type=F32)


def _dot_nt(a, b):
    return lax.dot_general(a, b, (((1,), (1,)), ((), ())), preferred_element_type=F32)


def _dot_tn(a, b):
    return lax.dot_general(a, b, (((0,), (0,)), ((), ())), preferred_element_type=F32)


def _split_bf16(x):
    hi = x.astype(BF16)
    return hi, (x - hi.astype(F32)).astype(BF16)


def _dot_hp(a, b):
    ah, al = _split_bf16(a)
    bh, bl = _split_bf16(b)
    return _dot(jnp.concatenate([ah, ah, al], axis=1), jnp.concatenate([bh, bl, bh], axis=0))


def _dot_hp_exact_lhs(a_bf16, b):
    bh, bl = _split_bf16(b)
    return _dot(jnp.concatenate([a_bf16, a_bf16], axis=1), jnp.concatenate([bh, bl], axis=0))


def _dot_hp_exact_rhs(a, b_bf16):
    ah, al = _split_bf16(a)
    return _dot(jnp.concatenate([ah, al], axis=1), jnp.concatenate([b_bf16, b_bf16], axis=0))


def _ffn_ln_kernel(x_ref, wg_ref, wu_ref, wd_ref, g_ref, b_ref, o_ref, *rest, alpha, emit_bf16):
    xb_ref = rest[-1]
    j = pl.program_id(1)

    @pl.when(j == 0)
    def _():
        xb_ref[...] = x_ref[...].astype(BF16)
        o_ref[...] = jnp.zeros_like(o_ref)

    xb = xb_ref[...]
    wg, wu, wd = wg_ref[...].astype(BF16), wu_ref[...].astype(BF16), wd_ref[...].astype(BF16)
    if emit_bf16:
        rest[0][...], rest[1][...], rest[2][...] = wg, wu, wd
    act = (_silu(_dot(xb, wg)) * _dot(xb, wu)).astype(BF16)
    o_ref[...] += _dot(act, wd)

    @pl.when(j == pl.num_programs(1) - 1)
    def _():
        y = alpha * x_ref[...] + 0.5 * o_ref[...]
        o_ref[...] = _layer_norm(y, g_ref[...], b_ref[...])


def _ffn_ln(x, weights, ln_g, ln_b, *, layer, which, ln_idx, alpha, emit_bf16=False):
    n, d = x.shape
    tm = _pick(n, 1024)
    if len(weights) == 2:
        wgu, wdn = weights
        f = wdn.shape[2]
        tf = _pick(f, 256, LANES)
        nf = f // tf
        w_args = (wgu, wgu, wdn)
        w_specs = [
            pl.BlockSpec((None, None, d, tf), lambda i, j: (layer, which, 0, j)),
            pl.BlockSpec((None, None, d, tf), lambda i, j: (layer, which, 0, j + nf)),
            pl.BlockSpec((None, None, tf, d), lambda i, j: (layer, which, j, 0)),
        ]
    else:
        w_args = weights
        f = weights[2].shape[0]
        tf = _pick(f, 256, LANES)
        nf = f // tf
        w_specs = [
            pl.BlockSpec((d, tf), lambda i, j: (0, j)),
            pl.BlockSpec((d, tf), lambda i, j: (0, j)),
            pl.BlockSpec((tf, d), lambda i, j: (j, 0)),
        ]
    out_specs = [pl.BlockSpec((tm, d), lambda i, j: (i, 0))]
    out_shape = [jax.ShapeDtypeStruct((n, d), F32)]
    if emit_bf16:
        assert n == tm, "weight tiles are written once only when there is a single row block"
        out_specs += [pl.BlockSpec((d, tf), lambda i, j: (0, j)), pl.BlockSpec((d, tf), lambda i, j: (0, j)),
                      pl.BlockSpec((tf, d), lambda i, j: (j, 0))]
        out_shape += [jax.ShapeDtypeStruct((d, f), BF16), jax.ShapeDtypeStruct((d, f), BF16),
                      jax.ShapeDtypeStruct((f, d), BF16)]
    outs = pl.pallas_call(
        functools.partial(_ffn_ln_kernel, alpha=alpha, emit_bf16=emit_bf16),
        grid=(n // tm, nf),
        in_specs=[
            pl.BlockSpec((tm, d), lambda i, j: (i, 0)),
            *w_specs,
            pl.BlockSpec((None, None, 1, d), lambda i, j: (layer, ln_idx, 0, 0)),
            pl.BlockSpec((None, None, 1, d), lambda i, j: (layer, ln_idx, 0, 0)),
        ],
        out_specs=out_specs,
        out_shape=out_shape,
        scratch_shapes=[pltpu.VMEM((tm, d), BF16)],
        compiler_params=_cparams(("parallel", "arbitrary")),
        name="ffn_ln",
    )(x, *w_args, ln_g, ln_b)
    return (outs[0], tuple(outs[1:])) if emit_bf16 else outs[0]


def _matmul_kernel(x_ref, w_ref, o_ref):
    k = pl.program_id(2)

    @pl.when(k == 0)
    def _():
        o_ref[...] = jnp.zeros_like(o_ref)

    o_ref[...] += _dot(x_ref[...].astype(BF16), w_ref[...])


def _matmul(x, w, *, layer, tm_pref=1024, tn_pref=512, tk_pref=2048, name="matmul"):
    m, kdim = x.shape
    nout = w.shape[2]
    tm = _pick(m, tm_pref)
    tn = _pick(nout, tn_pref, LANES)
    tk = _pick(kdim, tk_pref, LANES)
    return pl.pallas_call(
        _matmul_kernel,
        grid=(m // tm, nout // tn, kdim // tk),
        in_specs=[
            pl.BlockSpec((tm, tk), lambda i, j, k: (i, k)),
            pl.BlockSpec((None, tk, tn), lambda i, j, k: (layer, k, j)),
        ],
        out_specs=pl.BlockSpec((tm, tn), lambda i, j, k: (i, j)),
        out_shape=jax.ShapeDtypeStruct((m, nout), F32),
        compiler_params=_cparams(("parallel", "parallel", "arbitrary")),
        name=name,
    )(x, w)


def _gather_nsa_kernel(pt_ref, *refs, n_group, t_new):
    del pt_ref
    page_refs = refs[:n_group]
    new_ref, cmp_ref = refs[n_group:]
    s = pl.program_id(1)
    last = pl.num_programs(1) - 1
    blocks_per_page = PAGE // CMP_BLOCK

    @pl.when(s < last)
    def _():
        cmp_ref[:, CMP_BLOCK:CMP_PITCH, :] = jnp.zeros((cmp_ref.shape[0], CMP_PITCH - CMP_BLOCK, 2 * LANES), F32)
        for g in range(n_group):
            for kind in range(2):
                x = page_refs[g][kind].reshape(NSA_G * HD, PAGE).T
                cols = slice(kind * LANES, (kind + 1) * LANES)
                for nl in range(blocks_per_page):
                    cmp_ref[g * blocks_per_page + nl, 0:CMP_BLOCK, cols] = x[nl * CMP_BLOCK:(nl + 1) * CMP_BLOCK]

    @pl.when(s == last)
    def _():
        cmp_ref[...] = jnp.zeros_like(cmp_ref)
        cmp_ref[0, 0:t_new, :] = new_ref[...]


def _gather_nsa_pages(cache_t, page_table, new_arr, new_colblk, *, layer):
    b, n_pages = page_table.shape
    t_new = new_arr.shape[1]
    assert t_new <= CMP_BLOCK
    n_group = math.gcd(n_pages, 8)
    n_steps = n_pages // n_group + 1
    blocks = n_group * PAGE // CMP_BLOCK

    def page_map(g):
        return lambda bi, s, pt: (layer, pt[bi, jnp.minimum(s * n_group + g, n_pages - 1)], 0, 0, 0, 0)

    in_specs = [pl.BlockSpec((None, None, 2, NSA_G, HD, PAGE), page_map(g)) for g in range(n_group)]
    in_specs.append(pl.BlockSpec((None, t_new, 2 * LANES), lambda bi, s, pt: (bi, 0, new_colblk)))
    return pl.pallas_call(
        functools.partial(_gather_nsa_kernel, n_group=n_group, t_new=t_new),
        grid_spec=pltpu.PrefetchScalarGridSpec(
            num_scalar_prefetch=1, grid=(b, n_steps), in_specs=in_specs,
            out_specs=pl.BlockSpec((None, blocks, CMP_PITCH, 2 * LANES), lambda bi, s, pt: (bi, s, 0, 0))),
        out_shape=jax.ShapeDtypeStruct((b, n_steps * blocks, CMP_PITCH, 2 * LANES), F32),
        compiler_params=_cparams(("parallel", "arbitrary")),
        name="gather_nsa_pages",
    )(page_table, *([cache_t] * n_group), new_arr)


def _suffix_matrix(n):
    j = lax.broadcasted_iota(jnp.int32, (2 * n, n), 0)
    s = lax.broadcasted_iota(jnp.int32, (2 * n, n), 1)
    return ((j > s) & ((j < n) | (j - n > s))).astype(BF16)


def _sb_neg_log(z2, mask):
    nl = jnp.maximum(z2, 0.0) + jnp.log2(1.0 + jnp.exp2(-jnp.abs(z2)))
    if mask is not None:
        nl = jnp.where(mask, nl, 0.0)
    hi = nl.astype(BF16)
    lo = (nl - hi.astype(F32)).astype(BF16)
    return nl, jnp.concatenate([hi, lo], axis=1)


def _sb_weight(z2, nl, tail, carry, mask):
    w = jnp.exp2(z2 - nl - tail - carry)
    if mask is not None:
        w = jnp.where(mask, w, 0.0)
    return w.astype(BF16)


def _sb_weights(z2, tri2, carry, mask):
    nl, hilo = _sb_neg_log(z2, mask)
    tail = _dot(hilo, tri2)
    return _sb_weight(z2, nl, tail, carry, mask), tail[:, 0:1] + nl[:, 0:1]


def _sb_kernel(qi_ref, ki_ref, q_ref, k_ref, v_ref, o_ref, acc_ref, c_ref, *, tq, tk, q0):
    p = pl.program_id(1)
    i = qi_ref[p]
    jt = ki_ref[p]
    qs0 = q0 + i * tq

    @pl.when(jt == (qs0 + tq - 2) // tk)
    def _():
        acc_ref[...] = jnp.zeros_like(acc_ref)
        c_ref[...] = jnp.zeros_like(c_ref)

    def tile(masked):
        mask = None
        if masked:
            qpos = qs0 + lax.broadcasted_iota(jnp.int32, (tq, tk), 0)
            kpos = jt * tk + lax.broadcasted_iota(jnp.int32, (tq, tk), 1)
            mask = kpos < qpos
        tri = _suffix_matrix(tk)
        lane_half = lax.broadcasted_iota(jnp.int32, (tq, LANES), 1) // HD
        cols = [slice((h // 2) * LANES, (h // 2 + 1) * LANES) for h in range(SB_H)]
        zs = []
        for h in range(SB_H):
            qc = jnp.where(lane_half == h % 2, q_ref[:, cols[h]] * QSCALE2, 0.0).astype(BF16)
            zs.append(_dot_nt(qc, k_ref[:, cols[h]].astype(BF16)))
        nls, hilos = [], []
        for h in range(SB_H):
            nl, hilo = _sb_neg_log(zs[h], mask)
            nls.append(nl)
            hilos.append(hilo)
        tails = [_dot(hilos[h], tri) for h in range(SB_H)]
        ws = []
        for h in range(SB_H):
            ws.append(_sb_weight(zs[h], nls[h], tails[h], c_ref[h], mask))
            c_ref[h] += tails[h][:, 0:1] + nls[h][:, 0:1]
        for h in range(SB_H):
            acc_ref[h] += _dot(ws[h], v_ref[:, cols[h]].astype(BF16))

    whole = (jt + 1) * tk <= qs0

    @pl.when(whole)
    def _():
        tile(False)

    @pl.when(jnp.logical_not(whole))
    def _():
        tile(True)

    @pl.when(jt == 0)
    def _():
        lane_half = lax.broadcasted_iota(jnp.int32, (tq, LANES), 1) // HD
        for c in range(SB_H // 2):
            o_ref[:, c * LANES:(c + 1) * LANES] = jnp.where(lane_half == 0, acc_ref[2 * c], acc_ref[2 * c + 1])


def _sb_attend(q_arr, q_blk, k_arr, k_blk, v_arr, v_blk, *, q0):
    b, t = q_arr.shape[:2]
    l = k_arr.shape[1]
    tq = _pick(t, 256)
    tk = _pick(l, 256, LANES)
    pairs = [(i, jt) for i in range(t // tq) for jt in range((q0 + (i + 1) * tq - 2) // tk, -1, -1)]
    qi = jnp.asarray([pr[0] for pr in pairs], jnp.int32)
    ki = jnp.asarray([pr[1] for pr in pairs], jnp.int32)
    return pl.pallas_call(
        functools.partial(_sb_kernel, tq=tq, tk=tk, q0=q0),
        grid_spec=pltpu.PrefetchScalarGridSpec(
            num_scalar_prefetch=2, grid=(b, len(pairs)),
            in_specs=[
                pl.BlockSpec((None, tq, GW), lambda bi, p, qi_r, ki_r: (bi, qi_r[p], q_blk)),
                pl.BlockSpec((None, tk, GW), lambda bi, p, qi_r, ki_r: (bi, ki_r[p], k_blk)),
                pl.BlockSpec((None, tk, GW), lambda bi, p, qi_r, ki_r: (bi, ki_r[p], v_blk)),
            ],
            out_specs=pl.BlockSpec((None, tq, GW), lambda bi, p, qi_r, ki_r: (bi, qi_r[p], 0)),
            scratch_shapes=[pltpu.VMEM((SB_H, tq, LANES), F32), pltpu.VMEM((SB_H, tq, 1), F32)]),
        out_shape=jax.ShapeDtypeStruct((b, t, GW), F32),
        compiler_params=_cparams(("parallel", "arbitrary")),
        name="sb_attend",
    )(qi, ki, q_arr, k_arr, v_arr)


def _sb_paged_kernel(pt_ref, q_ref, new_ref, *refs, n_group, t):
    del pt_ref
    page_refs = refs[:n_group]
    o_ref, qbd_ref, acc_ref, c_ref = refs[n_group:]
    s = pl.program_id(1)
    m = SB_H * t
    tri = _suffix_matrix(PAGE)

    @pl.when(s == 0)
    def _():
        lane_head = lax.broadcasted_iota(jnp.int32, (t, GW), 1) // HD
        q = q_ref[...] * QSCALE2
        for h in range(SB_H):
            qbd_ref[h * t:(h + 1) * t, :] = jnp.where(lane_head == h, q, 0.0)
        pad = jnp.zeros((PAGE - t, GW), F32)
        k_new = jnp.concatenate([new_ref[:, 0:GW], pad], axis=0).astype(BF16)
        v_new = jnp.concatenate([new_ref[:, GW:2 * GW], pad], axis=0).astype(BF16)
        row = lax.broadcasted_iota(jnp.int32, (m, PAGE), 0)
        mask = lax.broadcasted_iota(jnp.int32, (m, PAGE), 1) < row - (row // t) * t
        w, rs = _sb_weights(_dot_nt(qbd_ref[...].astype(BF16), k_new), tri, 0.0, mask)
        acc_ref[...] = _dot(w, v_new)
        c_ref[...] = rs

    qb = qbd_ref[...].astype(BF16)
    pages = range(n_group)
    zs = [_dot(qb, page_refs[g][0].reshape(GW, PAGE).astype(BF16)) for g in pages]
    nls, hilos = zip(*[_sb_neg_log(zs[g], None) for g in pages])
    tails = [_dot(hilos[g], tri) for g in pages]
    carries = [None] * n_group
    carry = c_ref[...]
    for g in reversed(pages):
        carries[g] = carry
        carry = carry + tails[g][:, 0:1] + nls[g][:, 0:1]
    c_ref[...] = carry
    ws = [_sb_weight(zs[g], nls[g], tails[g], carries[g], None) for g in pages]
    acc = acc_ref[...]
    for g in pages:
        acc = acc + _dot_nt(ws[g], page_refs[g][1].reshape(GW, PAGE).astype(BF16))
    acc_ref[...] = acc

    @pl.when(s == pl.num_programs(1) - 1)
    def _():
        lane_head = lax.broadcasted_iota(jnp.int32, (t, GW), 1) // HD
        out = jnp.zeros((t, GW), F32)
        for h in range(SB_H):
            out = out + jnp.where(lane_head == h, acc_ref[h * t:(h + 1) * t, :], 0.0)
        o_ref[...] = out


def _sb_attend_paged(q_arr, q_blk, new_blk, cache_t, page_table, *, layer):
    b, t = q_arr.shape[:2]
    n_pages = page_table.shape[1]
    n_group = math.gcd(n_pages, PAGES_PER_STEP)
    n_steps = n_pages // n_group
    m = SB_H * t

    def page_map(g):
        return lambda bi, s, pt: (layer, pt[bi, n_pages - (s + 1) * n_group + g], 0, 0, 0, 0)

    in_specs = [
        pl.BlockSpec((None, t, GW), lambda bi, s, pt: (bi, 0, q_blk)),
        pl.BlockSpec((None, t, 2 * GW), lambda bi, s, pt: (bi, 0, new_blk)),
    ] + [pl.BlockSpec((None, None, 2, SB_H, HD, PAGE), page_map(g)) for g in range(n_group)]
    return pl.pallas_call(
        functools.partial(_sb_paged_kernel, n_group=n_group, t=t),
        grid_spec=pltpu.PrefetchScalarGridSpec(
            num_scalar_prefetch=1, grid=(b, n_steps), in_specs=in_specs,
            out_specs=pl.BlockSpec((None, t, GW), lambda bi, s, pt: (bi, 0, 0)),
            scratch_shapes=[pltpu.VMEM((m, GW), F32), pltpu.VMEM((m, GW), F32), pltpu.VMEM((m, 1), F32)]),
        out_shape=jax.ShapeDtypeStruct((b, t, GW), F32),
        compiler_params=_cparams(("parallel", "arbitrary")),
        name="sb_attend_paged",
    )(page_table, q_arr, q_arr, *([cache_t] * n_group))


def _masked_softmax(s2, mask):
    sm = jnp.where(mask, s2, NEG)
    m = jnp.max(sm, axis=-1, keepdims=True)
    e = jnp.where(mask, jnp.exp2(sm - m), 0.0)
    return e / jnp.maximum(jnp.sum(e, axis=-1, keepdims=True), 1e-30)


def _compress_kernel(xk_ref, xv_ref, w_ref, o_ref, *, pitch):
    width = 2 * LANES
    n = o_ref.shape[0]
    parts = []
    for j in range(CMP_BLOCK):
        rows = pl.ds(j, n, stride=pitch)
        xj = jnp.concatenate([xk_ref[rows, :], xv_ref[rows, :]], axis=1).astype(BF16)
        parts.append(_dot(xj, w_ref[j * width:(j + 1) * width, :]))
    while len(parts) > 1:
        parts = [parts[i] + parts[i + 1] for i in range(0, len(parts), 2)]
    o_ref[...] = parts[0]


def _nsa_compress(x_arr, x_blk, phi_big, *, layer):
    b, nb, pitch = x_arr.shape[:3]
    width = 2 * LANES
    tn = _pick(nb, 160)
    return pl.pallas_call(
        functools.partial(_compress_kernel, pitch=pitch),
        grid=(b, nb // tn),
        in_specs=[
            pl.BlockSpec((None, tn * pitch, LANES), lambda bi, i: (bi, i, 2 * x_blk)),
            pl.BlockSpec((None, tn * pitch, LANES), lambda bi, i: (bi, i, 2 * x_blk + 1)),
            pl.BlockSpec((None, CMP_BLOCK * width, width), lambda bi, i: (layer, 0, 0)),
        ],
        out_specs=pl.BlockSpec((None, tn, width), lambda bi, i: (bi, i, 0)),
        out_shape=jax.ShapeDtypeStruct((b, nb, width), F32),
        compiler_params=_cparams(("parallel", "parallel")),
        name="nsa_compress",
    )(*([x_arr.reshape(b, nb * pitch, x_arr.shape[3])] * 2), phi_big)


def _select_blocks(imp, blk, nb, n_sel):
    tq, nbp = imp.shape
    if tq % LANES != 0:
        rank = jnp.zeros((tq, nbp), F32)
        for mblk in range(nb):
            col = imp[:, mblk:mblk + 1]
            beats = (col > imp) | ((col == imp) & (blk > mblk))
            rank = rank + jnp.where(beats, 1.0, 0.0)
        return jnp.where(rank < n_sel, 1.0, 0.0)
    imp_t = imp.T
    n_parts = -(-nb // SUBLANES)
    parts = [imp_t[p * SUBLANES:(p + 1) * SUBLANES, :] for p in range(n_parts)]
    row_in_part = lax.broadcasted_iota(jnp.int32, (SUBLANES, tq), 0)
    ranks = [jnp.zeros((SUBLANES, tq), F32) for _ in range(n_parts)]
    for mblk in range(nb):
        pm, rm = divmod(mblk, SUBLANES)
        row = parts[pm][rm:rm + 1, :]
        for p in range(n_parts):
            if p < pm:
                beats = row > parts[p]
            elif p > pm:
                beats = row >= parts[p]
            else:
                beats = (row > parts[p]) | ((row == parts[p]) & (row_in_part > rm))
            ranks[p] = ranks[p] + jnp.where(beats, 1.0, 0.0)
    sel_t = [jnp.where(r < n_sel, 1.0, 0.0) for r in ranks]
    if nbp > n_parts * SUBLANES:
        sel_t.append(jnp.zeros((nbp - n_parts * SUBLANES, tq), F32))
    return jnp.concatenate(sel_t, axis=0).T


def _nsa_kernel(*refs, n_page_refs, tq, tk, q0, lw, nb, nbp, n_sel, wl):
    n_prefetch = 3 if n_page_refs else 2
    qi_ref, ki_ref = refs[0], refs[1]
    q_ref, sm_ref, kc_ref = refs[n_prefetch:n_prefetch + 3]
    key_refs = refs[n_prefetch + 3:n_prefetch + 4 + n_page_refs]
    win_ref, o_ref, qs_ref, sel_ref, part_ref, m_ref, l_ref, acc_ref = refs[n_prefetch + 4 + n_page_refs:]
    p = pl.program_id(1)
    i = qi_ref[p]
    j = ki_ref[p]
    rq = NSA_R * tq
    qs0 = q0 + i * tq
    j_last = (qs0 + tq - 1) // tk
    scale = QSCALE2

    @pl.when(j == 0)
    def _():
        lane_half = lax.broadcasted_iota(jnp.int32, (tq, LANES), 1) // HD
        gates = _sigmoid(sm_ref[...])
        qpos_b = qs0 + lax.broadcasted_iota(jnp.int32, (tq, nbp), 0)
        blk = lax.broadcasted_iota(jnp.int32, (tq, nbp), 1)
        cmask = ((blk + 1) * CMP_BLOCK - 1 <= qpos_b) & (blk < nb)
        cur = qpos_b // CMP_BLOCK
        forced = (blk == 0) | (blk == cur)
        valid = blk <= cur
        start = pl.multiple_of(jnp.maximum(lw + (i + 1) * tq - wl, 0), SUBLANES)
        qpos_w = qs0 + lax.broadcasted_iota(jnp.int32, (tq, wl), 0)
        kpos_w = q0 - lw + start + lax.broadcasted_iota(jnp.int32, (tq, wl), 1)
        wmask = (kpos_w <= qpos_w) & (kpos_w > qpos_w - WINDOW)
        kw = win_ref[pl.ds(start, wl), 0:LANES].astype(BF16)
        vw = win_ref[pl.ds(start, wl), LANES:2 * LANES].astype(BF16)
        kc = kc_ref[:, 0:LANES].astype(BF16)
        vc = kc_ref[:, LANES:2 * LANES].astype(BF16)
        groups = range(NSA_G)
        for g in groups:
            for r in range(NSA_R):
                chunk, half = 2 * g + r // 2, r % 2
                qc = q_ref[:, chunk * LANES:(chunk + 1) * LANES]
                if half != g:
                    qc = pltpu.roll(qc, HD, 1)
                qs_ref[g, r * tq:(r + 1) * tq, :] = jnp.where(lane_half == g, qc * scale, 0.0)
        qs = [qs_ref[g].astype(BF16) for g in groups]
        s_c = [_dot_nt(qs[g], kc).reshape(NSA_R, tq, nbp) for g in groups]
        s_w = [_dot_nt(qs[g], kw).reshape(NSA_R, tq, wl) for g in groups]
        p_c = [_masked_softmax(s_c[g], cmask[None]) for g in groups]
        p_w = [_masked_softmax(s_w[g], wmask[None]) for g in groups]
        o_c = [_dot(p_c[g].reshape(rq, nbp).astype(BF16), vc) for g in groups]
        o_w = [_dot(p_w[g].reshape(rq, wl).astype(BF16), vw) for g in groups]
        for g in groups:
            imp = jnp.sum(p_c[g], axis=0)
            imp = jnp.where(forced, SEL_FORCE, jnp.where(valid, imp, -1.0))
            imp = jnp.where(blk < nb, imp, -2.0)
            sel_ref[g] = _select_blocks(imp, blk, nb, n_sel)
            for r in range(NSA_R):
                rows = slice(r * tq, (r + 1) * tq)
                lane_c = SM_GATE + 0 * NSA_G * NSA_R + g * NSA_R + r
                lane_w = SM_GATE + 2 * NSA_G * NSA_R + g * NSA_R + r
                part_ref[g, rows, :] = (gates[:, lane_c:lane_c + 1] * o_c[g][rows]
                                        + gates[:, lane_w:lane_w + 1] * o_w[g][rows])
        m_ref[...] = jnp.full_like(m_ref, NEG)
        l_ref[...] = jnp.zeros_like(l_ref)
        acc_ref[...] = jnp.zeros_like(acc_ref)

    slab = tq if tq >= LANES else rq
    pieces = [(g, slice(r0, r0 + slab)) for g in range(NSA_G) for r0 in range(0, rq, slab)]

    def sel_tile(n, kpos0, score_fn, pv_fn):
        kblk = (kpos0 + lax.broadcasted_iota(jnp.int32, (nbp, n), 1)) // CMP_BLOCK
        expand = (lax.broadcasted_iota(jnp.int32, (nbp, n), 0) == kblk).astype(BF16)
        qpos = qs0 + lax.broadcasted_iota(jnp.int32, (tq, n), 0)
        kpos = kpos0 + lax.broadcasted_iota(jnp.int32, (tq, n), 1)
        causal = kpos <= qpos
        biases = []
        for g in range(NSA_G):
            bias = jnp.where((_dot(sel_ref[g].astype(BF16), expand) > 0.5) & causal, 0.0, NEG)
            biases.append(bias if slab == tq else jnp.concatenate([bias] * (slab // tq), axis=0))
        scores = [score_fn(qs_ref[g, rows, :].astype(BF16)) + biases[g] for g, rows in pieces]
        probs, corrs = [], []
        for (g, rows), s in zip(pieces, scores):
            m_old = m_ref[g, rows, :]
            m_new = jnp.maximum(m_old, jnp.max(s, axis=-1, keepdims=True))
            pr = jnp.exp2(s - m_new)
            corr = jnp.exp2(m_old - m_new)
            l_ref[g, rows, :] = corr * l_ref[g, rows, :] + jnp.sum(pr, axis=-1, keepdims=True)
            m_ref[g, rows, :] = m_new
            probs.append(pr.astype(BF16))
            corrs.append(corr)
        for (g, rows), pr, corr in zip(pieces, probs, corrs):
            acc_ref[g, rows, :] = corr * acc_ref[g, rows, :] + pv_fn(pr)

    if n_page_refs == 0:
        ks = key_refs[0][:, 0:LANES].astype(BF16)
        vs = key_refs[0][:, LANES:2 * LANES].astype(BF16)
        sel_tile(tk, j * tk, lambda q: _dot_nt(q, ks), lambda pr: _dot(pr, vs))
    else:
        n_past_tiles = q0 // tk

        @pl.when(j < n_past_tiles)
        def _():
            kst = jnp.concatenate([r[0].reshape(NSA_G * HD, PAGE) for r in key_refs[:-1]], axis=1).astype(BF16)
            vst = jnp.concatenate([r[1].reshape(NSA_G * HD, PAGE) for r in key_refs[:-1]], axis=1).astype(BF16)
            sel_tile(tk, j * tk, lambda q: _dot(q, kst), lambda pr: _dot_nt(pr, vst))

        @pl.when(j == n_past_tiles)
        def _():
            pad = jnp.zeros((PAGE - tq, LANES), F32)
            ks = jnp.concatenate([key_refs[-1][:, 0:LANES], pad], axis=0).astype(BF16)
            vs = jnp.concatenate([key_refs[-1][:, LANES:2 * LANES], pad], axis=0).astype(BF16)
            sel_tile(PAGE, q0, lambda q: _dot_nt(q, ks), lambda pr: _dot(pr, vs))

    @pl.when(j == j_last)
    def _():
        gates = _sigmoid(sm_ref[...])
        lane_half = lax.broadcasted_iota(jnp.int32, (tq, LANES), 1) // HD
        for g in range(NSA_G):
            o_s = acc_ref[g] / jnp.maximum(l_ref[g], 1e-30)
            res = []
            for r in range(NSA_R):
                rows = slice(r * tq, (r + 1) * tq)
                lane_s = SM_GATE + 1 * NSA_G * NSA_R + g * NSA_R + r
                res.append(part_ref[g, rows, :] + gates[:, lane_s:lane_s + 1] * o_s[rows])
            for c2 in range(2):
                even, odd = res[2 * c2], res[2 * c2 + 1]
                if g == 1:
                    even = pltpu.roll(even, HD, 1)
                else:
                    odd = pltpu.roll(odd, HD, 1)
                chunk = 2 * g + c2
                o_ref[:, chunk * LANES:(chunk + 1) * LANES] = jnp.where(lane_half == 0, even, odd)


def _nsa_attend(q_arr, q_blk, sm_arr, sm_blk, kc_arr, ks_arr, ks_blk, win_arr, *, q0, lw, n_real_keys,
                paged=None):
    b, t = q_arr.shape[:2]
    nbp = kc_arr.shape[1]
    nb = -(-n_real_keys // CMP_BLOCK)
    n_sel = min(SEL_TOPN, nb)
    tq = _pick(t, 128)
    wl = min(WINDOW + tq, lw + t)
    rq = NSA_R * tq
    if paged is None:
        tk = _pick(ks_arr.shape[1], 1024, LANES)
        n_page_refs, prefetch = 0, ()
        key_specs = [pl.BlockSpec((None, tk, 2 * LANES), lambda bi, p, qi_r, ki_r: (bi, ki_r[p], ks_blk))]
        key_args = (ks_arr,)
    else:
        cache_t, page_table, layer = paged
        n_pages = page_table.shape[1]
        n_page_refs = math.gcd(n_pages, PAGES_PER_STEP)
        tk = n_page_refs * PAGE
        assert tq == t and q0 == n_pages * PAGE and q0 % tk == 0 and t <= PAGE
        prefetch = (page_table,)

        def page_map(g):
            def index_map(bi, p, qi_r, ki_r, pt):
                return (layer, pt[bi, jnp.minimum(ki_r[p] * n_page_refs + g, n_pages - 1)], 1, 0, 0, 0)
            return index_map

        key_specs = [pl.BlockSpec((None, None, 2, NSA_G, HD, PAGE), page_map(g)) for g in range(n_page_refs)]
        key_specs.append(pl.BlockSpec((None, t, 2 * LANES), lambda bi, p, qi_r, ki_r, pt: (bi, 0, ks_blk)))
        key_args = (cache_t,) * n_page_refs + (ks_arr,)
    pairs = [(i, j) for i in range(t // tq) for j in range((q0 + (i + 1) * tq - 1) // tk + 1)]
    qi = jnp.asarray([pr[0] for pr in pairs], jnp.int32)
    ki = jnp.asarray([pr[1] for pr in pairs], jnp.int32)
    return pl.pallas_call(
        functools.partial(_nsa_kernel, n_page_refs=n_page_refs, tq=tq, tk=tk, q0=q0, lw=lw, nb=nb, nbp=nbp,
                          n_sel=n_sel, wl=wl),
        grid_spec=pltpu.PrefetchScalarGridSpec(
            num_scalar_prefetch=2 + len(prefetch), grid=(b, len(pairs)),
            in_specs=[
                pl.BlockSpec((None, tq, GW), lambda bi, p, qi_r, ki_r, *_: (bi, qi_r[p], q_blk)),
                pl.BlockSpec((None, tq, LANES), lambda bi, p, qi_r, ki_r, *_: (bi, qi_r[p], sm_blk)),
                pl.BlockSpec((None, nbp, 2 * LANES), lambda bi, p, qi_r, ki_r, *_: (bi, 0, 0)),
                *key_specs,
                pl.BlockSpec((None, lw + t, 2 * LANES), lambda bi, p, qi_r, ki_r, *_: (bi, 0, 0)),
            ],
            out_specs=pl.BlockSpec((None, tq, GW), lambda bi, p, qi_r, ki_r, *_: (bi, qi_r[p], 0)),
            scratch_shapes=[
                pltpu.VMEM((NSA_G, rq, LANES), F32),
                pltpu.VMEM((NSA_G, tq, nbp), F32),
                pltpu.VMEM((NSA_G, rq, LANES), F32),
                pltpu.VMEM((NSA_G, rq, 1), F32),
                pltpu.VMEM((NSA_G, rq, 1), F32),
                pltpu.VMEM((NSA_G, rq, LANES), F32),
            ]),
        out_shape=jax.ShapeDtypeStruct((b, t, GW), F32),
        compiler_params=_cparams(("parallel", "arbitrary")),
        name="nsa_attend",
    )(qi, ki, *prefetch, q_arr, sm_arr, kc_arr, *key_args, win_arr)


def _lru_kernel(x_ref, gt_ref, buf_ref, h0_ref, cw_ref, cb_ref, wbd_ref, gb_ref, lam_ref,
                o_ref, hl_ref, nb_ref, xp_ref, a_ref, b_ref, h_ref, *, tt):
    s = pl.program_id(1)
    hist = CONV_PAD - (CONV_W - 1)

    @pl.when(s == 0)
    def _():
        xp_ref[hist:CONV_PAD, :] = buf_ref[...]
        h_ref[...] = h0_ref[...]

    xp_ref[CONV_PAD:CONV_PAD + tt, :] = x_ref[...]
    u = cb_ref[...]
    for jw in range(CONV_W):
        u = u + cw_ref[jw:jw + 1, :] * xp_ref[hist + jw:hist + jw + tt, :]
    gts = _dot(u.astype(BF16), wbd_ref[...]) + gb_ref[...]
    r = _sigmoid(gts[:, :GW])
    ig = _sigmoid(gts[:, GW:])
    log_a = -LRU_C * r * _softplus(-lam_ref[...])
    a = jnp.exp(log_a)
    a_ref[...] = a
    b_ref[...] = jnp.sqrt(-jnp.tanh(log_a) * (a * a + 1.0)) * (ig * u)

    def step(t, h):
        h = a_ref[pl.ds(t, 1), :] * h + b_ref[pl.ds(t, 1), :]
        b_ref[pl.ds(t, 1), :] = h
        return h

    h = lax.fori_loop(0, tt, step, h_ref[...], unroll=8)
    h_ref[...] = h
    o_ref[...] = b_ref[...] * _gelu_tanh(gt_ref[...])
    tail = xp_ref[tt + hist:tt + CONV_PAD, :]
    xp_ref[hist:CONV_PAD, :] = tail

    @pl.when(s == pl.num_programs(1) - 1)
    def _():
        hl_ref[...] = h
        nb_ref[...] = tail


def _rg_lru(h_arr, buf, h0, conv_w, conv_b, wbd, gate_b, lam, *, layer):
    b, t = h_arr.shape[:2]
    tt = _pick(t, 512)
    x_blk, g_blk = C_LRUX // GW, C_LRUG // GW
    wspec = lambda shape: pl.BlockSpec((None,) + shape, lambda bi, s: (layer,) + (0,) * len(shape))
    return pl.pallas_call(
        functools.partial(_lru_kernel, tt=tt),
        grid=(b, t // tt),
        in_specs=[
            pl.BlockSpec((None, tt, GW), lambda bi, s: (bi, s, x_blk)),
            pl.BlockSpec((None, tt, GW), lambda bi, s: (bi, s, g_blk)),
            pl.BlockSpec((None, CONV_W - 1, GW), lambda bi, s: (bi, 0, 0)),
            pl.BlockSpec((None, 1, GW), lambda bi, s: (bi, 0, 0)),
            wspec((CONV_W, GW)), wspec((1, GW)), wspec((GW, 2 * GW)), wspec((1, 2 * GW)), wspec((1, GW)),
        ],
        out_specs=[
            pl.BlockSpec((None, tt, GW), lambda bi, s: (bi, s, 0)),
            pl.BlockSpec((None, 1, GW), lambda bi, s: (bi, 0, 0)),
            pl.BlockSpec((None, CONV_W - 1, GW), lambda bi, s: (bi, 0, 0)),
        ],
        out_shape=[
            jax.ShapeDtypeStruct((b, t, GW), F32),
            jax.ShapeDtypeStruct((b, 1, GW), F32),
            jax.ShapeDtypeStruct((b, CONV_W - 1, GW), F32),
        ],
        scratch_shapes=[
            pltpu.VMEM((CONV_PAD + tt, GW), F32),
            pltpu.VMEM((tt, GW), F32),
            pltpu.VMEM((tt, GW), F32),
            pltpu.VMEM((1, GW), F32),
        ],
        compiler_params=_cparams(("parallel", "arbitrary")),
        name="rg_lru",
    )(h_arr, h_arr, buf, h0, conv_w, conv_b, wbd, gate_b, lam)


def _gdn_kernel(qkv_ref, z_ref, sm_ref, buf_ref, s0_ref, cw_ref, al_ref, dtb_ref, ng_ref,
                o_ref, sout_ref, nb_ref, xp_ref, smp_ref, st_ref, *, tt, tc):
    s = pl.program_id(1)
    hist = CONV_PAD - (CONV_W - 1)
    n_chunks = tc // GDN_CHUNK

    @pl.when(s == 0)
    def _():
        xp_ref[...] = jnp.zeros_like(xp_ref)
        smp_ref[...] = jnp.zeros_like(smp_ref)
        xp_ref[hist:CONV_PAD, :] = buf_ref[...]
        st_ref[...] = s0_ref[...]

    xp_ref[CONV_PAD:CONV_PAD + tt, :] = qkv_ref[...]
    smp_ref[0:tt, :] = sm_ref[...]
    row_ok = lax.broadcasted_iota(jnp.int32, (tc, 1), 0) < tt
    y = cw_ref[0:1, :] * xp_ref[hist:hist + tc, :]
    for jw in range(1, CONV_W):
        y = y + cw_ref[jw:jw + 1, :] * xp_ref[hist + jw:hist + jw + tc, :]
    y = jnp.where(row_ok, _silu(y), 0.0)
    sm = smp_ref[...]
    beta_all = jnp.where(row_ok, _sigmoid(sm), 0.0)
    g_all = jnp.where(row_ok, -jnp.exp(al_ref[...]) * _softplus(sm + dtb_ref[...]), 0.0)

    ri = lax.broadcasted_iota(jnp.int32, (tc, tc), 0)
    ci = lax.broadcasted_iota(jnp.int32, (tc, tc), 1)
    same = (ri // GDN_CHUNK) == (ci // GDN_CHUNK)
    incl = same & (ci <= ri)
    strict = same & (ci < ri)
    ones_where = lambda m: jnp.where(m, 1.0, 0.0).astype(BF16)
    gcum = _dot_hp_exact_lhs(ones_where(incl), g_all)
    gcum_t = _dot_hp_exact_rhs(g_all.T, ones_where(same & (ri <= ci)))
    gtot = _dot_hp_exact_lhs(ones_where(same), g_all)
    eye = jnp.where(ri == ci, 1.0, 0.0)

    heads = range(GDN_H)
    pws, tinvs, qks, rhss, qgs, kds, gts = [], [], [], [], [], [], []
    for h in heads:
        q = y[:, h * GDN_D:(h + 1) * GDN_D]
        k = y[:, GW + h * GDN_D:GW + (h + 1) * GDN_D]
        v = y[:, 2 * GW + h * GDN_D:2 * GW + (h + 1) * GDN_D]
        q = q * lax.rsqrt(jnp.sum(q * q, axis=-1, keepdims=True) + RMS_EPS) * (GDN_D ** -0.5)
        k = k * lax.rsqrt(jnp.sum(k * k, axis=-1, keepdims=True) + RMS_EPS)
        beta = beta_all[:, SM_BETA + h:SM_BETA + h + 1]
        gc = gcum[:, SM_DECAY + h:SM_DECAY + h + 1]
        gr = gcum_t[SM_DECAY + h:SM_DECAY + h + 1, :]
        gt = gtot[:, SM_DECAY + h:SM_DECAY + h + 1]
        decay = jnp.exp(jnp.where(incl, gc - gr, NEG))
        kb = k * beta
        kbf = k.astype(BF16)
        a_mat = jnp.where(strict, _dot_nt(kb.astype(BF16), kbf) * decay, 0.0)
        qks.append(jnp.where(incl, _dot_nt(q.astype(BF16), kbf) * decay, 0.0).astype(BF16))
        pws.append(-a_mat)
        tinvs.append(eye - a_mat)
        eg = jnp.exp(gc)
        rhss.append(jnp.concatenate([v * beta, kb * eg], axis=1))
        qgs.append((q * eg).astype(BF16))
        kds.append((k * jnp.exp(gt - gc)).astype(BF16))
        gts.append(gt)
    for _ in range(int(math.log2(GDN_CHUNK)) - 1):
        pws = [_dot_hp(pws[h], pws[h]) for h in heads]
        tinvs = [tinvs[h] + _dot_hp(tinvs[h], pws[h]) for h in heads]
    sols = [_dot_hp(tinvs[h], rhss[h]) for h in heads]
    us = [sols[h][:, :GDN_D] for h in heads]
    ws = [sols[h][:, GDN_D:].astype(BF16) for h in heads]
    sts = [st_ref[h] for h in heads]
    outs = [[] for _ in heads]
    for c in range(n_chunks):
        rows = slice(c * GDN_CHUNK, (c + 1) * GDN_CHUNK)
        stbs = [sts[h].astype(BF16) for h in heads]
        v_news = [(us[h][rows] - _dot(ws[h][rows], stbs[h])).astype(BF16) for h in heads]
        for h in heads:
            outs[h].append(_dot(qgs[h][rows], stbs[h]) + _dot(qks[h][rows, rows], v_news[h]))
        for h in heads:
            g_last = jnp.exp(gts[h][c * GDN_CHUNK:c * GDN_CHUNK + 1, :])
            sts[h] = sts[h] * g_last + _dot_tn(kds[h][rows], v_news[h])
    for h in heads:
        st_ref[h] = sts[h]
        o = outs[h][0] if n_chunks == 1 else jnp.concatenate(outs[h], axis=0)
        o = o * lax.rsqrt(jnp.mean(o * o, axis=-1, keepdims=True) + RMS_EPS) * ng_ref[...]
        o_ref[:, h * GDN_D:(h + 1) * GDN_D] = o[0:tt] * _silu(z_ref[:, h * GDN_D:(h + 1) * GDN_D])

    tail = xp_ref[tt + hist:tt + CONV_PAD, :]
    xp_ref[hist:CONV_PAD, :] = tail

    @pl.when(s == pl.num_programs(1) - 1)
    def _():
        sout_ref[...] = st_ref[...]
        nb_ref[...] = tail


def _gated_deltanet(h_arr, buf, s0, conv_w, a_log_row, dt_bias_row, norm_g, *, layer):
    b, t = h_arr.shape[:2]
    tt = _pick(t, 128)
    tc = -(-tt // GDN_CHUNK) * GDN_CHUNK
    assert tc == tt or t == tt, "sequence length must be a multiple of the chunk unless it fits one block"
    w3 = 3 * GW
    wspec = lambda shape: pl.BlockSpec((None,) + shape, lambda bi, s: (layer,) + (0,) * len(shape))
    return pl.pallas_call(
        functools.partial(_gdn_kernel, tt=tt, tc=tc),
        grid=(b, t // tt),
        in_specs=[
            pl.BlockSpec((None, tt, w3), lambda bi, s: (bi, s, C_GQKV // w3)),
            pl.BlockSpec((None, tt, GW), lambda bi, s: (bi, s, C_GZ // GW)),
            pl.BlockSpec((None, tt, LANES), lambda bi, s: (bi, s, C_SMALL // LANES)),
            pl.BlockSpec((None, CONV_W - 1, w3), lambda bi, s: (bi, 0, 0)),
            pl.BlockSpec((None, GDN_H, GDN_D, GDN_D), lambda bi, s: (bi, 0, 0, 0)),
            wspec((CONV_W, w3)), wspec((1, LANES)), wspec((1, LANES)), wspec((1, GDN_D)),
        ],
        out_specs=[
            pl.BlockSpec((None, tt, GW), lambda bi, s: (bi, s, 0)),
            pl.BlockSpec((None, GDN_H, GDN_D, GDN_D), lambda bi, s: (bi, 0, 0, 0)),
            pl.BlockSpec((None, CONV_W - 1, w3), lambda bi, s: (bi, 0, 0)),
        ],
        out_shape=[
            jax.ShapeDtypeStruct((b, t, GW), F32),
            jax.ShapeDtypeStruct((b, GDN_H, GDN_D, GDN_D), F32),
            jax.ShapeDtypeStruct((b, CONV_W - 1, w3), F32),
        ],
        scratch_shapes=[
            pltpu.VMEM((CONV_PAD + tc, w3), F32),
            pltpu.VMEM((tc, LANES), F32),
            pltpu.VMEM((GDN_H, GDN_D, GDN_D), F32),
        ],
        compiler_params=_cparams(("parallel", "arbitrary")),
        name="gated_deltanet",
    )(h_arr, h_arr, h_arr, buf, s0, conv_w, a_log_row, dt_bias_row, norm_g)


def _outproj_kernel(oa_ref, ob_ref, oc_ref, od_ref, x_ref, w_ref, gg_ref, g_ref, b_ref, o_ref, *, alpha):
    def rms(v, gain):
        return (v * lax.rsqrt(jnp.mean(v * v, axis=-1, keepdims=True) + RMS_EPS) * gain).astype(BF16)

    acc = _dot(rms(oa_ref[...], gg_ref[0:1, :]), w_ref[0:GW, :])
    acc += _dot(rms(ob_ref[...], gg_ref[1:2, :]), w_ref[GW:2 * GW, :])
    acc += _dot(rms(oc_ref[...], gg_ref[2:3, :]), w_ref[2 * GW:3 * GW, :])
    acc += _dot(od_ref[...].astype(BF16), w_ref[3 * GW:4 * GW, :])
    o_ref[...] = _layer_norm(alpha * x_ref[...] + acc, g_ref[...], b_ref[...])


def _outproj_ln(oa, ob, oc, od, x, w_out, grp_g, ln_g, ln_b, *, layer, alpha):
    n, d = x.shape
    tm = _pick(n, 512)
    mix_spec = pl.BlockSpec((tm, GW), lambda i: (i, 0))
    return pl.pallas_call(
        functools.partial(_outproj_kernel, alpha=alpha),
        grid=(n // tm,),
        in_specs=[
            mix_spec, mix_spec, mix_spec, mix_spec,
            pl.BlockSpec((tm, d), lambda i: (i, 0)),
            pl.BlockSpec((None, 4 * GW, d), lambda i: (layer, 0, 0)),
            pl.BlockSpec((None, 3, GW), lambda i: (layer, 0, 0)),
            pl.BlockSpec((None, None, 1, d), lambda i: (layer, 1, 0, 0)),
            pl.BlockSpec((None, None, 1, d), lambda i: (layer, 1, 0, 0)),
        ],
        out_specs=pl.BlockSpec((tm, d), lambda i: (i, 0)),
        out_shape=jax.ShapeDtypeStruct((n, d), F32),
        compiler_params=_cparams(("parallel",)),
        name="outproj_ln",
    )(oa, ob, oc, od, x, w_out, grp_g, ln_g, ln_b)


def _prep_weights(ffn_gu, ffn_down, w_in, w_out, nsa_phi, lru_gate_w, lru_gate_b, gdn_A_log, gdn_dt_bias):
    depth, d_model, _ = w_in.shape
    o_nsakv, o_gate, o_lrux, o_gqkv, o_gz, o_gb = 2048, 2816, 2840, 3864, 5400, 5912
    zeros = jnp.zeros((depth, d_model, P_IN_PAD - 5920), w_in.dtype)
    w_in_p = jnp.concatenate([
        w_in[..., 512:1536],
        w_in[..., 0:512],
        w_in[..., 1536:2048],
        w_in[..., o_nsakv:o_nsakv + 512],
        w_in[..., o_lrux:o_lrux + 1024],
        w_in[..., o_gz:o_gz + 512],
        w_in[..., o_nsakv + 512:o_gate],
        w_in[..., o_gate:o_lrux],
        w_in[..., o_gb:o_gb + 8],
        zeros,
        w_in[..., o_gqkv:o_gz],
    ], axis=-1).astype(BF16)
    phi_b = jnp.transpose(nsa_phi, (0, 2, 1, 3, 4)).astype(BF16)
    col_blocks = []
    for c in range(2):
        for g in range(NSA_G):
            hot = jnp.zeros((2, NSA_G, 1, 1), BF16).at[c, g].set(1)
            col_blocks.append(phi_b[:, :, :, None, :, :] * hot[None, None])
    phi_big = jnp.concatenate(col_blocks, axis=-1).reshape(depth, CMP_BLOCK * 2 * NSA_G * HD, 2 * NSA_G * HD)
    eye_b = jnp.eye(LRU_BLKS, dtype=F32)
    wbd = jnp.einsum("lknce,nm->lnckme", lru_gate_w, eye_b).reshape(depth, GW, 2 * GW).astype(BF16)
    gate_b = lru_gate_b.reshape(depth, 1, 2 * GW)
    pad_l = jnp.zeros((depth, SM_DECAY), F32)
    pad_r = jnp.zeros((depth, LANES - SM_DECAY - GDN_H), F32)
    a_log_row = jnp.concatenate([pad_l, gdn_A_log, pad_r], axis=1)[:, None, :]
    dt_bias_row = jnp.concatenate([pad_l, gdn_dt_bias, pad_r], axis=1)[:, None, :]
    return dict(
        wgu=ffn_gu, wd=ffn_down, w_in=w_in_p, w_out=w_out.astype(BF16),
        phi_big=phi_big, wbd=wbd, gate_b=gate_b, a_log_row=a_log_row, dt_bias_row=dt_bias_row)


def _trunk_layer(x, b, t, q0, past, wts, layer, alpha, ffn_bf16=None):
    ln_g, ln_b = wts["ln_g"], wts["ln_b"]
    emit = ffn_bf16 is None
    ffn_w = [(wts["wgu"], wts["wd"])] * 2 if emit else ffn_bf16
    x = _ffn_ln(x, ffn_w[0], ln_g, ln_b, layer=layer, which=0, ln_idx=0, alpha=alpha, emit_bf16=emit)
    if emit:
        x, emitted0 = x
    h = _matmul(x, wts["w_in"], layer=layer, name="in_proj")
    h3 = h.reshape(b, t, P_IN_PAD)

    sb_rows = h3[:, :, C_SBK:C_SBK + 2 * GW]
    nsa_rows = h3[:, :, C_NSAKV:C_NSAKV + GW]
    nsa_win_new = h3[:, :, C_NSAW:C_NSAW + 2 * LANES]

    if past["page_table"] is None:
        lw = 0
        o_a = _sb_attend(h3, C_SBQ // GW, h3, C_SBK // GW, h3, C_SBV // GW, q0=q0)
        cmp_rows, cmp_blk = h3.reshape(b, t // CMP_BLOCK, CMP_BLOCK, P_IN_PAD), C_NSAKV // (2 * LANES)
        win_all = nsa_win_new
        paged = None
    else:
        pt = past["page_table"]
        lw = past["nsa_win"].shape[2]
        o_a = _sb_attend_paged(h3, C_SBQ // GW, C_SBK // (2 * GW), past["sb_kv"], pt, layer=layer)
        cmp_rows = _gather_nsa_pages(past["nsa_kv"], pt, h3, C_NSAKV // (2 * LANES), layer=layer)
        cmp_blk = 0
        win_all = jnp.concatenate([past["nsa_win"][layer], nsa_win_new], axis=1)
        paged = (past["nsa_kv"], pt, layer)
    ks_blk = (C_NSAKV + 2 * LANES) // (2 * LANES)

    kc = _nsa_compress(cmp_rows, cmp_blk, wts["phi_big"], layer=layer)
    nb_arr = kc.shape[1]
    nbp = -(-nb_arr // LANES) * LANES
    kc = jnp.pad(kc, ((0, 0), (0, nbp - nb_arr), (0, 0)))
    o_b = _nsa_attend(h3, C_NSAQ // GW, h3, C_SMALL // LANES, kc, h3, ks_blk, win_all,
                      q0=q0, lw=lw, n_real_keys=q0 + t, paged=paged)
    new_win = win_all[:, -min(WINDOW, lw + t):]

    o_c, h_last, new_lru_buf = _rg_lru(h3, past["lru_conv"], past["lru_h"], wts["lru_conv_w"], wts["lru_conv_b"],
                                       wts["wbd"], wts["gate_b"], wts["lru_lambda"], layer=layer)
    o_d, s_new, new_gdn_buf = _gated_deltanet(h3, past["gdn_conv"], past["gdn_S"], wts["gdn_conv_w"],
                                              wts["a_log_row"], wts["dt_bias_row"], wts["gdn_norm_g"], layer=layer)

    n = b * t
    x = _outproj_ln(o_a.reshape(n, GW), o_b.reshape(n, GW), o_c.reshape(n, GW), o_d.reshape(n, GW), x,
                    wts["w_out"], wts["grp_norm_g"], ln_g, ln_b, layer=layer, alpha=alpha)
    x = _ffn_ln(x, ffn_w[1], ln_g, ln_b, layer=layer, which=1, ln_idx=2, alpha=alpha, emit_bf16=emit)
    if emit:
        x, emitted1 = x
    new_state = (
        sb_rows.reshape(b, t, 2, SB_H, HD),
        nsa_rows.reshape(b, t, 4, NSA_G, HD),
        new_win.reshape(b, new_win.shape[1], 2, NSA_G, HD),
        h_last.reshape(b, GW),
        new_lru_buf,
        s_new,
        new_gdn_buf,
    )
    return x, new_state, ([emitted0, emitted1] if emit else None)


def kernel(x_prompt, x_sample, cache_sb_kv, cache_nsa_kv, cache_nsa_win, state_lru_h, state_lru_conv,
           state_gdn_S, state_gdn_conv, page_table, ln_g, ln_b, ffn_gu, ffn_down, w_in, w_out, grp_norm_g,
           nsa_phi, lru_conv_w, lru_conv_b, lru_gate_w, lru_gate_b, lru_lambda, gdn_conv_w, gdn_A_log,
           gdn_dt_bias, gdn_norm_g):
    depth, d_model = w_in.shape[0], w_in.shape[1]
    alpha = (2 * depth) ** 0.25
    n_b, seq = x_prompt.shape[:2]
    n_db, dec_seq = x_sample.shape[:2]
    past_len = page_table.shape[1] * PAGE
    n_pool = cache_sb_kv.shape[1]

    wts = _prep_weights(ffn_gu, ffn_down, w_in, w_out, nsa_phi, lru_gate_w, lru_gate_b, gdn_A_log, gdn_dt_bias)
    wts.update(
        ln_g=ln_g.reshape(depth, 3, 1, d_model), ln_b=ln_b.reshape(depth, 3, 1, d_model),
        grp_norm_g=grp_norm_g, lru_conv_w=lru_conv_w, lru_conv_b=lru_conv_b.reshape(depth, 1, GW),
        lru_lambda=lru_lambda.reshape(depth, 1, GW), gdn_conv_w=gdn_conv_w,
        gdn_norm_g=gdn_norm_g.reshape(depth, 1, GDN_D))

    sb_cache = jnp.transpose(cache_sb_kv, (0, 1, 3, 4, 5, 2))
    nsa_cache = jnp.transpose(cache_nsa_kv, (0, 1, 3, 4, 5, 2))
    nsa_win = cache_nsa_win.reshape(depth, n_db, cache_nsa_win.shape[2], 2 * LANES)

    y_p = x_prompt.reshape(n_b * seq, d_model)
    y_s = x_sample.reshape(n_db * dec_seq, d_model)
    st_p, st_s = [], []
    for l in range(depth):
        past_p = dict(page_table=None,
                      lru_h=jnp.zeros((n_b, 1, GW), F32), lru_conv=jnp.zeros((n_b, CONV_W - 1, GW), F32),
                      gdn_S=jnp.zeros((n_b, GDN_H, GDN_D, GDN_D), F32),
                      gdn_conv=jnp.zeros((n_b, CONV_W - 1, 3 * GW), F32))
        past_s = dict(page_table=page_table, sb_kv=sb_cache, nsa_kv=nsa_cache, nsa_win=nsa_win,
                      lru_h=state_lru_h[l].reshape(n_db, 1, GW), lru_conv=state_lru_conv[l],
                      gdn_S=state_gdn_S[l], gdn_conv=state_gdn_conv[l])
        y_s, new_s, ffn_bf16 = _trunk_layer(y_s, n_db, dec_seq, past_len, past_s, wts, l, alpha)
        y_p, new_p, _ = _trunk_layer(y_p, n_b, seq, 0, past_p, wts, l, alpha, ffn_bf16=ffn_bf16)
        st_p.append(new_p)
        st_s.append(new_s)
    p = [jnp.stack(a) for a in zip(*st_p)]
    s = [jnp.stack(a) for a in zip(*st_s)]
    return (y_p.reshape(n_b, seq, d_model), y_s.reshape(n_db, dec_seq, d_model),
            p[0], p[1], p[2], p[3], p[4], p[5], p[6], s[0], s[1], s[2], s[3], s[4], s[5], s[6])
```

```python
import functools
import math

import jax
import jax.numpy as jnp
from jax import lax
from jax.experimental import pallas as pl
from jax.experimental.pallas import tpu as pltpu

F32 = jnp.float32
BF16 = jnp.bfloat16

PAGE = 128
GW = 512
HD = 64
SB_H = 8
NSA_G = 2
NSA_R = 4
CMP_BLOCK = 64
CMP_PITCH = 72
SEL_TOPN = 16
SEL_FORCE = float(NSA_R + 1)
WINDOW = 512
LRU_BLKS = 8
LRU_C = 8.0
CONV_W = 4
GDN_H = 4
GDN_D = 128
GDN_CHUNK = 64
LN_EPS = 1e-5
RMS_EPS = 1e-6
NEG = -1e30
LOG2E = 1.4426950408889634
QSCALE2 = (HD ** -0.5) * LOG2E

LANES = 128
SUBLANES = 8
VMEM_LIMIT = 56 * 1024 * 1024

C_SBK, C_SBV, C_SBQ, C_NSAQ, C_NSAKV = 0, 512, 1024, 1536, 2048
C_LRUX, C_LRUG, C_GZ, C_NSAW, C_SMALL, C_GQKV = 2560, 3072, 3584, 4096, 4352, 4608
P_IN_PAD = 6144
SM_GATE, SM_BETA, SM_DECAY = 0, 24, 28
CONV_PAD = 8
PAGES_PER_STEP = 16


def _pick(n, pref, mult=SUBLANES):
    if n <= pref:
        return n
    for t in range(pref, 0, -1):
        if n % t == 0 and t % mult == 0:
            return t
    return n


def _cparams(sem):
    return pltpu.CompilerParams(dimension_semantics=sem, vmem_limit_bytes=VMEM_LIMIT)


def _sigmoid(x):
    return 1.0 / (1.0 + jnp.exp(-x))


def _silu(x):
    return x * _sigmoid(x)


def _softplus(x):
    return jnp.maximum(x, 0.0) + jnp.log1p(jnp.exp(-jnp.abs(x)))


def _gelu_tanh(x):
    return 0.5 * x * (1.0 + jnp.tanh(math.sqrt(2.0 / math.pi) * (x + 0.044715 * (x * x * x))))


def _layer_norm(y, g, b):
    mu = jnp.mean(y, axis=-1, keepdims=True)
    d = y - mu
    var = jnp.mean(d * d, axis=-1, keepdims=True)
    return d * lax.rsqrt(var + LN_EPS) * g + b


def _dot(a, b):
    return jnp.dot(a, b, preferred_element_type=F32)


def _dot_nt(a, b):
    return lax.dot_general(a, b, (((1,), (1,)), ((), ())), preferred_element_type=F32)


def _dot_tn(a, b):
    return lax.dot_general(a, b, (((0,), (0,)), ((), ())), preferred_element_type=F32)


def _split_bf16(x):
    hi = x.astype(BF16)
    return hi, (x - hi.astype(F32)).astype(BF16)


def _dot_hp(a, b):
    ah, al = _split_bf16(a)
    bh, bl = _split_bf16(b)
    return _dot(jnp.concatenate([ah, ah, al], axis=1), jnp.concatenate([bh, bl, bh], axis=0))


def _dot_hp_exact_lhs(a_bf16, b):
    bh, bl = _split_bf16(b)
    return _dot(jnp.concatenate([a_bf16, a_bf16], axis=1), jnp.concatenate([bh, bl], axis=0))


def _dot_hp_exact_rhs(a, b_bf16):
    ah, al = _split_bf16(a)
    return _dot(jnp.concatenate([ah, al], axis=1), jnp.concatenate([b_bf16, b_bf16], axis=0))


def _ffn_ln_kernel(x_ref, wg_ref, wu_ref, wd_ref, g_ref, b_ref, o_ref, *rest, alpha, emit_bf16):
    xb_ref = rest[-1]
    j = pl.program_id(1)

    @pl.when(j == 0)
    def _():
        xb_ref[...] = x_ref[...].astype(BF16)
        o_ref[...] = jnp.zeros_like(o_ref)

    xb = xb_ref[...]
    wg, wu, wd = wg_ref[...].astype(BF16), wu_ref[...].astype(BF16), wd_ref[...].astype(BF16)
    if emit_bf16:
        rest[0][...], rest[1][...], rest[2][...] = wg, wu, wd
    act = (_silu(_dot(xb, wg)) * _dot(xb, wu)).astype(BF16)
    o_ref[...] += _dot(act, wd)

    @pl.when(j == pl.num_programs(1) - 1)
    def _():
        y = alpha * x_ref[...] + 0.5 * o_ref[...]
        o_ref[...] = _layer_norm(y, g_ref[...], b_ref[...])


def _ffn_ln(x, weights, ln_g, ln_b, *, layer, which, ln_idx, alpha, emit_bf16=False):
    n, d = x.shape
    tm = _pick(n, 1024)
    if len(weights) == 2:
        wgu, wdn = weights
        f = wdn.shape[2]
        tf = _pick(f, 256, LANES)
        nf = f // tf
        w_args = (wgu, wgu, wdn)
        w_specs = [
            pl.BlockSpec((None, None, d, tf), lambda i, j: (layer, which, 0, j)),
            pl.BlockSpec((None, None, d, tf), lambda i, j: (layer, which, 0, j + nf)),
            pl.BlockSpec((None, None, tf, d), lambda i, j: (layer, which, j, 0)),
        ]
    else:
        w_args = weights
        f = weights[2].shape[0]
        tf = _pick(f, 256, LANES)
        nf = f // tf
        w_specs = [
            pl.BlockSpec((d, tf), lambda i, j: (0, j)),
            pl.BlockSpec((d, tf), lambda i, j: (0, j)),
            pl.BlockSpec((tf, d), lambda i, j: (j, 0)),
        ]
    out_specs = [pl.BlockSpec((tm, d), lambda i, j: (i, 0))]
    out_shape = [jax.ShapeDtypeStruct((n, d), F32)]
    if emit_bf16:
        assert n == tm, "weight tiles are written once only when there is a single row block"
        out_specs += [pl.BlockSpec((d, tf), lambda i, j: (0, j)), pl.BlockSpec((d, tf), lambda i, j: (0, j)),
                      pl.BlockSpec((tf, d), lambda i, j: (j, 0))]
        out_shape += [jax.ShapeDtypeStruct((d, f), BF16), jax.ShapeDtypeStruct((d, f), BF16),
                      jax.ShapeDtypeStruct((f, d), BF16)]
    outs = pl.pallas_call(
        functools.partial(_ffn_ln_kernel, alpha=alpha, emit_bf16=emit_bf16),
        grid=(n // tm, nf),
        in_specs=[
            pl.BlockSpec((tm, d), lambda i, j: (i, 0)),
            *w_specs,
            pl.BlockSpec((None, None, 1, d), lambda i, j: (layer, ln_idx, 0, 0)),
            pl.BlockSpec((None, None, 1, d), lambda i, j: (layer, ln_idx, 0, 0)),
        ],
        out_specs=out_specs,
        out_shape=out_shape,
        scratch_shapes=[pltpu.VMEM((tm, d), BF16)],
        compiler_params=_cparams(("parallel", "arbitrary")),
        name="ffn_ln",
    )(x, *w_args, ln_g, ln_b)
    return (outs[0], tuple(outs[1:])) if emit_bf16 else outs[0]


def _matmul_kernel(x_ref, w_ref, o_ref):
    k = pl.program_id(2)

    @pl.when(k == 0)
    def _():
        o_ref[...] = jnp.zeros_like(o_ref)

    o_ref[...] += _dot(x_ref[...].astype(BF16), w_ref[...])


def _matmul(x, w, *, layer, tm_pref=1024, tn_pref=512, tk_pref=2048, name="matmul"):
    m, kdim = x.shape
    nout = w.shape[2]
    tm = _pick(m, tm_pref)
    tn = _pick(nout, tn_pref, LANES)
    tk = _pick(kdim, tk_pref, LANES)
    return pl.pallas_call(
        _matmul_kernel,
        grid=(m // tm, nout // tn, kdim // tk),
        in_specs=[
            pl.BlockSpec((tm, tk), lambda i, j, k: (i, k)),
            pl.BlockSpec((None, tk, tn), lambda i, j, k: (layer, k, j)),
        ],
        out_specs=pl.BlockSpec((tm, tn), lambda i, j, k: (i, j)),
        out_shape=jax.ShapeDtypeStruct((m, nout), F32),
        compiler_params=_cparams(("parallel", "parallel", "arbitrary")),
        name=name,
    )(x, w)


def _gather_nsa_kernel(pt_ref, *refs, n_group, t_new):
    del pt_ref
    page_refs = refs[:n_group]
    new_ref, cmp_ref = refs[n_group:]
    s = pl.program_id(1)
    last = pl.num_programs(1) - 1
    blocks_per_page = PAGE // CMP_BLOCK

    @pl.when(s < last)
    def _():
        cmp_ref[:, CMP_BLOCK:CMP_PITCH, :] = jnp.zeros((cmp_ref.shape[0], CMP_PITCH - CMP_BLOCK, 2 * LANES), F32)
        for g in range(n_group):
            for kind in range(2):
                x = page_refs[g][kind].reshape(NSA_G * HD, PAGE).T
                cols = slice(kind * LANES, (kind + 1) * LANES)
                for nl in range(blocks_per_page):
                    cmp_ref[g * blocks_per_page + nl, 0:CMP_BLOCK, cols] = x[nl * CMP_BLOCK:(nl + 1) * CMP_BLOCK]

    @pl.when(s == last)
    def _():
        cmp_ref[...] = jnp.zeros_like(cmp_ref)
        cmp_ref[0, 0:t_new, :] = new_ref[...]


def _gather_nsa_pages(cache_t, page_table, new_arr, new_colblk, *, layer):
    b, n_pages = page_table.shape
    t_new = new_arr.shape[1]
    assert t_new <= CMP_BLOCK
    n_group = math.gcd(n_pages, PAGES_PER_STEP)
    n_steps = n_pages // n_group + 1
    blocks = n_group * PAGE // CMP_BLOCK

    def page_map(g):
        return lambda bi, s, pt: (layer, pt[bi, jnp.minimum(s * n_group + g, n_pages - 1)], 0, 0, 0, 0)

    in_specs = [pl.BlockSpec((None, None, 2, NSA_G, HD, PAGE), page_map(g)) for g in range(n_group)]
    in_specs.append(pl.BlockSpec((None, t_new, 2 * LANES), lambda bi, s, pt: (bi, 0, new_colblk)))
    return pl.pallas_call(
        functools.partial(_gather_nsa_kernel, n_group=n_group, t_new=t_new),
        grid_spec=pltpu.PrefetchScalarGridSpec(
            num_scalar_prefetch=1, grid=(b, n_steps), in_specs=in_specs,
            out_specs=pl.BlockSpec((None, blocks, CMP_PITCH, 2 * LANES), lambda bi, s, pt: (bi, s, 0, 0))),
        out_shape=jax.ShapeDtypeStruct((b, n_steps * blocks, CMP_PITCH, 2 * LANES), F32),
        compiler_params=_cparams(("parallel", "arbitrary")),
        name="gather_nsa_pages",
    )(page_table, *([cache_t] * n_group), new_arr)


def _suffix_matrix(n):
    j = lax.broadcasted_iota(jnp.int32, (2 * n, n), 0)
    s = lax.broadcasted_iota(jnp.int32, (2 * n, n), 1)
    return ((j > s) & ((j < n) | (j - n > s))).astype(BF16)


def _sb_neg_log(z2, mask):
    nl = jnp.maximum(z2, 0.0) + jnp.log2(1.0 + jnp.exp2(-jnp.abs(z2)))
    if mask is not None:
        nl = jnp.where(mask, nl, 0.0)
    hi = nl.astype(BF16)
    lo = (nl - hi.astype(F32)).astype(BF16)
    return nl, jnp.concatenate([hi, lo], axis=1)


def _sb_weight(z2, nl, tail, carry, mask):
    w = jnp.exp2(z2 - nl - tail - carry)
    if mask is not None:
        w = jnp.where(mask, w, 0.0)
    return w.astype(BF16)


def _sb_weights(z2, tri2, carry, mask):
    nl, hilo = _sb_neg_log(z2, mask)
    tail = _dot(hilo, tri2)
    return _sb_weight(z2, nl, tail, carry, mask), tail[:, 0:1] + nl[:, 0:1]


def _sb_kernel(qi_ref, ki_ref, q_ref, k_ref, v_ref, o_ref, acc_ref, c_ref, *, tq, tk, q0):
    p = pl.program_id(1)
    i = qi_ref[p]
    jt = ki_ref[p]
    qs0 = q0 + i * tq

    @pl.when(jt == (qs0 + tq - 2) // tk)
    def _():
        acc_ref[...] = jnp.zeros_like(acc_ref)
        c_ref[...] = jnp.zeros_like(c_ref)

    def tile(masked):
        mask = None
        if masked:
            qpos = qs0 + lax.broadcasted_iota(jnp.int32, (tq, tk), 0)
            kpos = jt * tk + lax.broadcasted_iota(jnp.int32, (tq, tk), 1)
            mask = kpos < qpos
        tri = _suffix_matrix(tk)
        lane_half = lax.broadcasted_iota(jnp.int32, (tq, LANES), 1) // HD
        cols = [slice((h // 2) * LANES, (h // 2 + 1) * LANES) for h in range(SB_H)]
        zs = []
        for h in range(SB_H):
            qc = jnp.where(lane_half == h % 2, q_ref[:, cols[h]] * QSCALE2, 0.0).astype(BF16)
            zs.append(_dot_nt(qc, k_ref[:, cols[h]].astype(BF16)))
        nls, hilos = [], []
        for h in range(SB_H):
            nl, hilo = _sb_neg_log(zs[h], mask)
            nls.append(nl)
            hilos.append(hilo)
        tails = [_dot(hilos[h], tri) for h in range(SB_H)]
        ws = []
        for h in range(SB_H):
            ws.append(_sb_weight(zs[h], nls[h], tails[h], c_ref[h], mask))
            c_ref[h] += tails[h][:, 0:1] + nls[h][:, 0:1]
        for h in range(SB_H):
            acc_ref[h] += _dot(ws[h], v_ref[:, cols[h]].astype(BF16))

    whole = (jt + 1) * tk <= qs0

    @pl.when(whole)
    def _():
        tile(False)

    @pl.when(jnp.logical_not(whole))
    def _():
        tile(True)

    @pl.when(jt == 0)
    def _():
        lane_half = lax.broadcasted_iota(jnp.int32, (tq, LANES), 1) // HD
        for c in range(SB_H // 2):
            o_ref[:, c * LANES:(c + 1) * LANES] = jnp.where(lane_half == 0, acc_ref[2 * c], acc_ref[2 * c + 1])


def _sb_attend(q_arr, q_blk, k_arr, k_blk, v_arr, v_blk, *, q0):
    b, t = q_arr.shape[:2]
    l = k_arr.shape[1]
    tq = _pick(t, 256)
    tk = _pick(l, 256, LANES)
    pairs = [(i, jt) for i in range(t // tq) for jt in range((q0 + (i + 1) * tq - 2) // tk, -1, -1)]
    qi = jnp.asarray([pr[0] for pr in pairs], jnp.int32)
    ki = jnp.asarray([pr[1] for pr in pairs], jnp.int32)
    return pl.pallas_call(
        functools.partial(_sb_kernel, tq=tq, tk=tk, q0=q0),
        grid_spec=pltpu.PrefetchScalarGridSpec(
            num_scalar_prefetch=2, grid=(b, len(pairs)),
            in_specs=[
                pl.BlockSpec((None, tq, GW), lambda bi, p, qi_r, ki_r: (bi, qi_r[p], q_blk)),
                pl.BlockSpec((None, tk, GW), lambda bi, p, qi_r, ki_r: (bi, ki_r[p], k_blk)),
                pl.BlockSpec((None, tk, GW), lambda bi, p, qi_r, ki_r: (bi, ki_r[p], v_blk)),
            ],
            out_specs=pl.BlockSpec((None, tq, GW), lambda bi, p, qi_r, ki_r: (bi, qi_r[p], 0)),
            scratch_shapes=[pltpu.VMEM((SB_H, tq, LANES), F32), pltpu.VMEM((SB_H, tq, 1), F32)]),
        out_shape=jax.ShapeDtypeStruct((b, t, GW), F32),
        compiler_params=_cparams(("parallel", "arbitrary")),
        name="sb_attend",
    )(qi, ki, q_arr, k_arr, v_arr)


def _sb_paged_kernel(pt_ref, q_ref, new_ref, *refs, n_group, t):
    del pt_ref
    page_refs = refs[:n_group]
    o_ref, qbd_ref, acc_ref, c_ref = refs[n_group:]
    s = pl.program_id(1)
    m = SB_H * t
    tri = _suffix_matrix(PAGE)

    @pl.when(s == 0)
    def _():
        lane_head = lax.broadcasted_iota(jnp.int32, (t, GW), 1) // HD
        q = q_ref[...] * QSCALE2
        for h in range(SB_H):
            qbd_ref[h * t:(h + 1) * t, :] = jnp.where(lane_head == h, q, 0.0)
        pad = jnp.zeros((PAGE - t, GW), F32)
        k_new = jnp.concatenate([new_ref[:, 0:GW], pad], axis=0).astype(BF16)
        v_new = jnp.concatenate([new_ref[:, GW:2 * GW], pad], axis=0).astype(BF16)
        row = lax.broadcasted_iota(jnp.int32, (m, PAGE), 0)
        mask = lax.broadcasted_iota(jnp.int32, (m, PAGE), 1) < row - (row // t) * t
        w, rs = _sb_weights(_dot_nt(qbd_ref[...].astype(BF16), k_new), tri, 0.0, mask)
        acc_ref[...] = _dot(w, v_new)
        c_ref[...] = rs

    qb = qbd_ref[...].astype(BF16)
    pages = range(n_group)
    zs = [_dot(qb, page_refs[g][0].reshape(GW, PAGE).astype(BF16)) for g in pages]
    nls, hilos = zip(*[_sb_neg_log(zs[g], None) for g in pages])
    tails = [_dot(hilos[g], tri) for g in pages]
    carries = [None] * n_group
    carry = c_ref[...]
    for g in reversed(pages):
        carries[g] = carry
        carry = carry + tails[g][:, 0:1] + nls[g][:, 0:1]
    c_ref[...] = carry
    ws = [_sb_weight(zs[g], nls[g], tails[g], carries[g], None) for g in pages]
    acc = acc_ref[...]
    for g in pages:
        acc = acc + _dot_nt(ws[g], page_refs[g][1].reshape(GW, PAGE).astype(BF16))
    acc_ref[...] = acc

    @pl.when(s == pl.num_programs(1) - 1)
    def _():
        lane_head = lax.broadcasted_iota(jnp.int32, (t, GW), 1) // HD
        out = jnp.zeros((t, GW), F32)
        for h in range(SB_H):
            out = out + jnp.where(lane_head == h, acc_ref[h * t:(h + 1) * t, :], 0.0)
        o_ref[...] = out


def _sb_attend_paged(q_arr, q_blk, new_blk, cache_t, page_table, *, layer):
    b, t = q_arr.shape[:2]
    n_pages = page_table.shape[1]
    n_group = math.gcd(n_pages, PAGES_PER_STEP)
    n_steps = n_pages // n_group
    m = SB_H * t

    def page_map(g):
        return lambda bi, s, pt: (layer, pt[bi, n_pages - (s + 1) * n_group + g], 0, 0, 0, 0)

    in_specs = [
        pl.BlockSpec((None, t, GW), lambda bi, s, pt: (bi, 0, q_blk)),
        pl.BlockSpec((None, t, 2 * GW), lambda bi, s, pt: (bi, 0, new_blk)),
    ] + [pl.BlockSpec((None, None, 2, SB_H, HD, PAGE), page_map(g)) for g in range(n_group)]
    return pl.pallas_call(
        functools.partial(_sb_paged_kernel, n_group=n_group, t=t),
        grid_spec=pltpu.PrefetchScalarGridSpec(
            num_scalar_prefetch=1, grid=(b, n_steps), in_specs=in_specs,
            out_specs=pl.BlockSpec((None, t, GW), lambda bi, s, pt: (bi, 0, 0)),
            scratch_shapes=[pltpu.VMEM((m, GW), F32), pltpu.VMEM((m, GW), F32), pltpu.VMEM((m, 1), F32)]),
        out_shape=jax.ShapeDtypeStruct((b, t, GW), F32),
        compiler_params=_cparams(("parallel", "arbitrary")),
        name="sb_attend_paged",
    )(page_table, q_arr, q_arr, *([cache_t] * n_group))


def _masked_softmax(s2, mask):
    sm = jnp.where(mask, s2, NEG)
    m = jnp.max(sm, axis=-1, keepdims=True)
    e = jnp.where(mask, jnp.exp2(sm - m), 0.0)
    return e / jnp.maximum(jnp.sum(e, axis=-1, keepdims=True), 1e-30)


def _compress_kernel(xk_ref, xv_ref, w_ref, o_ref, *, pitch):
    width = 2 * LANES
    n = o_ref.shape[0]
    parts = []
    for j in range(CMP_BLOCK):
        rows = pl.ds(j, n, stride=pitch)
        xj = jnp.concatenate([xk_ref[rows, :], xv_ref[rows, :]], axis=1).astype(BF16)
        parts.append(_dot(xj, w_ref[j * width:(j + 1) * width, :]))
    while len(parts) > 1:
        parts = [parts[i] + parts[i + 1] for i in range(0, len(parts), 2)]
    o_ref[...] = parts[0]


def _nsa_compress(x_arr, x_blk, phi_big, *, layer):
    b, nb, pitch = x_arr.shape[:3]
    width = 2 * LANES
    tn = _pick(nb, 160)
    return pl.pallas_call(
        functools.partial(_compress_kernel, pitch=pitch),
        grid=(b, nb // tn),
        in_specs=[
            pl.BlockSpec((None, tn * pitch, LANES), lambda bi, i: (bi, i, 2 * x_blk)),
            pl.BlockSpec((None, tn * pitch, LANES), lambda bi, i: (bi, i, 2 * x_blk + 1)),
            pl.BlockSpec((None, CMP_BLOCK * width, width), lambda bi, i: (layer, 0, 0)),
        ],
        out_specs=pl.BlockSpec((None, tn, width), lambda bi, i: (bi, i, 0)),
        out_shape=jax.ShapeDtypeStruct((b, nb, width), F32),
        compiler_params=_cparams(("parallel", "parallel")),
        name="nsa_compress",
    )(*([x_arr.reshape(b, nb * pitch, x_arr.shape[3])] * 2), phi_big)


def _select_blocks(imp, blk, nb, n_sel):
    tq, nbp = imp.shape
    if tq % LANES != 0:
        rank = jnp.zeros((tq, nbp), F32)
        for mblk in range(nb):
            col = imp[:, mblk:mblk + 1]
            beats = (col > imp) | ((col == imp) & (blk > mblk))
            rank = rank + jnp.where(beats, 1.0, 0.0)
        return jnp.where(rank < n_sel, 1.0, 0.0)
    imp_t = imp.T
    n_parts = -(-nb // SUBLANES)
    parts = [imp_t[p * SUBLANES:(p + 1) * SUBLANES, :] for p in range(n_parts)]
    row_in_part = lax.broadcasted_iota(jnp.int32, (SUBLANES, tq), 0)
    ranks = [jnp.zeros((SUBLANES, tq), F32) for _ in range(n_parts)]
    for mblk in range(nb):
        pm, rm = divmod(mblk, SUBLANES)
        row = parts[pm][rm:rm + 1, :]
        for p in range(n_parts):
            if p < pm:
                beats = row > parts[p]
            elif p > pm:
                beats = row >= parts[p]
            else:
                beats = (row > parts[p]) | ((row == parts[p]) & (row_in_part > rm))
            ranks[p] = ranks[p] + jnp.where(beats, 1.0, 0.0)
    sel_t = [jnp.where(r < n_sel, 1.0, 0.0) for r in ranks]
    if nbp > n_parts * SUBLANES:
        sel_t.append(jnp.zeros((nbp - n_parts * SUBLANES, tq), F32))
    return jnp.concatenate(sel_t, axis=0).T


def _nsa_kernel(*refs, n_page_refs, tq, tk, q0, lw, nb, nbp, n_sel, wl):
    n_prefetch = 3 if n_page_refs else 2
    qi_ref, ki_ref = refs[0], refs[1]
    q_ref, sm_ref, kc_ref = refs[n_prefetch:n_prefetch + 3]
    key_refs = refs[n_prefetch + 3:n_prefetch + 4 + n_page_refs]
    win_ref, o_ref, qs_ref, sel_ref, part_ref, m_ref, l_ref, acc_ref = refs[n_prefetch + 4 + n_page_refs:]
    p = pl.program_id(1)
    i = qi_ref[p]
    j = ki_ref[p]
    rq = NSA_R * tq
    qs0 = q0 + i * tq
    j_last = (qs0 + tq - 1) // tk
    scale = QSCALE2

    @pl.when(j == 0)
    def _():
        lane_half = lax.broadcasted_iota(jnp.int32, (tq, LANES), 1) // HD
        gates = _sigmoid(sm_ref[...])
        qpos_b = qs0 + lax.broadcasted_iota(jnp.int32, (tq, nbp), 0)
        blk = lax.broadcasted_iota(jnp.int32, (tq, nbp), 1)
        cmask = ((blk + 1) * CMP_BLOCK - 1 <= qpos_b) & (blk < nb)
        cur = qpos_b // CMP_BLOCK
        forced = (blk == 0) | (blk == cur)
        valid = blk <= cur
        start = pl.multiple_of(jnp.maximum(lw + (i + 1) * tq - wl, 0), SUBLANES)
        qpos_w = qs0 + lax.broadcasted_iota(jnp.int32, (tq, wl), 0)
        kpos_w = q0 - lw + start + lax.broadcasted_iota(jnp.int32, (tq, wl), 1)
        wmask = (kpos_w <= qpos_w) & (kpos_w > qpos_w - WINDOW)
        kw = win_ref[pl.ds(start, wl), 0:LANES].astype(BF16)
        vw = win_ref[pl.ds(start, wl), LANES:2 * LANES].astype(BF16)
        kc = kc_ref[:, 0:LANES].astype(BF16)
        vc = kc_ref[:, LANES:2 * LANES].astype(BF16)
        groups = range(NSA_G)
        for g in groups:
            for r in range(NSA_R):
                chunk, half = 2 * g + r // 2, r % 2
                qc = q_ref[:, chunk * LANES:(chunk + 1) * LANES]
                if half != g:
                    qc = pltpu.roll(qc, HD, 1)
                qs_ref[g, r * tq:(r + 1) * tq, :] = jnp.where(lane_half == g, qc * scale, 0.0)
        qs = [qs_ref[g].astype(BF16) for g in groups]
        s_c = [_dot_nt(qs[g], kc).reshape(NSA_R, tq, nbp) for g in groups]
        s_w = [_dot_nt(qs[g], kw).reshape(NSA_R, tq, wl) for g in groups]
        p_c = [_masked_softmax(s_c[g], cmask[None]) for g in groups]
        p_w = [_masked_softmax(s_w[g], wmask[None]) for g in groups]
        o_c = [_dot(p_c[g].reshape(rq, nbp).astype(BF16), vc) for g in groups]
        o_w = [_dot(p_w[g].reshape(rq, wl).astype(BF16), vw) for g in groups]
        for g in groups:
            imp = jnp.sum(p_c[g], axis=0)
            imp = jnp.where(forced, SEL_FORCE, jnp.where(valid, imp, -1.0))
            imp = jnp.where(blk < nb, imp, -2.0)
            sel_ref[g] = _select_blocks(imp, blk, nb, n_sel)
            for r in range(NSA_R):
                rows = slice(r * tq, (r + 1) * tq)
                lane_c = SM_GATE + 0 * NSA_G * NSA_R + g * NSA_R + r
                lane_w = SM_GATE + 2 * NSA_G * NSA_R + g * NSA_R + r
                part_ref[g, rows, :] = (gates[:, lane_c:lane_c + 1] * o_c[g][rows]
                                        + gates[:, lane_w:lane_w + 1] * o_w[g][rows])
        m_ref[...] = jnp.full_like(m_ref, NEG)
        l_ref[...] = jnp.zeros_like(l_ref)
        acc_ref[...] = jnp.zeros_like(acc_ref)

    slab = tq if tq >= LANES else rq
    pieces = [(g, slice(r0, r0 + slab)) for g in range(NSA_G) for r0 in range(0, rq, slab)]

    def sel_tile(n, kpos0, score_fn, pv_fn):
        kblk = (kpos0 + lax.broadcasted_iota(jnp.int32, (nbp, n), 1)) // CMP_BLOCK
        expand = (lax.broadcasted_iota(jnp.int32, (nbp, n), 0) == kblk).astype(BF16)
        qpos = qs0 + lax.broadcasted_iota(jnp.int32, (tq, n), 0)
        kpos = kpos0 + lax.broadcasted_iota(jnp.int32, (tq, n), 1)
        causal = kpos <= qpos
        biases = []
        for g in range(NSA_G):
            bias = jnp.where((_dot(sel_ref[g].astype(BF16), expand) > 0.5) & causal, 0.0, NEG)
            biases.append(bias if slab == tq else jnp.concatenate([bias] * (slab // tq), axis=0))
        scores = [score_fn(qs_ref[g, rows, :].astype(BF16)) + biases[g] for g, rows in pieces]
        probs, corrs = [], []
        for (g, rows), s in zip(pieces, scores):
            m_old = m_ref[g, rows, :]
            m_new = jnp.maximum(m_old, jnp.max(s, axis=-1, keepdims=True))
            pr = jnp.exp2(s - m_new)
            corr = jnp.exp2(m_old - m_new)
            l_ref[g, rows, :] = corr * l_ref[g, rows, :] + jnp.sum(pr, axis=-1, keepdims=True)
            m_ref[g, rows, :] = m_new
            probs.append(pr.astype(BF16))
            corrs.append(corr)
        for (g, rows), pr, corr in zip(pieces, probs, corrs):
            acc_ref[g, rows, :] = corr * acc_ref[g, rows, :] + pv_fn(pr)

    if n_page_refs == 0:
        ks = key_refs[0][:, 0:LANES].astype(BF16)
        vs = key_refs[0][:, LANES:2 * LANES].astype(BF16)
        sel_tile(tk, j * tk, lambda q: _dot_nt(q, ks), lambda pr: _dot(pr, vs))
    else:
        n_past_tiles = q0 // tk

        @pl.when(j < n_past_tiles)
        def _():
            kst = jnp.concatenate([r[0].reshape(NSA_G * HD, PAGE) for r in key_refs[:-1]], axis=1).astype(BF16)
            vst = jnp.concatenate([r[1].reshape(NSA_G * HD, PAGE) for r in key_refs[:-1]], axis=1).astype(BF16)
            sel_tile(tk, j * tk, lambda q: _dot(q, kst), lambda pr: _dot_nt(pr, vst))

        @pl.when(j == n_past_tiles)
        def _():
            pad = jnp.zeros((PAGE - tq, LANES), F32)
            ks = jnp.concatenate([key_refs[-1][:, 0:LANES], pad], axis=0).astype(BF16)
            vs = jnp.concatenate([key_refs[-1][:, LANES:2 * LANES], pad], axis=0).astype(BF16)
            sel_tile(PAGE, q0, lambda q: _dot_nt(q, ks), lambda pr: _dot(pr, vs))

    @pl.when(j == j_last)
    def _():
        gates = _sigmoid(sm_ref[...])
        lane_half = lax.broadcasted_iota(jnp.int32, (tq, LANES), 1) // HD
        for g in range(NSA_G):
            o_s = acc_ref[g] / jnp.maximum(l_ref[g], 1e-30)
            res = []
            for r in range(NSA_R):
                rows = slice(r * tq, (r + 1) * tq)
                lane_s = SM_GATE + 1 * NSA_G * NSA_R + g * NSA_R + r
                res.append(part_ref[g, rows, :] + gates[:, lane_s:lane_s + 1] * o_s[rows])
            for c2 in range(2):
                even, odd = res[2 * c2], res[2 * c2 + 1]
                if g == 1:
                    even = pltpu.roll(even, HD, 1)
                else:
                    odd = pltpu.roll(odd, HD, 1)
                chunk = 2 * g + c2
                o_ref[:, chunk * LANES:(chunk + 1) * LANES] = jnp.where(lane_half == 0, even, odd)


def _nsa_attend(q_arr, q_blk, sm_arr, sm_blk, kc_arr, ks_arr, ks_blk, win_arr, win_blk, *, q0, lw, n_real_keys,
                paged=None):
    b, t = q_arr.shape[:2]
    nbp = kc_arr.shape[1]
    nb = -(-n_real_keys // CMP_BLOCK)
    n_sel = min(SEL_TOPN, nb)
    tq = _pick(t, 128)
    wl = min(WINDOW + tq, lw + t)
    rq = NSA_R * tq
    if paged is None:
        tk = _pick(ks_arr.shape[1], 1024, LANES)
        n_page_refs, prefetch = 0, ()
        key_specs = [pl.BlockSpec((None, tk, 2 * LANES), lambda bi, p, qi_r, ki_r: (bi, ki_r[p], ks_blk))]
        key_args = (ks_arr,)
    else:
        cache_t, page_table, layer = paged
        n_pages = page_table.shape[1]
        n_page_refs = math.gcd(n_pages, PAGES_PER_STEP)
        tk = n_page_refs * PAGE
        assert tq == t and q0 == n_pages * PAGE and q0 % tk == 0 and t <= PAGE
        prefetch = (page_table,)

        def page_map(g):
            def index_map(bi, p, qi_r, ki_r, pt):
                return (layer, pt[bi, jnp.minimum(ki_r[p] * n_page_refs + g, n_pages - 1)], 1, 0, 0, 0)
            return index_map

        key_specs = [pl.BlockSpec((None, None, 2, NSA_G, HD, PAGE), page_map(g)) for g in range(n_page_refs)]
        key_specs.append(pl.BlockSpec((None, t, 2 * LANES), lambda bi, p, qi_r, ki_r, pt: (bi, 0, ks_blk)))
        key_args = (cache_t,) * n_page_refs + (ks_arr,)
    pairs = [(i, j) for i in range(t // tq) for j in range((q0 + (i + 1) * tq - 1) // tk + 1)]
    qi = jnp.asarray([pr[0] for pr in pairs], jnp.int32)
    ki = jnp.asarray([pr[1] for pr in pairs], jnp.int32)
    return pl.pallas_call(
        functools.partial(_nsa_kernel, n_page_refs=n_page_refs, tq=tq, tk=tk, q0=q0, lw=lw, nb=nb, nbp=nbp,
                          n_sel=n_sel, wl=wl),
        grid_spec=pltpu.PrefetchScalarGridSpec(
            num_scalar_prefetch=2 + len(prefetch), grid=(b, len(pairs)),
            in_specs=[
                pl.BlockSpec((None, tq, GW), lambda bi, p, qi_r, ki_r, *_: (bi, qi_r[p], q_blk)),
                pl.BlockSpec((None, tq, LANES), lambda bi, p, qi_r, ki_r, *_: (bi, qi_r[p], sm_blk)),
                pl.BlockSpec((None, nbp, 2 * LANES), lambda bi, p, qi_r, ki_r, *_: (bi, 0, 0)),
                *key_specs,
                pl.BlockSpec((None, lw + t, 2 * LANES), lambda bi, p, qi_r, ki_r, *_: (bi, 0, win_blk)),
            ],
            out_specs=pl.BlockSpec((None, tq, GW), lambda bi, p, qi_r, ki_r, *_: (bi, qi_r[p], 0)),
            scratch_shapes=[
                pltpu.VMEM((NSA_G, rq, LANES), F32),
                pltpu.VMEM((NSA_G, tq, nbp), F32),
                pltpu.VMEM((NSA_G, rq, LANES), F32),
                pltpu.VMEM((NSA_G, rq, 1), F32),
                pltpu.VMEM((NSA_G, rq, 1), F32),
                pltpu.VMEM((NSA_G, rq, LANES), F32),
            ]),
        out_shape=jax.ShapeDtypeStruct((b, t, GW), F32),
        compiler_params=_cparams(("parallel", "arbitrary")),
        name="nsa_attend",
    )(qi, ki, *prefetch, q_arr, sm_arr, kc_arr, *key_args, win_arr)


def _lru_kernel(x_ref, gt_ref, buf_ref, h0_ref, cw_ref, cb_ref, wbd_ref, gb_ref, lam_ref,
                o_ref, hl_ref, nb_ref, xp_ref, a_ref, b_ref, h_ref, *, tt):
    s = pl.program_id(1)
    hist = CONV_PAD - (CONV_W - 1)

    @pl.when(s == 0)
    def _():
        xp_ref[hist:CONV_PAD, :] = buf_ref[...]
        h_ref[...] = h0_ref[...]

    xp_ref[CONV_PAD:CONV_PAD + tt, :] = x_ref[...]
    u = cb_ref[...]
    for jw in range(CONV_W):
        u = u + cw_ref[jw:jw + 1, :] * xp_ref[hist + jw:hist + jw + tt, :]
    gts = _dot(u.astype(BF16), wbd_ref[...]) + gb_ref[...]
    r = _sigmoid(gts[:, :GW])
    ig = _sigmoid(gts[:, GW:])
    log_a = -LRU_C * r * _softplus(-lam_ref[...])
    a = jnp.exp(log_a)
    a_ref[...] = a
    b_ref[...] = jnp.sqrt(-jnp.tanh(log_a) * (a * a + 1.0)) * (ig * u)

    def step(t, h):
        h = a_ref[pl.ds(t, 1), :] * h + b_ref[pl.ds(t, 1), :]
        b_ref[pl.ds(t, 1), :] = h
        return h

    h = lax.fori_loop(0, tt, step, h_ref[...], unroll=8)
    h_ref[...] = h
    o_ref[...] = b_ref[...] * _gelu_tanh(gt_ref[...])
    tail = xp_ref[tt + hist:tt + CONV_PAD, :]
    xp_ref[hist:CONV_PAD, :] = tail

    @pl.when(s == pl.num_programs(1) - 1)
    def _():
        hl_ref[...] = h
        nb_ref[...] = tail


def _rg_lru(h_arr, buf, h0, conv_w, conv_b, wbd, gate_b, lam, *, layer):
    b, t = h_arr.shape[:2]
    tt = _pick(t, 512)
    x_blk, g_blk = C_LRUX // GW, C_LRUG // GW
    wspec = lambda shape: pl.BlockSpec((None,) + shape, lambda bi, s: (layer,) + (0,) * len(shape))
    return pl.pallas_call(
        functools.partial(_lru_kernel, tt=tt),
        grid=(b, t // tt),
        in_specs=[
            pl.BlockSpec((None, tt, GW), lambda bi, s: (bi, s, x_blk)),
            pl.BlockSpec((None, tt, GW), lambda bi, s: (bi, s, g_blk)),
            pl.BlockSpec((None, CONV_W - 1, GW), lambda bi, s: (bi, 0, 0)),
            pl.BlockSpec((None, 1, GW), lambda bi, s: (bi, 0, 0)),
            wspec((CONV_W, GW)), wspec((1, GW)), wspec((GW, 2 * GW)), wspec((1, 2 * GW)), wspec((1, GW)),
        ],
        out_specs=[
            pl.BlockSpec((None, tt, GW), lambda bi, s: (bi, s, 0)),
            pl.BlockSpec((None, 1, GW), lambda bi, s: (bi, 0, 0)),
            pl.BlockSpec((None, CONV_W - 1, GW), lambda bi, s: (bi, 0, 0)),
        ],
        out_shape=[
            jax.ShapeDtypeStruct((b, t, GW), F32),
            jax.ShapeDtypeStruct((b, 1, GW), F32),
            jax.ShapeDtypeStruct((b, CONV_W - 1, GW), F32),
        ],
        scratch_shapes=[
            pltpu.VMEM((CONV_PAD + tt, GW), F32),
            pltpu.VMEM((tt, GW), F32),
            pltpu.VMEM((tt, GW), F32),
            pltpu.VMEM((1, GW), F32),
        ],
        compiler_params=_cparams(("parallel", "arbitrary")),
        name="rg_lru",
    )(h_arr, h_arr, buf, h0, conv_w, conv_b, wbd, gate_b, lam)


def _gdn_kernel(qkv_ref, z_ref, sm_ref, buf_ref, s0_ref, cw_ref, al_ref, dtb_ref, ng_ref,
                o_ref, sout_ref, nb_ref, xp_ref, smp_ref, st_ref, *, tt, tc, nbs):
    s = pl.program_id(1)
    hist = CONV_PAD - (CONV_W - 1)
    n_chunks = tc // GDN_CHUNK

    @pl.when(s == 0)
    def _():
        xp_ref[...] = jnp.zeros_like(xp_ref)
        smp_ref[...] = jnp.zeros_like(smp_ref)
        xp_ref[:, hist:CONV_PAD, :] = buf_ref[...]
        st_ref[...] = s0_ref[...]

    xp_ref[:, CONV_PAD:CONV_PAD + tt, :] = qkv_ref[...]
    smp_ref[:, 0:tt, :] = sm_ref[...]
    row_ok = lax.broadcasted_iota(jnp.int32, (tc, 1), 0) < tt
    ri = lax.broadcasted_iota(jnp.int32, (tc, tc), 0)
    ci = lax.broadcasted_iota(jnp.int32, (tc, tc), 1)
    same = (ri // GDN_CHUNK) == (ci // GDN_CHUNK)
    incl = same & (ci <= ri)
    strict = same & (ci < ri)
    ones_where = lambda m: jnp.where(m, 1.0, 0.0).astype(BF16)
    eye = jnp.where(ri == ci, 1.0, 0.0)

    heads = range(nbs * GDN_H)
    pws, tinvs, qks, rhss, qgs, kds, gts = [], [], [], [], [], [], []
    for piece in heads:
        bb, h = divmod(piece, GDN_H)
        if h == 0:
            y = cw_ref[0:1, :] * xp_ref[bb, hist:hist + tc, :]
            for jw in range(1, CONV_W):
                y = y + cw_ref[jw:jw + 1, :] * xp_ref[bb, hist + jw:hist + jw + tc, :]
            y = jnp.where(row_ok, _silu(y), 0.0)
            sm = smp_ref[bb]
            beta_all = jnp.where(row_ok, _sigmoid(sm), 0.0)
            g_all = jnp.where(row_ok, -jnp.exp(al_ref[...]) * _softplus(sm + dtb_ref[...]), 0.0)
            gcum = _dot_hp_exact_lhs(ones_where(incl), g_all)
            gcum_t = _dot_hp_exact_rhs(g_all.T, ones_where(same & (ri <= ci)))
            gtot = _dot_hp_exact_lhs(ones_where(same), g_all)
        q = y[:, h * GDN_D:(h + 1) * GDN_D]
        k = y[:, GW + h * GDN_D:GW + (h + 1) * GDN_D]
        v = y[:, 2 * GW + h * GDN_D:2 * GW + (h + 1) * GDN_D]
        q = q * lax.rsqrt(jnp.sum(q * q, axis=-1, keepdims=True) + RMS_EPS) * (GDN_D ** -0.5)
        k = k * lax.rsqrt(jnp.sum(k * k, axis=-1, keepdims=True) + RMS_EPS)
        beta = beta_all[:, SM_BETA + h:SM_BETA + h + 1]
        gc = gcum[:, SM_DECAY + h:SM_DECAY + h + 1]
        gr = gcum_t[SM_DECAY + h:SM_DECAY + h + 1, :]
        gt = gtot[:, SM_DECAY + h:SM_DECAY + h + 1]
        decay = jnp.exp(jnp.where(incl, gc - gr, NEG))
        kb = k * beta
        kbf = k.astype(BF16)
        a_mat = jnp.where(strict, _dot_nt(kb.astype(BF16), kbf) * decay, 0.0)
        qks.append(jnp.where(incl, _dot_nt(q.astype(BF16), kbf) * decay, 0.0).astype(BF16))
        pws.append(-a_mat)
        tinvs.append(eye - a_mat)
        eg = jnp.exp(gc)
        rhss.append(jnp.concatenate([v * beta, kb * eg], axis=1))
        qgs.append((q * eg).astype(BF16))
        kds.append((k * jnp.exp(gt - gc)).astype(BF16))
        gts.append(gt)
    for _ in range(int(math.log2(GDN_CHUNK)) - 1):
        pws = [_dot_hp(pws[h], pws[h]) for h in heads]
        tinvs = [tinvs[h] + _dot_hp(tinvs[h], pws[h]) for h in heads]
    sols = [_dot_hp(tinvs[h], rhss[h]) for h in heads]
    us = [sols[h][:, :GDN_D] for h in heads]
    ws = [sols[h][:, GDN_D:].astype(BF16) for h in heads]
    sts = [st_ref[piece // GDN_H, piece % GDN_H] for piece in heads]
    outs = [[] for _ in heads]
    for c in range(n_chunks):
        rows = slice(c * GDN_CHUNK, (c + 1) * GDN_CHUNK)
        stbs = [sts[h].astype(BF16) for h in heads]
        v_news = [(us[h][rows] - _dot(ws[h][rows], stbs[h])).astype(BF16) for h in heads]
        for h in heads:
            outs[h].append(_dot(qgs[h][rows], stbs[h]) + _dot(qks[h][rows, rows], v_news[h]))
        for h in heads:
            g_last = jnp.exp(gts[h][c * GDN_CHUNK:c * GDN_CHUNK + 1, :])
            sts[h] = sts[h] * g_last + _dot_tn(kds[h][rows], v_news[h])
    for piece in heads:
        bb, h = divmod(piece, GDN_H)
        st_ref[bb, h] = sts[piece]
        o = outs[piece][0] if n_chunks == 1 else jnp.concatenate(outs[piece], axis=0)
        o = o * lax.rsqrt(jnp.mean(o * o, axis=-1, keepdims=True) + RMS_EPS) * ng_ref[...]
        o_ref[bb, :, h * GDN_D:(h + 1) * GDN_D] = o[0:tt] * _silu(z_ref[bb, :, h * GDN_D:(h + 1) * GDN_D])

    tail = xp_ref[:, tt + hist:tt + CONV_PAD, :]
    xp_ref[:, hist:CONV_PAD, :] = tail

    @pl.when(s == pl.num_programs(1) - 1)
    def _():
        sout_ref[...] = st_ref[...]
        nb_ref[...] = tail


def _gated_deltanet(h_arr, buf, s0, conv_w, a_log_row, dt_bias_row, norm_g, *, layer):
    b, t = h_arr.shape[:2]
    tt = _pick(t, 128)
    tc = -(-tt // GDN_CHUNK) * GDN_CHUNK
    assert tc == tt or t == tt, "sequence length must be a multiple of the chunk unless it fits one block"
    w3 = 3 * GW
    nbs = 2 if b % 2 == 0 else 1
    wspec = lambda shape: pl.BlockSpec((None,) + shape, lambda bi, s: (layer,) + (0,) * len(shape))
    return pl.pallas_call(
        functools.partial(_gdn_kernel, tt=tt, tc=tc, nbs=nbs),
        grid=(b // nbs, t // tt),
        in_specs=[
            pl.BlockSpec((nbs, tt, w3), lambda bi, s: (bi, s, C_GQKV // w3)),
            pl.BlockSpec((nbs, tt, GW), lambda bi, s: (bi, s, C_GZ // GW)),
            pl.BlockSpec((nbs, tt, LANES), lambda bi, s: (bi, s, C_SMALL // LANES)),
            pl.BlockSpec((nbs, CONV_W - 1, w3), lambda bi, s: (bi, 0, 0)),
            pl.BlockSpec((nbs, GDN_H, GDN_D, GDN_D), lambda bi, s: (bi, 0, 0, 0)),
            wspec((CONV_W, w3)), wspec((1, LANES)), wspec((1, LANES)), wspec((1, GDN_D)),
        ],
        out_specs=[
            pl.BlockSpec((nbs, tt, GW), lambda bi, s: (bi, s, 0)),
            pl.BlockSpec((nbs, GDN_H, GDN_D, GDN_D), lambda bi, s: (bi, 0, 0, 0)),
            pl.BlockSpec((nbs, CONV_W - 1, w3), lambda bi, s: (bi, 0, 0)),
        ],
        out_shape=[
            jax.ShapeDtypeStruct((b, t, GW), F32),
            jax.ShapeDtypeStruct((b, GDN_H, GDN_D, GDN_D), F32),
            jax.ShapeDtypeStruct((b, CONV_W - 1, w3), F32),
        ],
        scratch_shapes=[
            pltpu.VMEM((nbs, CONV_PAD + tc, w3), F32),
            pltpu.VMEM((nbs, tc, LANES), F32),
            pltpu.VMEM((nbs, GDN_H, GDN_D, GDN_D), F32),
        ],
        compiler_params=_cparams(("parallel", "arbitrary")),
        name="gated_deltanet",
    )(h_arr, h_arr, h_arr, buf, s0, conv_w, a_log_row, dt_bias_row, norm_g)


def _outproj_kernel(oa_ref, ob_ref, oc_ref, od_ref, x_ref, w_ref, gg_ref, g_ref, b_ref, o_ref, *, alpha):
    def rms(v, gain):
        return (v * lax.rsqrt(jnp.mean(v * v, axis=-1, keepdims=True) + RMS_EPS) * gain).astype(BF16)

    acc = _dot(rms(oa_ref[...], gg_ref[0:1, :]), w_ref[0:GW, :])
    acc += _dot(rms(ob_ref[...], gg_ref[1:2, :]), w_ref[GW:2 * GW, :])
    acc += _dot(rms(oc_ref[...], gg_ref[2:3, :]), w_ref[2 * GW:3 * GW, :])
    acc += _dot(od_ref[...].astype(BF16), w_ref[3 * GW:4 * GW, :])
    o_ref[...] = _layer_norm(alpha * x_ref[...] + acc, g_ref[...], b_ref[...])


def _outproj_ln(oa, ob, oc, od, x, w_out, grp_g, ln_g, ln_b, *, layer, alpha):
    n, d = x.shape
    tm = _pick(n, 512)
    mix_spec = pl.BlockSpec((tm, GW), lambda i: (i, 0))
    return pl.pallas_call(
        functools.partial(_outproj_kernel, alpha=alpha),
        grid=(n // tm,),
        in_specs=[
            mix_spec, mix_spec, mix_spec, mix_spec,
            pl.BlockSpec((tm, d), lambda i: (i, 0)),
            pl.BlockSpec((None, 4 * GW, d), lambda i: (layer, 0, 0)),
            pl.BlockSpec((None, 3, GW), lambda i: (layer, 0, 0)),
            pl.BlockSpec((None, None, 1, d), lambda i: (layer, 1, 0, 0)),
            pl.BlockSpec((None, None, 1, d), lambda i: (layer, 1, 0, 0)),
        ],
        out_specs=pl.BlockSpec((tm, d), lambda i: (i, 0)),
        out_shape=jax.ShapeDtypeStruct((n, d), F32),
        compiler_params=_cparams(("parallel",)),
        name="outproj_ln",
    )(oa, ob, oc, od, x, w_out, grp_g, ln_g, ln_b)


def _prep_weights(ffn_gu, ffn_down, w_in, w_out, nsa_phi, lru_gate_w, lru_gate_b, gdn_A_log, gdn_dt_bias):
    depth, d_model, _ = w_in.shape
    o_nsakv, o_gate, o_lrux, o_gqkv, o_gz, o_gb = 2048, 2816, 2840, 3864, 5400, 5912
    zeros = jnp.zeros((depth, d_model, P_IN_PAD - 5920), w_in.dtype)
    w_in_p = jnp.concatenate([
        w_in[..., 512:1536],
        w_in[..., 0:512],
        w_in[..., 1536:2048],
        w_in[..., o_nsakv:o_nsakv + 512],
        w_in[..., o_lrux:o_lrux + 1024],
        w_in[..., o_gz:o_gz + 512],
        w_in[..., o_nsakv + 512:o_gate],
        w_in[..., o_gate:o_lrux],
        w_in[..., o_gb:o_gb + 8],
        zeros,
        w_in[..., o_gqkv:o_gz],
    ], axis=-1).astype(BF16)
    phi_b = jnp.transpose(nsa_phi, (0, 2, 1, 3, 4)).astype(BF16)
    col_blocks = []
    for c in range(2):
        for g in range(NSA_G):
            hot = jnp.zeros((2, NSA_G, 1, 1), BF16).at[c, g].set(1)
            col_blocks.append(phi_b[:, :, :, None, :, :] * hot[None, None])
    phi_big = jnp.concatenate(col_blocks, axis=-1).reshape(depth, CMP_BLOCK * 2 * NSA_G * HD, 2 * NSA_G * HD)
    eye_b = jnp.eye(LRU_BLKS, dtype=F32)
    wbd = jnp.einsum("lknce,nm->lnckme", lru_gate_w, eye_b).reshape(depth, GW, 2 * GW).astype(BF16)
    gate_b = lru_gate_b.reshape(depth, 1, 2 * GW)
    pad_l = jnp.zeros((depth, SM_DECAY), F32)
    pad_r = jnp.zeros((depth, LANES - SM_DECAY - GDN_H), F32)
    a_log_row = jnp.concatenate([pad_l, gdn_A_log, pad_r], axis=1)[:, None, :]
    dt_bias_row = jnp.concatenate([pad_l, gdn_dt_bias, pad_r], axis=1)[:, None, :]
    return dict(
        wgu=ffn_gu, wd=ffn_down, w_in=w_in_p, w_out=w_out.astype(BF16),
        phi_big=phi_big, wbd=wbd, gate_b=gate_b, a_log_row=a_log_row, dt_bias_row=dt_bias_row)


def _trunk_layer(x, b, t, q0, past, wts, layer, alpha, ffn_bf16=None):
    ln_g, ln_b = wts["ln_g"], wts["ln_b"]
    emit = ffn_bf16 is None
    ffn_w = [(wts["wgu"], wts["wd"])] * 2 if emit else ffn_bf16
    x = _ffn_ln(x, ffn_w[0], ln_g, ln_b, layer=layer, which=0, ln_idx=0, alpha=alpha, emit_bf16=emit)
    if emit:
        x, emitted0 = x
    h = _matmul(x, wts["w_in"], layer=layer, name="in_proj")
    h3 = h.reshape(b, t, P_IN_PAD)

    sb_rows = h3[:, :, C_SBK:C_SBK + 2 * GW]
    nsa_rows = h3[:, :, C_NSAKV:C_NSAKV + GW]
    nsa_win_new = h3[:, :, C_NSAW:C_NSAW + 2 * LANES]

    if past["page_table"] is None:
        lw = 0
        o_a = _sb_attend(h3, C_SBQ // GW, h3, C_SBK // GW, h3, C_SBV // GW, q0=q0)
        cmp_rows, cmp_blk = h3.reshape(b, t // CMP_BLOCK, CMP_BLOCK, P_IN_PAD), C_NSAKV // (2 * LANES)
        win_all, win_blk = h3, C_NSAW // (2 * LANES)
        paged = None
    else:
        pt = past["page_table"]
        lw = past["nsa_win"].shape[2]
        o_a = _sb_attend_paged(h3, C_SBQ // GW, C_SBK // (2 * GW), past["sb_kv"], pt, layer=layer)
        cmp_rows = _gather_nsa_pages(past["nsa_kv"], pt, h3, C_NSAKV // (2 * LANES), layer=layer)
        cmp_blk = 0
        win_all, win_blk = jnp.concatenate([past["nsa_win"][layer], nsa_win_new], axis=1), 0
        paged = (past["nsa_kv"], pt, layer)
    ks_blk = (C_NSAKV + 2 * LANES) // (2 * LANES)

    kc = _nsa_compress(cmp_rows, cmp_blk, wts["phi_big"], layer=layer)
    nb_arr = kc.shape[1]
    nbp = -(-nb_arr // LANES) * LANES
    kc = jnp.pad(kc, ((0, 0), (0, nbp - nb_arr), (0, 0)))
    o_b = _nsa_attend(h3, C_NSAQ // GW, h3, C_SMALL // LANES, kc, h3, ks_blk, win_all, win_blk,
                      q0=q0, lw=lw, n_real_keys=q0 + t, paged=paged)
    n_win = min(WINDOW, lw + t)
    new_win = win_all[:, win_all.shape[1] - n_win:, win_blk * 2 * LANES:(win_blk + 1) * 2 * LANES]

    o_c, h_last, new_lru_buf = _rg_lru(h3, past["lru_conv"], past["lru_h"], wts["lru_conv_w"], wts["lru_conv_b"],
                                       wts["wbd"], wts["gate_b"], wts["lru_lambda"], layer=layer)
    o_d, s_new, new_gdn_buf = _gated_deltanet(h3, past["gdn_conv"], past["gdn_S"], wts["gdn_conv_w"],
                                              wts["a_log_row"], wts["dt_bias_row"], wts["gdn_norm_g"], layer=layer)

    n = b * t
    x = _outproj_ln(o_a.reshape(n, GW), o_b.reshape(n, GW), o_c.reshape(n, GW), o_d.reshape(n, GW), x,
                    wts["w_out"], wts["grp_norm_g"], ln_g, ln_b, layer=layer, alpha=alpha)
    x = _ffn_ln(x, ffn_w[1], ln_g, ln_b, layer=layer, which=1, ln_idx=2, alpha=alpha, emit_bf16=emit)
    if emit:
        x, emitted1 = x
    new_state = (
        sb_rows.reshape(b, t, 2, SB_H, HD),
        nsa_rows.reshape(b, t, 4, NSA_G, HD),
        new_win.reshape(b, new_win.shape[1], 2, NSA_G, HD),
        h_last.reshape(b, GW),
        new_lru_buf,
        s_new,
        new_gdn_buf,
    )
    return x, new_state, ([emitted0, emitted1] if emit else None)


def kernel(x_prompt, x_sample, cache_sb_kv, cache_nsa_kv, cache_nsa_win, state_lru_h, state_lru_conv,
           state_gdn_S, state_gdn_conv, page_table, ln_g, ln_b, ffn_gu, ffn_down, w_in, w_out, grp_norm_g,
           nsa_phi, lru_conv_w, lru_conv_b, lru_gate_w, lru_gate_b, lru_lambda, gdn_conv_w, gdn_A_log,
           gdn_dt_bias, gdn_norm_g):
    depth, d_model = w_in.shape[0], w_in.shape[1]
    alpha = (2 * depth) ** 0.25
    n_b, seq = x_prompt.shape[:2]
    n_db, dec_seq = x_sample.shape[:2]
    past_len = page_table.shape[1] * PAGE
    n_pool = cache_sb_kv.shape[1]

    wts = _prep_weights(ffn_gu, ffn_down, w_in, w_out, nsa_phi, lru_gate_w, lru_gate_b, gdn_A_log, gdn_dt_bias)
    wts.update(
        ln_g=ln_g.reshape(depth, 3, 1, d_model), ln_b=ln_b.reshape(depth, 3, 1, d_model),
        grp_norm_g=grp_norm_g, lru_conv_w=lru_conv_w, lru_conv_b=lru_conv_b.reshape(depth, 1, GW),
        lru_lambda=lru_lambda.reshape(depth, 1, GW), gdn_conv_w=gdn_conv_w,
        gdn_norm_g=gdn_norm_g.reshape(depth, 1, GDN_D))

    sb_cache = jnp.transpose(cache_sb_kv, (0, 1, 3, 4, 5, 2))
    nsa_cache = jnp.transpose(cache_nsa_kv, (0, 1, 3, 4, 5, 2))
    nsa_win = cache_nsa_win.reshape(depth, n_db, cache_nsa_win.shape[2], 2 * LANES)

    y_p = x_prompt.reshape(n_b * seq, d_model)
    y_s = x_sample.reshape(n_db * dec_seq, d_model)
    st_p, st_s = [], []
    for l in range(depth):
        past_p = dict(page_table=None,
                      lru_h=jnp.zeros((n_b, 1, GW), F32), lru_conv=jnp.zeros((n_b, CONV_W - 1, GW), F32),
                      gdn_S=jnp.zeros((n_b, GDN_H, GDN_D, GDN_D), F32),
                      gdn_conv=jnp.zeros((n_b, CONV_W - 1, 3 * GW), F32))
        past_s = dict(page_table=page_table, sb_kv=sb_cache, nsa_kv=nsa_cache, nsa_win=nsa_win,
                      lru_h=state_lru_h[l].reshape(n_db, 1, GW), lru_conv=state_lru_conv[l],
                      gdn_S=state_gdn_S[l], gdn_conv=state_gdn_conv[l])
        y_s, new_s, ffn_bf16 = _trunk_layer(y_s, n_db, dec_seq, past_len, past_s, wts, l, alpha)
        y_p, new_p, _ = _trunk_layer(y_p, n_b, seq, 0, past_p, wts, l, alpha, ffn_bf16=ffn_bf16)
        st_p.append(new_p)
        st_s.append(new_s)
    p = [jnp.stack(a) for a in zip(*st_p)]
    s = [jnp.stack(a) for a in zip(*st_s)]
    return (y_p.reshape(n_b, seq, d_model), y_s.reshape(n_db, dec_seq, d_model),
            p[0], p[1], p[2], p[3], p[4], p[5], p[6], s[0], s[1], s[2], s[3], s[4], s[5], s[6])
```

```python
import functools
import math

import jax
import jax.numpy as jnp
from jax import lax
from jax.experimental import pallas as pl
from jax.experimental.pallas import tpu as pltpu

F32 = jnp.float32
BF16 = jnp.bfloat16

PAGE = 128
GW = 512
HD = 64
SB_H = 8
NSA_G = 2
NSA_R = 4
CMP_BLOCK = 64
CMP_PITCH = 72
SEL_TOPN = 16
SEL_FORCE = float(NSA_R + 1)
WINDOW = 512
LRU_BLKS = 8
LRU_C = 8.0
CONV_W = 4
GDN_H = 4
GDN_D = 128
GDN_CHUNK = 64
LN_EPS = 1e-5
RMS_EPS = 1e-6
NEG = -1e30
LOG2E = 1.4426950408889634
QSCALE2 = (HD ** -0.5) * LOG2E

LANES = 128
SUBLANES = 8
VMEM_LIMIT = 56 * 1024 * 1024

C_SBK, C_SBV, C_SBQ, C_NSAQ, C_NSAKV = 0, 512, 1024, 1536, 2048
C_LRUX, C_LRUG, C_GZ, C_NSAW, C_SMALL, C_GQKV = 2560, 3072, 3584, 4096, 4352, 4608
P_IN_PAD = 6144
SM_GATE, SM_BETA, SM_DECAY = 0, 24, 28
CONV_PAD = 8
PAGES_PER_STEP = 16


def _pick(n, pref, mult=SUBLANES):
    if n <= pref:
        return n
    for t in range(pref, 0, -1):
        if n % t == 0 and t % mult == 0:
            return t
    return n


def _cparams(sem):
    return pltpu.CompilerParams(dimension_semantics=sem, vmem_limit_bytes=VMEM_LIMIT)


def _sigmoid(x):
    return 1.0 / (1.0 + jnp.exp(-x))


def _silu(x):
    return x * _sigmoid(x)


def _softplus(x):
    return jnp.maximum(x, 0.0) + jnp.log1p(jnp.exp(-jnp.abs(x)))


def _gelu_tanh(x):
    return 0.5 * x * (1.0 + jnp.tanh(math.sqrt(2.0 / math.pi) * (x + 0.044715 * (x * x * x))))


def _layer_norm(y, g, b):
    mu = jnp.mean(y, axis=-1, keepdims=True)
    d = y - mu
    var = jnp.mean(d * d, axis=-1, keepdims=True)
    return d * lax.rsqrt(var + LN_EPS) * g + b


def _dot(a, b):
    return jnp.dot(a, b, preferred_element_type=F32)


def _dot_nt(a, b):
    return lax.dot_general(a, b, (((1,), (1,)), ((), ())), preferred_element_type=F32)


def _dot_tn(a, b):
    return lax.dot_general(a, b, (((0,), (0,)), ((), ())), preferred_element_type=F32)


def _split_bf16(x):
    hi = x.astype(BF16)
    return hi, (x - hi.astype(F32)).astype(BF16)


def _dot_hp(a, b):
    ah, al = _split_bf16(a)
    bh, bl = _split_bf16(b)
    return _dot(jnp.concatenate([ah, ah, al], axis=1), jnp.concatenate([bh, bl, bh], axis=0))


def _dot_hp_exact_lhs(a_bf16, b):
    bh, bl = _split_bf16(b)
    return _dot(jnp.concatenate([a_bf16, a_bf16], axis=1), jnp.concatenate([bh, bl], axis=0))


def _dot_hp_exact_rhs(a, b_bf16):
    ah, al = _split_bf16(a)
    return _dot(jnp.concatenate([ah, al], axis=1), jnp.concatenate([b_bf16, b_bf16], axis=0))


def _ffn_ln_kernel(x_ref, wg_ref, wu_ref, wd_ref, g_ref, b_ref, o_ref, *rest, alpha, emit_bf16):
    xb_ref = rest[-1]
    j = pl.program_id(1)

    @pl.when(j == 0)
    def _():
        xb_ref[...] = x_ref[...].astype(BF16)
        o_ref[...] = jnp.zeros_like(o_ref)

    xb = xb_ref[...]
    wg, wu, wd = wg_ref[...].astype(BF16), wu_ref[...].astype(BF16), wd_ref[...].astype(BF16)
    if emit_bf16:
        rest[0][...], rest[1][...], rest[2][...] = wg, wu, wd
    act = (_silu(_dot(xb, wg)) * _dot(xb, wu)).astype(BF16)
    o_ref[...] += _dot(act, wd)

    @pl.when(j == pl.num_programs(1) - 1)
    def _():
        y = alpha * x_ref[...] + 0.5 * o_ref[...]
        o_ref[...] = _layer_norm(y, g_ref[...], b_ref[...])


def _ffn_ln(x, weights, ln_g, ln_b, *, layer, which, ln_idx, alpha, emit_bf16=False):
    n, d = x.shape
    tm = _pick(n, 1024)
    if len(weights) == 2:
        wgu, wdn = weights
        f = wdn.shape[2]
        tf = _pick(f, 256, LANES)
        nf = f // tf
        w_args = (wgu, wgu, wdn)
        w_specs = [
            pl.BlockSpec((None, None, d, tf), lambda i, j: (layer, which, 0, j)),
            pl.BlockSpec((None, None, d, tf), lambda i, j: (layer, which, 0, j + nf)),
            pl.BlockSpec((None, None, tf, d), lambda i, j: (layer, which, j, 0)),
        ]
    else:
        w_args = weights
        f = weights[2].shape[0]
        tf = _pick(f, 256, LANES)
        nf = f // tf
        w_specs = [
            pl.BlockSpec((d, tf), lambda i, j: (0, j)),
            pl.BlockSpec((d, tf), lambda i, j: (0, j)),
            pl.BlockSpec((tf, d), lambda i, j: (j, 0)),
        ]
    out_specs = [pl.BlockSpec((tm, d), lambda i, j: (i, 0))]
    out_shape = [jax.ShapeDtypeStruct((n, d), F32)]
    if emit_bf16:
        assert n == tm, "weight tiles are written once only when there is a single row block"
        out_specs += [pl.BlockSpec((d, tf), lambda i, j: (0, j)), pl.BlockSpec((d, tf), lambda i, j: (0, j)),
                      pl.BlockSpec((tf, d), lambda i, j: (j, 0))]
        out_shape += [jax.ShapeDtypeStruct((d, f), BF16), jax.ShapeDtypeStruct((d, f), BF16),
                      jax.ShapeDtypeStruct((f, d), BF16)]
    outs = pl.pallas_call(
        functools.partial(_ffn_ln_kernel, alpha=alpha, emit_bf16=emit_bf16),
        grid=(n // tm, nf),
        in_specs=[
            pl.BlockSpec((tm, d), lambda i, j: (i, 0)),
            *w_specs,
            pl.BlockSpec((None, None, 1, d), lambda i, j: (layer, ln_idx, 0, 0)),
            pl.BlockSpec((None, None, 1, d), lambda i, j: (layer, ln_idx, 0, 0)),
        ],
        out_specs=out_specs,
        out_shape=out_shape,
        scratch_shapes=[pltpu.VMEM((tm, d), BF16)],
        compiler_params=_cparams(("parallel", "arbitrary")),
        name="ffn_ln",
    )(x, *w_args, ln_g, ln_b)
    return (outs[0], tuple(outs[1:])) if emit_bf16 else outs[0]


def _matmul_kernel(x_ref, w_ref, o_ref, *lead_ref, n_lead_tiles):
    j = pl.program_id(1)
    k = pl.program_id(2)

    @pl.when(k == 0)
    def _():
        o_ref[...] = jnp.zeros_like(o_ref)

    o_ref[...] += _dot(x_ref[...].astype(BF16), w_ref[...])

    if n_lead_tiles:
        @pl.when((j < n_lead_tiles) & (k == pl.num_programs(2) - 1))
        def _():
            lead_ref[0][...] = o_ref[...]


def _matmul(x, w, *, layer, tm_pref=1024, tn_pref=512, tk_pref=2048, lead_cols=0, name="matmul"):
    m, kdim = x.shape
    nout = w.shape[2]
    tm = _pick(m, tm_pref)
    tn = _pick(nout, tn_pref, LANES)
    tk = _pick(kdim, tk_pref, LANES)
    n_lead_tiles = lead_cols // tn
    assert n_lead_tiles * tn == lead_cols
    out_specs = [pl.BlockSpec((tm, tn), lambda i, j, k: (i, j))]
    out_shape = [jax.ShapeDtypeStruct((m, nout), F32)]
    if n_lead_tiles:
        out_specs.append(pl.BlockSpec((tm, tn), lambda i, j, k: (i, jnp.minimum(j, n_lead_tiles - 1))))
        out_shape.append(jax.ShapeDtypeStruct((m, lead_cols), F32))
    outs = pl.pallas_call(
        functools.partial(_matmul_kernel, n_lead_tiles=n_lead_tiles),
        grid=(m // tm, nout // tn, kdim // tk),
        in_specs=[
            pl.BlockSpec((tm, tk), lambda i, j, k: (i, k)),
            pl.BlockSpec((None, tk, tn), lambda i, j, k: (layer, k, j)),
        ],
        out_specs=out_specs,
        out_shape=out_shape,
        compiler_params=_cparams(("parallel", "arbitrary" if n_lead_tiles else "parallel", "arbitrary")),
        name=name,
    )(x, w)
    return outs if n_lead_tiles else outs[0]


def _gather_nsa_kernel(pt_ref, *refs, n_group, t_new):
    del pt_ref
    page_refs = refs[:n_group]
    new_ref, cmp_ref = refs[n_group:]
    s = pl.program_id(1)
    last = pl.num_programs(1) - 1
    blocks_per_page = PAGE // CMP_BLOCK

    @pl.when(s < last)
    def _():
        cmp_ref[:, CMP_BLOCK:CMP_PITCH, :] = jnp.zeros((cmp_ref.shape[0], CMP_PITCH - CMP_BLOCK, 2 * LANES), F32)
        for g in range(n_group):
            for kind in range(2):
                x = page_refs[g][kind].reshape(NSA_G * HD, PAGE).T
                cols = slice(kind * LANES, (kind + 1) * LANES)
                for nl in range(blocks_per_page):
                    cmp_ref[g * blocks_per_page + nl, 0:CMP_BLOCK, cols] = x[nl * CMP_BLOCK:(nl + 1) * CMP_BLOCK]

    @pl.when(s == last)
    def _():
        cmp_ref[...] = jnp.zeros_like(cmp_ref)
        cmp_ref[0, 0:t_new, :] = new_ref[...]


def _gather_nsa_pages(cache_t, page_table, new_arr, new_colblk, *, layer):
    b, n_pages = page_table.shape
    t_new = new_arr.shape[1]
    assert t_new <= CMP_BLOCK
    n_group = math.gcd(n_pages, PAGES_PER_STEP)
    n_steps = n_pages // n_group + 1
    blocks = n_group * PAGE // CMP_BLOCK

    def page_map(g):
        return lambda bi, s, pt: (layer, pt[bi, jnp.minimum(s * n_group + g, n_pages - 1)], 0, 0, 0, 0)

    in_specs = [pl.BlockSpec((None, None, 2, NSA_G, HD, PAGE), page_map(g)) for g in range(n_group)]
    in_specs.append(pl.BlockSpec((None, t_new, 2 * LANES), lambda bi, s, pt: (bi, 0, new_colblk)))
    return pl.pallas_call(
        functools.partial(_gather_nsa_kernel, n_group=n_group, t_new=t_new),
        grid_spec=pltpu.PrefetchScalarGridSpec(
            num_scalar_prefetch=1, grid=(b, n_steps), in_specs=in_specs,
            out_specs=pl.BlockSpec((None, blocks, CMP_PITCH, 2 * LANES), lambda bi, s, pt: (bi, s, 0, 0))),
        out_shape=jax.ShapeDtypeStruct((b, n_steps * blocks, CMP_PITCH, 2 * LANES), F32),
        compiler_params=_cparams(("parallel", "arbitrary")),
        name="gather_nsa_pages",
    )(page_table, *([cache_t] * n_group), new_arr)


def _suffix_matrix(n):
    j = lax.broadcasted_iota(jnp.int32, (2 * n, n), 0)
    s = lax.broadcasted_iota(jnp.int32, (2 * n, n), 1)
    return ((j > s) & ((j < n) | (j - n > s))).astype(BF16)


def _sb_neg_log(z2, mask):
    nl = jnp.maximum(z2, 0.0) + jnp.log2(1.0 + jnp.exp2(-jnp.abs(z2)))
    if mask is not None:
        nl = jnp.where(mask, nl, 0.0)
    hi = nl.astype(BF16)
    lo = (nl - hi.astype(F32)).astype(BF16)
    return nl, jnp.concatenate([hi, lo], axis=1)


def _sb_weight(z2, nl, tail, carry, mask):
    w = jnp.exp2(z2 - nl - tail - carry)
    if mask is not None:
        w = jnp.where(mask, w, 0.0)
    return w.astype(BF16)


def _sb_weights(z2, tri2, carry, mask):
    nl, hilo = _sb_neg_log(z2, mask)
    tail = _dot(hilo, tri2)
    return _sb_weight(z2, nl, tail, carry, mask), tail[:, 0:1] + nl[:, 0:1]


def _sb_kernel(qi_ref, ki_ref, q_ref, k_ref, v_ref, o_ref, acc_ref, c_ref, *, tq, tk, q0):
    p = pl.program_id(1)
    i = qi_ref[p]
    jt = ki_ref[p]
    qs0 = q0 + i * tq

    @pl.when(jt == (qs0 + tq - 2) // tk)
    def _():
        acc_ref[...] = jnp.zeros_like(acc_ref)
        c_ref[...] = jnp.zeros_like(c_ref)

    def tile(masked):
        mask = None
        if masked:
            qpos = qs0 + lax.broadcasted_iota(jnp.int32, (tq, tk), 0)
            kpos = jt * tk + lax.broadcasted_iota(jnp.int32, (tq, tk), 1)
            mask = kpos < qpos
        tri = _suffix_matrix(tk)
        lane_half = lax.broadcasted_iota(jnp.int32, (tq, LANES), 1) // HD
        cols = [slice((h // 2) * LANES, (h // 2 + 1) * LANES) for h in range(SB_H)]
        zs = []
        for h in range(SB_H):
            qc = jnp.where(lane_half == h % 2, q_ref[:, cols[h]] * QSCALE2, 0.0).astype(BF16)
            zs.append(_dot_nt(qc, k_ref[:, cols[h]].astype(BF16)))
        nls, hilos = [], []
        for h in range(SB_H):
            nl, hilo = _sb_neg_log(zs[h], mask)
            nls.append(nl)
            hilos.append(hilo)
        tails = [_dot(hilos[h], tri) for h in range(SB_H)]
        ws = []
        for h in range(SB_H):
            ws.append(_sb_weight(zs[h], nls[h], tails[h], c_ref[h], mask))
            c_ref[h] += tails[h][:, 0:1] + nls[h][:, 0:1]
        for h in range(SB_H):
            acc_ref[h] += _dot(ws[h], v_ref[:, cols[h]].astype(BF16))

    whole = (jt + 1) * tk <= qs0

    @pl.when(whole)
    def _():
        tile(False)

    @pl.when(jnp.logical_not(whole))
    def _():
        tile(True)

    @pl.when(jt == 0)
    def _():
        lane_half = lax.broadcasted_iota(jnp.int32, (tq, LANES), 1) // HD
        for c in range(SB_H // 2):
            o_ref[:, c * LANES:(c + 1) * LANES] = jnp.where(lane_half == 0, acc_ref[2 * c], acc_ref[2 * c + 1])


def _sb_attend(q_arr, q_blk, k_arr, k_blk, v_arr, v_blk, *, q0):
    b, t = q_arr.shape[:2]
    l = k_arr.shape[1]
    tq = _pick(t, 256)
    tk = _pick(l, 256, LANES)
    pairs = [(i, jt) for i in range(t // tq) for jt in range((q0 + (i + 1) * tq - 2) // tk, -1, -1)]
    qi = jnp.asarray([pr[0] for pr in pairs], jnp.int32)
    ki = jnp.asarray([pr[1] for pr in pairs], jnp.int32)
    return pl.pallas_call(
        functools.partial(_sb_kernel, tq=tq, tk=tk, q0=q0),
        grid_spec=pltpu.PrefetchScalarGridSpec(
            num_scalar_prefetch=2, grid=(b, len(pairs)),
            in_specs=[
                pl.BlockSpec((None, tq, GW), lambda bi, p, qi_r, ki_r: (bi, qi_r[p], q_blk)),
                pl.BlockSpec((None, tk, GW), lambda bi, p, qi_r, ki_r: (bi, ki_r[p], k_blk)),
                pl.BlockSpec((None, tk, GW), lambda bi, p, qi_r, ki_r: (bi, ki_r[p], v_blk)),
            ],
            out_specs=pl.BlockSpec((None, tq, GW), lambda bi, p, qi_r, ki_r: (bi, qi_r[p], 0)),
            scratch_shapes=[pltpu.VMEM((SB_H, tq, LANES), F32), pltpu.VMEM((SB_H, tq, 1), F32)]),
        out_shape=jax.ShapeDtypeStruct((b, t, GW), F32),
        compiler_params=_cparams(("parallel", "arbitrary")),
        name="sb_attend",
    )(qi, ki, q_arr, k_arr, v_arr)


def _sb_paged_kernel(pt_ref, q_ref, new_ref, *refs, n_group, t):
    del pt_ref
    page_refs = refs[:n_group]
    o_ref, qbd_ref, acc_ref, c_ref = refs[n_group:]
    s = pl.program_id(1)
    m = SB_H * t
    tri = _suffix_matrix(PAGE)

    @pl.when(s == 0)
    def _():
        lane_head = lax.broadcasted_iota(jnp.int32, (t, GW), 1) // HD
        q = q_ref[...] * QSCALE2
        for h in range(SB_H):
            qbd_ref[h * t:(h + 1) * t, :] = jnp.where(lane_head == h, q, 0.0)
        pad = jnp.zeros((PAGE - t, GW), F32)
        k_new = jnp.concatenate([new_ref[:, 0:GW], pad], axis=0).astype(BF16)
        v_new = jnp.concatenate([new_ref[:, GW:2 * GW], pad], axis=0).astype(BF16)
        row = lax.broadcasted_iota(jnp.int32, (m, PAGE), 0)
        mask = lax.broadcasted_iota(jnp.int32, (m, PAGE), 1) < row - (row // t) * t
        w, rs = _sb_weights(_dot_nt(qbd_ref[...].astype(BF16), k_new), tri, 0.0, mask)
        acc_ref[...] = _dot(w, v_new)
        c_ref[...] = rs

    qb = qbd_ref[...].astype(BF16)
    pages = range(n_group)
    zs = [_dot(qb, page_refs[g][0].reshape(GW, PAGE).astype(BF16)) for g in pages]
    nls, hilos = zip(*[_sb_neg_log(zs[g], None) for g in pages])
    tails = [_dot(hilos[g], tri) for g in pages]
    carries = [None] * n_group
    carry = c_ref[...]
    for g in reversed(pages):
        carries[g] = carry
        carry = carry + tails[g][:, 0:1] + nls[g][:, 0:1]
    c_ref[...] = carry
    ws = [_sb_weight(zs[g], nls[g], tails[g], carries[g], None) for g in pages]
    acc = acc_ref[...]
    for g in pages:
        acc = acc + _dot_nt(ws[g], page_refs[g][1].reshape(GW, PAGE).astype(BF16))
    acc_ref[...] = acc

    @pl.when(s == pl.num_programs(1) - 1)
    def _():
        lane_head = lax.broadcasted_iota(jnp.int32, (t, GW), 1) // HD
        out = jnp.zeros((t, GW), F32)
        for h in range(SB_H):
            out = out + jnp.where(lane_head == h, acc_ref[h * t:(h + 1) * t, :], 0.0)
        o_ref[...] = out


def _sb_attend_paged(q_arr, q_blk, new_blk, cache_t, page_table, *, layer):
    b, t = q_arr.shape[:2]
    n_pages = page_table.shape[1]
    n_group = math.gcd(n_pages, PAGES_PER_STEP)
    n_steps = n_pages // n_group
    m = SB_H * t

    def page_map(g):
        return lambda bi, s, pt: (layer, pt[bi, n_pages - (s + 1) * n_group + g], 0, 0, 0, 0)

    in_specs = [
        pl.BlockSpec((None, t, GW), lambda bi, s, pt: (bi, 0, q_blk)),
        pl.BlockSpec((None, t, 2 * GW), lambda bi, s, pt: (bi, 0, new_blk)),
    ] + [pl.BlockSpec((None, None, 2, SB_H, HD, PAGE), page_map(g)) for g in range(n_group)]
    return pl.pallas_call(
        functools.partial(_sb_paged_kernel, n_group=n_group, t=t),
        grid_spec=pltpu.PrefetchScalarGridSpec(
            num_scalar_prefetch=1, grid=(b, n_steps), in_specs=in_specs,
            out_specs=pl.BlockSpec((None, t, GW), lambda bi, s, pt: (bi, 0, 0)),
            scratch_shapes=[pltpu.VMEM((m, GW), F32), pltpu.VMEM((m, GW), F32), pltpu.VMEM((m, 1), F32)]),
        out_shape=jax.ShapeDtypeStruct((b, t, GW), F32),
        compiler_params=_cparams(("parallel", "arbitrary")),
        name="sb_attend_paged",
    )(page_table, q_arr, q_arr, *([cache_t] * n_group))


def _masked_softmax(s2, mask):
    sm = jnp.where(mask, s2, NEG)
    m = jnp.max(sm, axis=-1, keepdims=True)
    e = jnp.where(mask, jnp.exp2(sm - m), 0.0)
    return e / jnp.maximum(jnp.sum(e, axis=-1, keepdims=True), 1e-30)


def _compress_kernel(xk_ref, xv_ref, w_ref, o_ref, *, pitch):
    width = 2 * LANES
    n = o_ref.shape[0]
    parts = []
    for j in range(CMP_BLOCK):
        rows = pl.ds(j, n, stride=pitch)
        xj = jnp.concatenate([xk_ref[rows, :], xv_ref[rows, :]], axis=1).astype(BF16)
        parts.append(_dot(xj, w_ref[j * width:(j + 1) * width, :]))
    while len(parts) > 1:
        parts = [parts[i] + parts[i + 1] for i in range(0, len(parts), 2)]
    o_ref[...] = parts[0]


def _nsa_compress(x_arr, x_blk, phi_big, *, layer):
    b, nb, pitch = x_arr.shape[:3]
    width = 2 * LANES
    tn = _pick(nb, 160)
    return pl.pallas_call(
        functools.partial(_compress_kernel, pitch=pitch),
        grid=(b, nb // tn),
        in_specs=[
            pl.BlockSpec((None, tn * pitch, LANES), lambda bi, i: (bi, i, 2 * x_blk)),
            pl.BlockSpec((None, tn * pitch, LANES), lambda bi, i: (bi, i, 2 * x_blk + 1)),
            pl.BlockSpec((None, CMP_BLOCK * width, width), lambda bi, i: (layer, 0, 0)),
        ],
        out_specs=pl.BlockSpec((None, tn, width), lambda bi, i: (bi, i, 0)),
        out_shape=jax.ShapeDtypeStruct((b, nb, width), F32),
        compiler_params=_cparams(("parallel", "parallel")),
        name="nsa_compress",
    )(*([x_arr.reshape(b, nb * pitch, x_arr.shape[3])] * 2), phi_big)


def _select_blocks(imp, blk, nb, n_sel):
    tq, nbp = imp.shape
    if tq % LANES != 0:
        rank = jnp.zeros((tq, nbp), F32)
        for mblk in range(nb):
            col = imp[:, mblk:mblk + 1]
            beats = (col > imp) | ((col == imp) & (blk > mblk))
            rank = rank + jnp.where(beats, 1.0, 0.0)
        return jnp.where(rank < n_sel, 1.0, 0.0)
    imp_t = imp.T
    n_parts = -(-nb // SUBLANES)
    parts = [imp_t[p * SUBLANES:(p + 1) * SUBLANES, :] for p in range(n_parts)]
    row_in_part = lax.broadcasted_iota(jnp.int32, (SUBLANES, tq), 0)
    ranks = [jnp.zeros((SUBLANES, tq), F32) for _ in range(n_parts)]
    for mblk in range(nb):
        pm, rm = divmod(mblk, SUBLANES)
        row = parts[pm][rm:rm + 1, :]
        for p in range(n_parts):
            if p < pm:
                beats = row > parts[p]
            elif p > pm:
                beats = row >= parts[p]
            else:
                beats = (row > parts[p]) | ((row == parts[p]) & (row_in_part > rm))
            ranks[p] = ranks[p] + jnp.where(beats, 1.0, 0.0)
    sel_t = [jnp.where(r < n_sel, 1.0, 0.0) for r in ranks]
    if nbp > n_parts * SUBLANES:
        sel_t.append(jnp.zeros((nbp - n_parts * SUBLANES, tq), F32))
    return jnp.concatenate(sel_t, axis=0).T


def _nsa_kernel(*refs, n_page_refs, tq, tk, q0, lw, nb, nbp, n_sel, wl):
    n_prefetch = 3 if n_page_refs else 2
    qi_ref, ki_ref = refs[0], refs[1]
    q_ref, sm_ref, kc_ref = refs[n_prefetch:n_prefetch + 3]
    key_refs = refs[n_prefetch + 3:n_prefetch + 4 + n_page_refs]
    win_ref, o_ref, qs_ref, sel_ref, part_ref, m_ref, l_ref, acc_ref = refs[n_prefetch + 4 + n_page_refs:]
    p = pl.program_id(1)
    i = qi_ref[p]
    j = ki_ref[p]
    rq = NSA_R * tq
    qs0 = q0 + i * tq
    j_last = (qs0 + tq - 1) // tk
    scale = QSCALE2

    @pl.when(j == 0)
    def _():
        lane_half = lax.broadcasted_iota(jnp.int32, (tq, LANES), 1) // HD
        gates = _sigmoid(sm_ref[...])
        qpos_b = qs0 + lax.broadcasted_iota(jnp.int32, (tq, nbp), 0)
        blk = lax.broadcasted_iota(jnp.int32, (tq, nbp), 1)
        cmask = ((blk + 1) * CMP_BLOCK - 1 <= qpos_b) & (blk < nb)
        cur = qpos_b // CMP_BLOCK
        forced = (blk == 0) | (blk == cur)
        valid = blk <= cur
        start = pl.multiple_of(jnp.maximum(lw + (i + 1) * tq - wl, 0), SUBLANES)
        qpos_w = qs0 + lax.broadcasted_iota(jnp.int32, (tq, wl), 0)
        kpos_w = q0 - lw + start + lax.broadcasted_iota(jnp.int32, (tq, wl), 1)
        wmask = (kpos_w <= qpos_w) & (kpos_w > qpos_w - WINDOW)
        kw = win_ref[pl.ds(start, wl), 0:LANES].astype(BF16)
        vw = win_ref[pl.ds(start, wl), LANES:2 * LANES].astype(BF16)
        kc = kc_ref[:, 0:LANES].astype(BF16)
        vc = kc_ref[:, LANES:2 * LANES].astype(BF16)
        groups = range(NSA_G)
        for g in groups:
            for r in range(NSA_R):
                chunk, half = 2 * g + r // 2, r % 2
                qc = q_ref[:, chunk * LANES:(chunk + 1) * LANES]
                if half != g:
                    qc = pltpu.roll(qc, HD, 1)
                qs_ref[g, r * tq:(r + 1) * tq, :] = jnp.where(lane_half == g, qc * scale, 0.0)
        qs = [qs_ref[g].astype(BF16) for g in groups]
        s_c = [_dot_nt(qs[g], kc).reshape(NSA_R, tq, nbp) for g in groups]
        s_w = [_dot_nt(qs[g], kw).reshape(NSA_R, tq, wl) for g in groups]
        p_c = [_masked_softmax(s_c[g], cmask[None]) for g in groups]
        p_w = [_masked_softmax(s_w[g], wmask[None]) for g in groups]
        o_c = [_dot(p_c[g].reshape(rq, nbp).astype(BF16), vc) for g in groups]
        o_w = [_dot(p_w[g].reshape(rq, wl).astype(BF16), vw) for g in groups]
        for g in groups:
            imp = jnp.sum(p_c[g], axis=0)
            imp = jnp.where(forced, SEL_FORCE, jnp.where(valid, imp, -1.0))
            imp = jnp.where(blk < nb, imp, -2.0)
            sel_ref[g] = _select_blocks(imp, blk, nb, n_sel)
            for r in range(NSA_R):
                rows = slice(r * tq, (r + 1) * tq)
                lane_c = SM_GATE + 0 * NSA_G * NSA_R + g * NSA_R + r
                lane_w = SM_GATE + 2 * NSA_G * NSA_R + g * NSA_R + r
                part_ref[g, rows, :] = (gates[:, lane_c:lane_c + 1] * o_c[g][rows]
                                        + gates[:, lane_w:lane_w + 1] * o_w[g][rows])
        m_ref[...] = jnp.full_like(m_ref, NEG)
        l_ref[...] = jnp.zeros_like(l_ref)
        acc_ref[...] = jnp.zeros_like(acc_ref)

    slab = tq if tq >= LANES else rq
    pieces = [(g, slice(r0, r0 + slab)) for g in range(NSA_G) for r0 in range(0, rq, slab)]

    def sel_tile(n, kpos0, score_fn, pv_fn):
        kblk = (kpos0 + lax.broadcasted_iota(jnp.int32, (nbp, n), 1)) // CMP_BLOCK
        expand = (lax.broadcasted_iota(jnp.int32, (nbp, n), 0) == kblk).astype(BF16)
        qpos = qs0 + lax.broadcasted_iota(jnp.int32, (tq, n), 0)
        kpos = kpos0 + lax.broadcasted_iota(jnp.int32, (tq, n), 1)
        causal = kpos <= qpos
        biases = []
        for g in range(NSA_G):
            bias = jnp.where((_dot(sel_ref[g].astype(BF16), expand) > 0.5) & causal, 0.0, NEG)
            biases.append(bias if slab == tq else jnp.concatenate([bias] * (slab // tq), axis=0))
        scores = [score_fn(qs_ref[g, rows, :].astype(BF16)) + biases[g] for g, rows in pieces]
        probs, corrs = [], []
        for (g, rows), s in zip(pieces, scores):
            m_old = m_ref[g, rows, :]
            m_new = jnp.maximum(m_old, jnp.max(s, axis=-1, keepdims=True))
            pr = jnp.exp2(s - m_new)
            corr = jnp.exp2(m_old - m_new)
            l_ref[g, rows, :] = corr * l_ref[g, rows, :] + jnp.sum(pr, axis=-1, keepdims=True)
            m_ref[g, rows, :] = m_new
            probs.append(pr.astype(BF16))
            corrs.append(corr)
        for (g, rows), pr, corr in zip(pieces, probs, corrs):
            acc_ref[g, rows, :] = corr * acc_ref[g, rows, :] + pv_fn(pr)

    if n_page_refs == 0:
        ks = key_refs[0][:, 0:LANES].astype(BF16)
        vs = key_refs[0][:, LANES:2 * LANES].astype(BF16)
        sel_tile(tk, j * tk, lambda q: _dot_nt(q, ks), lambda pr: _dot(pr, vs))
    else:
        n_past_tiles = q0 // tk

        @pl.when(j < n_past_tiles)
        def _():
            kst = jnp.concatenate([r[0].reshape(NSA_G * HD, PAGE) for r in key_refs[:-1]], axis=1).astype(BF16)
            vst = jnp.concatenate([r[1].reshape(NSA_G * HD, PAGE) for r in key_refs[:-1]], axis=1).astype(BF16)
            sel_tile(tk, j * tk, lambda q: _dot(q, kst), lambda pr: _dot_nt(pr, vst))

        @pl.when(j == n_past_tiles)
        def _():
            pad = jnp.zeros((PAGE - tq, LANES), F32)
            ks = jnp.concatenate([key_refs[-1][:, 0:LANES], pad], axis=0).astype(BF16)
            vs = jnp.concatenate([key_refs[-1][:, LANES:2 * LANES], pad], axis=0).astype(BF16)
            sel_tile(PAGE, q0, lambda q: _dot_nt(q, ks), lambda pr: _dot(pr, vs))

    @pl.when(j == j_last)
    def _():
        gates = _sigmoid(sm_ref[...])
        lane_half = lax.broadcasted_iota(jnp.int32, (tq, LANES), 1) // HD
        for g in range(NSA_G):
            o_s = acc_ref[g] / jnp.maximum(l_ref[g], 1e-30)
            res = []
            for r in range(NSA_R):
                rows = slice(r * tq, (r + 1) * tq)
                lane_s = SM_GATE + 1 * NSA_G * NSA_R + g * NSA_R + r
                res.append(part_ref[g, rows, :] + gates[:, lane_s:lane_s + 1] * o_s[rows])
            for c2 in range(2):
                even, odd = res[2 * c2], res[2 * c2 + 1]
                if g == 1:
                    even = pltpu.roll(even, HD, 1)
                else:
                    odd = pltpu.roll(odd, HD, 1)
                chunk = 2 * g + c2
                o_ref[:, chunk * LANES:(chunk + 1) * LANES] = jnp.where(lane_half == 0, even, odd)


def _nsa_attend(q_arr, q_blk, sm_arr, sm_blk, kc_arr, ks_arr, ks_blk, win_arr, win_blk, *, q0, lw, n_real_keys,
                paged=None):
    b, t = q_arr.shape[:2]
    nbp = kc_arr.shape[1]
    nb = -(-n_real_keys // CMP_BLOCK)
    n_sel = min(SEL_TOPN, nb)
    tq = _pick(t, 128)
    wl = min(WINDOW + tq, lw + t)
    rq = NSA_R * tq
    if paged is None:
        tk = _pick(ks_arr.shape[1], 1024, LANES)
        n_page_refs, prefetch = 0, ()
        key_specs = [pl.BlockSpec((None, tk, 2 * LANES), lambda bi, p, qi_r, ki_r: (bi, ki_r[p], ks_blk))]
        key_args = (ks_arr,)
    else:
        cache_t, page_table, layer = paged
        n_pages = page_table.shape[1]
        n_page_refs = math.gcd(n_pages, PAGES_PER_STEP)
        tk = n_page_refs * PAGE
        assert tq == t and q0 == n_pages * PAGE and q0 % tk == 0 and t <= PAGE
        prefetch = (page_table,)

        def page_map(g):
            def index_map(bi, p, qi_r, ki_r, pt):
                return (layer, pt[bi, jnp.minimum(ki_r[p] * n_page_refs + g, n_pages - 1)], 1, 0, 0, 0)
            return index_map

        key_specs = [pl.BlockSpec((None, None, 2, NSA_G, HD, PAGE), page_map(g)) for g in range(n_page_refs)]
        key_specs.append(pl.BlockSpec((None, t, 2 * LANES), lambda bi, p, qi_r, ki_r, pt: (bi, 0, ks_blk)))
        key_args = (cache_t,) * n_page_refs + (ks_arr,)
    pairs = [(i, j) for i in range(t // tq) for j in range((q0 + (i + 1) * tq - 1) // tk + 1)]
    qi = jnp.asarray([pr[0] for pr in pairs], jnp.int32)
    ki = jnp.asarray([pr[1] for pr in pairs], jnp.int32)
    return pl.pallas_call(
        functools.partial(_nsa_kernel, n_page_refs=n_page_refs, tq=tq, tk=tk, q0=q0, lw=lw, nb=nb, nbp=nbp,
                          n_sel=n_sel, wl=wl),
        grid_spec=pltpu.PrefetchScalarGridSpec(
            num_scalar_prefetch=2 + len(prefetch), grid=(b, len(pairs)),
            in_specs=[
                pl.BlockSpec((None, tq, GW), lambda bi, p, qi_r, ki_r, *_: (bi, qi_r[p], q_blk)),
                pl.BlockSpec((None, tq, LANES), lambda bi, p, qi_r, ki_r, *_: (bi, qi_r[p], sm_blk)),
                pl.BlockSpec((None, nbp, 2 * LANES), lambda bi, p, qi_r, ki_r, *_: (bi, 0, 0)),
                *key_specs,
                pl.BlockSpec((None, lw + t, 2 * LANES), lambda bi, p, qi_r, ki_r, *_: (bi, 0, win_blk)),
            ],
            out_specs=pl.BlockSpec((None, tq, GW), lambda bi, p, qi_r, ki_r, *_: (bi, qi_r[p], 0)),
            scratch_shapes=[
                pltpu.VMEM((NSA_G, rq, LANES), F32),
                pltpu.VMEM((NSA_G, tq, nbp), F32),
                pltpu.VMEM((NSA_G, rq, LANES), F32),
                pltpu.VMEM((NSA_G, rq, 1), F32),
                pltpu.VMEM((NSA_G, rq, 1), F32),
                pltpu.VMEM((NSA_G, rq, LANES), F32),
            ]),
        out_shape=jax.ShapeDtypeStruct((b, t, GW), F32),
        compiler_params=_cparams(("parallel", "arbitrary")),
        name="nsa_attend",
    )(qi, ki, *prefetch, q_arr, sm_arr, kc_arr, *key_args, win_arr)


def _lru_kernel(x_ref, gt_ref, buf_ref, h0_ref, cw_ref, cb_ref, wbd_ref, gb_ref, lam_ref,
                o_ref, hl_ref, nb_ref, xp_ref, a_ref, b_ref, h_ref, *, tt):
    s = pl.program_id(1)
    hist = CONV_PAD - (CONV_W - 1)

    @pl.when(s == 0)
    def _():
        xp_ref[hist:CONV_PAD, :] = buf_ref[...]
        h_ref[...] = h0_ref[...]

    xp_ref[CONV_PAD:CONV_PAD + tt, :] = x_ref[...]
    u = cb_ref[...]
    for jw in range(CONV_W):
        u = u + cw_ref[jw:jw + 1, :] * xp_ref[hist + jw:hist + jw + tt, :]
    gts = _dot(u.astype(BF16), wbd_ref[...]) + gb_ref[...]
    r = _sigmoid(gts[:, :GW])
    ig = _sigmoid(gts[:, GW:])
    log_a = -LRU_C * r * _softplus(-lam_ref[...])
    a = jnp.exp(log_a)
    a_ref[...] = a
    b_ref[...] = jnp.sqrt(-jnp.tanh(log_a) * (a * a + 1.0)) * (ig * u)

    def step(t, h):
        h = a_ref[pl.ds(t, 1), :] * h + b_ref[pl.ds(t, 1), :]
        b_ref[pl.ds(t, 1), :] = h
        return h

    h = lax.fori_loop(0, tt, step, h_ref[...], unroll=8)
    h_ref[...] = h
    o_ref[...] = b_ref[...] * _gelu_tanh(gt_ref[...])
    tail = xp_ref[tt + hist:tt + CONV_PAD, :]
    xp_ref[hist:CONV_PAD, :] = tail

    @pl.when(s == pl.num_programs(1) - 1)
    def _():
        hl_ref[...] = h
        nb_ref[...] = tail


def _rg_lru(h_arr, buf, h0, conv_w, conv_b, wbd, gate_b, lam, *, layer):
    b, t = h_arr.shape[:2]
    tt = _pick(t, 512)
    x_blk, g_blk = C_LRUX // GW, C_LRUG // GW
    wspec = lambda shape: pl.BlockSpec((None,) + shape, lambda bi, s: (layer,) + (0,) * len(shape))
    return pl.pallas_call(
        functools.partial(_lru_kernel, tt=tt),
        grid=(b, t // tt),
        in_specs=[
            pl.BlockSpec((None, tt, GW), lambda bi, s: (bi, s, x_blk)),
            pl.BlockSpec((None, tt, GW), lambda bi, s: (bi, s, g_blk)),
            pl.BlockSpec((None, CONV_W - 1, GW), lambda bi, s: (bi, 0, 0)),
            pl.BlockSpec((None, 1, GW), lambda bi, s: (bi, 0, 0)),
            wspec((CONV_W, GW)), wspec((1, GW)), wspec((GW, 2 * GW)), wspec((1, 2 * GW)), wspec((1, GW)),
        ],
        out_specs=[
            pl.BlockSpec((None, tt, GW), lambda bi, s: (bi, s, 0)),
            pl.BlockSpec((None, 1, GW), lambda bi, s: (bi, 0, 0)),
            pl.BlockSpec((None, CONV_W - 1, GW), lambda bi, s: (bi, 0, 0)),
        ],
        out_shape=[
            jax.ShapeDtypeStruct((b, t, GW), F32),
            jax.ShapeDtypeStruct((b, 1, GW), F32),
            jax.ShapeDtypeStruct((b, CONV_W - 1, GW), F32),
        ],
        scratch_shapes=[
            pltpu.VMEM((CONV_PAD + tt, GW), F32),
            pltpu.VMEM((tt, GW), F32),
            pltpu.VMEM((tt, GW), F32),
            pltpu.VMEM((1, GW), F32),
        ],
        compiler_params=_cparams(("parallel", "arbitrary")),
        name="rg_lru",
    )(h_arr, h_arr, buf, h0, conv_w, conv_b, wbd, gate_b, lam)


def _gdn_kernel(qkv_ref, z_ref, sm_ref, buf_ref, s0_ref, cw_ref, al_ref, dtb_ref, ng_ref,
                o_ref, sout_ref, nb_ref, xp_ref, smp_ref, st_ref, *, tt, tc, nbs):
    s = pl.program_id(1)
    hist = CONV_PAD - (CONV_W - 1)
    n_chunks = tc // GDN_CHUNK

    @pl.when(s == 0)
    def _():
        xp_ref[...] = jnp.zeros_like(xp_ref)
        smp_ref[...] = jnp.zeros_like(smp_ref)
        xp_ref[:, hist:CONV_PAD, :] = buf_ref[...]
        st_ref[...] = s0_ref[...]

    xp_ref[:, CONV_PAD:CONV_PAD + tt, :] = qkv_ref[...]
    smp_ref[:, 0:tt, :] = sm_ref[...]
    row_ok = lax.broadcasted_iota(jnp.int32, (tc, 1), 0) < tt
    ri = lax.broadcasted_iota(jnp.int32, (tc, tc), 0)
    ci = lax.broadcasted_iota(jnp.int32, (tc, tc), 1)
    same = (ri // GDN_CHUNK) == (ci // GDN_CHUNK)
    incl = same & (ci <= ri)
    strict = same & (ci < ri)
    ones_where = lambda m: jnp.where(m, 1.0, 0.0).astype(BF16)
    eye = jnp.where(ri == ci, 1.0, 0.0)

    heads = range(nbs * GDN_H)
    pws, tinvs, qks, rhss, qgs, kds, gts = [], [], [], [], [], [], []
    for piece in heads:
        bb, h = divmod(piece, GDN_H)
        if h == 0:
            y = cw_ref[0:1, :] * xp_ref[bb, hist:hist + tc, :]
            for jw in range(1, CONV_W):
                y = y + cw_ref[jw:jw + 1, :] * xp_ref[bb, hist + jw:hist + jw + tc, :]
            y = jnp.where(row_ok, _silu(y), 0.0)
            sm = smp_ref[bb]
            beta_all = jnp.where(row_ok, _sigmoid(sm), 0.0)
            g_all = jnp.where(row_ok, -jnp.exp(al_ref[...]) * _softplus(sm + dtb_ref[...]), 0.0)
            gcum = _dot_hp_exact_lhs(ones_where(incl), g_all)
            gcum_t = _dot_hp_exact_rhs(g_all.T, ones_where(same & (ri <= ci)))
            gtot = _dot_hp_exact_lhs(ones_where(same), g_all)
        q = y[:, h * GDN_D:(h + 1) * GDN_D]
        k = y[:, GW + h * GDN_D:GW + (h + 1) * GDN_D]
        v = y[:, 2 * GW + h * GDN_D:2 * GW + (h + 1) * GDN_D]
        q = q * lax.rsqrt(jnp.sum(q * q, axis=-1, keepdims=True) + RMS_EPS) * (GDN_D ** -0.5)
        k = k * lax.rsqrt(jnp.sum(k * k, axis=-1, keepdims=True) + RMS_EPS)
        beta = beta_all[:, SM_BETA + h:SM_BETA + h + 1]
        gc = gcum[:, SM_DECAY + h:SM_DECAY + h + 1]
        gr = gcum_t[SM_DECAY + h:SM_DECAY + h + 1, :]
        gt = gtot[:, SM_DECAY + h:SM_DECAY + h + 1]
        decay = jnp.exp(jnp.where(incl, gc - gr, NEG))
        kb = k * beta
        kbf = k.astype(BF16)
        a_mat = jnp.where(strict, _dot_nt(kb.astype(BF16), kbf) * decay, 0.0)
        qks.append(jnp.where(incl, _dot_nt(q.astype(BF16), kbf) * decay, 0.0).astype(BF16))
        pws.append(-a_mat)
        tinvs.append(eye - a_mat)
        eg = jnp.exp(gc)
        rhss.append(jnp.concatenate([v * beta, kb * eg], axis=1))
        qgs.append((q * eg).astype(BF16))
        kds.append((k * jnp.exp(gt - gc)).astype(BF16))
        gts.append(gt)
    for _ in range(int(math.log2(GDN_CHUNK)) - 1):
        pws = [_dot_hp(pws[h], pws[h]) for h in heads]
        tinvs = [tinvs[h] + _dot_hp(tinvs[h], pws[h]) for h in heads]
    sols = [_dot_hp(tinvs[h], rhss[h]) for h in heads]
    us = [sols[h][:, :GDN_D] for h in heads]
    ws = [sols[h][:, GDN_D:].astype(BF16) for h in heads]
    sts = [st_ref[piece // GDN_H, piece % GDN_H] for piece in heads]
    outs = [[] for _ in heads]
    for c in range(n_chunks):
        rows = slice(c * GDN_CHUNK, (c + 1) * GDN_CHUNK)
        stbs = [sts[h].astype(BF16) for h in heads]
        v_news = [(us[h][rows] - _dot(ws[h][rows], stbs[h])).astype(BF16) for h in heads]
        for h in heads:
            outs[h].append(_dot(qgs[h][rows], stbs[h]) + _dot(qks[h][rows, rows], v_news[h]))
        for h in heads:
            g_last = jnp.exp(gts[h][c * GDN_CHUNK:c * GDN_CHUNK + 1, :])
            sts[h] = sts[h] * g_last + _dot_tn(kds[h][rows], v_news[h])
    for piece in heads:
        bb, h = divmod(piece, GDN_H)
        st_ref[bb, h] = sts[piece]
        o = outs[piece][0] if n_chunks == 1 else jnp.concatenate(outs[piece], axis=0)
        o = o * lax.rsqrt(jnp.mean(o * o, axis=-1, keepdims=True) + RMS_EPS) * ng_ref[...]
        o_ref[bb, :, h * GDN_D:(h + 1) * GDN_D] = o[0:tt] * _silu(z_ref[bb, :, h * GDN_D:(h + 1) * GDN_D])

    tail = xp_ref[:, tt + hist:tt + CONV_PAD, :]
    xp_ref[:, hist:CONV_PAD, :] = tail

    @pl.when(s == pl.num_programs(1) - 1)
    def _():
        sout_ref[...] = st_ref[...]
        nb_ref[...] = tail


def _gated_deltanet(h_arr, buf, s0, conv_w, a_log_row, dt_bias_row, norm_g, *, layer):
    b, t = h_arr.shape[:2]
    tt = _pick(t, 128)
    tc = -(-tt // GDN_CHUNK) * GDN_CHUNK
    assert tc == tt or t == tt, "sequence length must be a multiple of the chunk unless it fits one block"
    w3 = 3 * GW
    nbs = 2 if b % 2 == 0 else 1
    wspec = lambda shape: pl.BlockSpec((None,) + shape, lambda bi, s: (layer,) + (0,) * len(shape))
    return pl.pallas_call(
        functools.partial(_gdn_kernel, tt=tt, tc=tc, nbs=nbs),
        grid=(b // nbs, t // tt),
        in_specs=[
            pl.BlockSpec((nbs, tt, w3), lambda bi, s: (bi, s, C_GQKV // w3)),
            pl.BlockSpec((nbs, tt, GW), lambda bi, s: (bi, s, C_GZ // GW)),
            pl.BlockSpec((nbs, tt, LANES), lambda bi, s: (bi, s, C_SMALL // LANES)),
            pl.BlockSpec((nbs, CONV_W - 1, w3), lambda bi, s: (bi, 0, 0)),
            pl.BlockSpec((nbs, GDN_H, GDN_D, GDN_D), lambda bi, s: (bi, 0, 0, 0)),
            wspec((CONV_W, w3)), wspec((1, LANES)), wspec((1, LANES)), wspec((1, GDN_D)),
        ],
        out_specs=[
            pl.BlockSpec((nbs, tt, GW), lambda bi, s: (bi, s, 0)),
            pl.BlockSpec((nbs, GDN_H, GDN_D, GDN_D), lambda bi, s: (bi, 0, 0, 0)),
            pl.BlockSpec((nbs, CONV_W - 1, w3), lambda bi, s: (bi, 0, 0)),
        ],
        out_shape=[
            jax.ShapeDtypeStruct((b, t, GW), F32),
            jax.ShapeDtypeStruct((b, GDN_H, GDN_D, GDN_D), F32),
            jax.ShapeDtypeStruct((b, CONV_W - 1, w3), F32),
        ],
        scratch_shapes=[
            pltpu.VMEM((nbs, CONV_PAD + tc, w3), F32),
            pltpu.VMEM((nbs, tc, LANES), F32),
            pltpu.VMEM((nbs, GDN_H, GDN_D, GDN_D), F32),
        ],
        compiler_params=_cparams(("parallel", "arbitrary")),
        name="gated_deltanet",
    )(h_arr, h_arr, h_arr, buf, s0, conv_w, a_log_row, dt_bias_row, norm_g)


def _outproj_kernel(oa_ref, ob_ref, oc_ref, od_ref, x_ref, w_ref, gg_ref, g_ref, b_ref, o_ref, *, alpha):
    def rms(v, gain):
        return (v * lax.rsqrt(jnp.mean(v * v, axis=-1, keepdims=True) + RMS_EPS) * gain).astype(BF16)

    acc = _dot(rms(oa_ref[...], gg_ref[0:1, :]), w_ref[0:GW, :])
    acc += _dot(rms(ob_ref[...], gg_ref[1:2, :]), w_ref[GW:2 * GW, :])
    acc += _dot(rms(oc_ref[...], gg_ref[2:3, :]), w_ref[2 * GW:3 * GW, :])
    acc += _dot(od_ref[...].astype(BF16), w_ref[3 * GW:4 * GW, :])
    o_ref[...] = _layer_norm(alpha * x_ref[...] + acc, g_ref[...], b_ref[...])


def _outproj_ln(oa, ob, oc, od, x, w_out, grp_g, ln_g, ln_b, *, layer, alpha):
    n, d = x.shape
    tm = _pick(n, 512)
    mix_spec = pl.BlockSpec((tm, GW), lambda i: (i, 0))
    return pl.pallas_call(
        functools.partial(_outproj_kernel, alpha=alpha),
        grid=(n // tm,),
        in_specs=[
            mix_spec, mix_spec, mix_spec, mix_spec,
            pl.BlockSpec((tm, d), lambda i: (i, 0)),
            pl.BlockSpec((None, 4 * GW, d), lambda i: (layer, 0, 0)),
            pl.BlockSpec((None, 3, GW), lambda i: (layer, 0, 0)),
            pl.BlockSpec((None, None, 1, d), lambda i: (layer, 1, 0, 0)),
            pl.BlockSpec((None, None, 1, d), lambda i: (layer, 1, 0, 0)),
        ],
        out_specs=pl.BlockSpec((tm, d), lambda i: (i, 0)),
        out_shape=jax.ShapeDtypeStruct((n, d), F32),
        compiler_params=_cparams(("parallel",)),
        name="outproj_ln",
    )(oa, ob, oc, od, x, w_out, grp_g, ln_g, ln_b)


def _prep_weights(ffn_gu, ffn_down, w_in, w_out, nsa_phi, lru_gate_w, lru_gate_b, gdn_A_log, gdn_dt_bias):
    depth, d_model, _ = w_in.shape
    o_nsakv, o_gate, o_lrux, o_gqkv, o_gz, o_gb = 2048, 2816, 2840, 3864, 5400, 5912
    zeros = jnp.zeros((depth, d_model, P_IN_PAD - 5920), w_in.dtype)
    w_in_p = jnp.concatenate([
        w_in[..., 512:1536],
        w_in[..., 0:512],
        w_in[..., 1536:2048],
        w_in[..., o_nsakv:o_nsakv + 512],
        w_in[..., o_lrux:o_lrux + 1024],
        w_in[..., o_gz:o_gz + 512],
        w_in[..., o_nsakv + 512:o_gate],
        w_in[..., o_gate:o_lrux],
        w_in[..., o_gb:o_gb + 8],
        zeros,
        w_in[..., o_gqkv:o_gz],
    ], axis=-1).astype(BF16)
    phi_b = jnp.transpose(nsa_phi, (0, 2, 1, 3, 4)).astype(BF16)
    col_blocks = []
    for c in range(2):
        for g in range(NSA_G):
            hot = jnp.zeros((2, NSA_G, 1, 1), BF16).at[c, g].set(1)
            col_blocks.append(phi_b[:, :, :, None, :, :] * hot[None, None])
    phi_big = jnp.concatenate(col_blocks, axis=-1).reshape(depth, CMP_BLOCK * 2 * NSA_G * HD, 2 * NSA_G * HD)
    eye_b = jnp.eye(LRU_BLKS, dtype=F32)
    wbd = jnp.einsum("lknce,nm->lnckme", lru_gate_w, eye_b).reshape(depth, GW, 2 * GW).astype(BF16)
    gate_b = lru_gate_b.reshape(depth, 1, 2 * GW)
    pad_l = jnp.zeros((depth, SM_DECAY), F32)
    pad_r = jnp.zeros((depth, LANES - SM_DECAY - GDN_H), F32)
    a_log_row = jnp.concatenate([pad_l, gdn_A_log, pad_r], axis=1)[:, None, :]
    dt_bias_row = jnp.concatenate([pad_l, gdn_dt_bias, pad_r], axis=1)[:, None, :]
    return dict(
        wgu=ffn_gu, wd=ffn_down, w_in=w_in_p, w_out=w_out.astype(BF16),
        phi_big=phi_big, wbd=wbd, gate_b=gate_b, a_log_row=a_log_row, dt_bias_row=dt_bias_row)


def _trunk_layer(x, b, t, q0, past, wts, layer, alpha, ffn_bf16=None):
    ln_g, ln_b = wts["ln_g"], wts["ln_b"]
    emit = ffn_bf16 is None
    ffn_w = [(wts["wgu"], wts["wd"])] * 2 if emit else ffn_bf16
    x = _ffn_ln(x, ffn_w[0], ln_g, ln_b, layer=layer, which=0, ln_idx=0, alpha=alpha, emit_bf16=emit)
    if emit:
        x, emitted0 = x
    assert C_SBK == 0
    h, sb_rows = _matmul(x, wts["w_in"], layer=layer, lead_cols=2 * GW, name="in_proj")
    h3 = h.reshape(b, t, P_IN_PAD)

    nsa_rows = h3[:, :, C_NSAKV:C_NSAKV + GW]
    nsa_win_new = h3[:, :, C_NSAW:C_NSAW + 2 * LANES]

    if past["page_table"] is None:
        lw = 0
        o_a = _sb_attend(h3, C_SBQ // GW, h3, C_SBK // GW, h3, C_SBV // GW, q0=q0)
        cmp_rows, cmp_blk = h3.reshape(b, t // CMP_BLOCK, CMP_BLOCK, P_IN_PAD), C_NSAKV // (2 * LANES)
        win_all, win_blk = h3, C_NSAW // (2 * LANES)
        paged = None
    else:
        pt = past["page_table"]
        lw = past["nsa_win"].shape[2]
        o_a = _sb_attend_paged(h3, C_SBQ // GW, C_SBK // (2 * GW), past["sb_kv"], pt, layer=layer)
        cmp_rows = _gather_nsa_pages(past["nsa_kv"], pt, h3, C_NSAKV // (2 * LANES), layer=layer)
        cmp_blk = 0
        win_all, win_blk = jnp.concatenate([past["nsa_win"][layer], nsa_win_new], axis=1), 0
        paged = (past["nsa_kv"], pt, layer)
    ks_blk = (C_NSAKV + 2 * LANES) // (2 * LANES)

    kc = _nsa_compress(cmp_rows, cmp_blk, wts["phi_big"], layer=layer)
    nb_arr = kc.shape[1]
    nbp = -(-nb_arr // LANES) * LANES
    kc = jnp.pad(kc, ((0, 0), (0, nbp - nb_arr), (0, 0)))
    o_b = _nsa_attend(h3, C_NSAQ // GW, h3, C_SMALL // LANES, kc, h3, ks_blk, win_all, win_blk,
                      q0=q0, lw=lw, n_real_keys=q0 + t, paged=paged)
    n_win = min(WINDOW, lw + t)
    new_win = win_all[:, win_all.shape[1] - n_win:, win_blk * 2 * LANES:(win_blk + 1) * 2 * LANES]

    o_c, h_last, new_lru_buf = _rg_lru(h3, past["lru_conv"], past["lru_h"], wts["lru_conv_w"], wts["lru_conv_b"],
                                       wts["wbd"], wts["gate_b"], wts["lru_lambda"], layer=layer)
    o_d, s_new, new_gdn_buf = _gated_deltanet(h3, past["gdn_conv"], past["gdn_S"], wts["gdn_conv_w"],
                                              wts["a_log_row"], wts["dt_bias_row"], wts["gdn_norm_g"], layer=layer)

    n = b * t
    x = _outproj_ln(o_a.reshape(n, GW), o_b.reshape(n, GW), o_c.reshape(n, GW), o_d.reshape(n, GW), x,
                    wts["w_out"], wts["grp_norm_g"], ln_g, ln_b, layer=layer, alpha=alpha)
    x = _ffn_ln(x, ffn_w[1], ln_g, ln_b, layer=layer, which=1, ln_idx=2, alpha=alpha, emit_bf16=emit)
    if emit:
        x, emitted1 = x
    new_state = (
        sb_rows.reshape(b, t, 2, SB_H, HD),
        nsa_rows.reshape(b, t, 4, NSA_G, HD),
        new_win.reshape(b, new_win.shape[1], 2, NSA_G, HD),
        h_last.reshape(b, GW),
        new_lru_buf,
        s_new,
        new_gdn_buf,
    )
    return x, new_state, ([emitted0, emitted1] if emit else None)


def kernel(x_prompt, x_sample, cache_sb_kv, cache_nsa_kv, cache_nsa_win, state_lru_h, state_lru_conv,
           state_gdn_S, state_gdn_conv, page_table, ln_g, ln_b, ffn_gu, ffn_down, w_in, w_out, grp_norm_g,
           nsa_phi, lru_conv_w, lru_conv_b, lru_gate_w, lru_gate_b, lru_lambda, gdn_conv_w, gdn_A_log,
           gdn_dt_bias, gdn_norm_g):
    depth, d_model = w_in.shape[0], w_in.shape[1]
    alpha = (2 * depth) ** 0.25
    n_b, seq = x_prompt.shape[:2]
    n_db, dec_seq = x_sample.shape[:2]
    past_len = page_table.shape[1] * PAGE
    n_pool = cache_sb_kv.shape[1]

    wts = _prep_weights(ffn_gu, ffn_down, w_in, w_out, nsa_phi, lru_gate_w, lru_gate_b, gdn_A_log, gdn_dt_bias)
    wts.update(
        ln_g=ln_g.reshape(depth, 3, 1, d_model), ln_b=ln_b.reshape(depth, 3, 1, d_model),
        grp_norm_g=grp_norm_g, lru_conv_w=lru_conv_w, lru_conv_b=lru_conv_b.reshape(depth, 1, GW),
        lru_lambda=lru_lambda.reshape(depth, 1, GW), gdn_conv_w=gdn_conv_w,
        gdn_norm_g=gdn_norm_g.reshape(depth, 1, GDN_D))

    sb_cache = jnp.transpose(cache_sb_kv, (0, 1, 3, 4, 5, 2))
    nsa_cache = jnp.transpose(cache_nsa_kv, (0, 1, 3, 4, 5, 2))
    nsa_win = cache_nsa_win.reshape(depth, n_db, cache_nsa_win.shape[2], 2 * LANES)

    y_p = x_prompt.reshape(n_b * seq, d_model)
    y_s = x_sample.reshape(n_db * dec_seq, d_model)
    st_p, st_s = [], []
    for l in range(depth):
        past_p = dict(page_table=None,
                      lru_h=jnp.zeros((n_b, 1, GW), F32), lru_conv=jnp.zeros((n_b, CONV_W - 1, GW), F32),
                      gdn_S=jnp.zeros((n_b, GDN_H, GDN_D, GDN_D), F32),
                      gdn_conv=jnp.zeros((n_b, CONV_W - 1, 3 * GW), F32))
        past_s = dict(page_table=page_table, sb_kv=sb_cache, nsa_kv=nsa_cache, nsa_win=nsa_win,
                      lru_h=state_lru_h[l].reshape(n_db, 1, GW), lru_conv=state_lru_conv[l],
                      gdn_S=state_gdn_S[l], gdn_conv=state_gdn_conv[l])
        y_s, new_s, ffn_bf16 = _trunk_layer(y_s, n_db, dec_seq, past_len, past_s, wts, l, alpha)
        y_p, new_p, _ = _trunk_layer(y_p, n_b, seq, 0, past_p, wts, l, alpha, ffn_bf16=ffn_bf16)
        st_p.append(new_p)
        st_s.append(new_s)
    p = [jnp.stack(a) for a in zip(*st_p)]
    s = [jnp.stack(a) for a in zip(*st_s)]
    return (y_p.reshape(n_b, seq, d_model), y_s.reshape(n_db, dec_seq, d_model),
            p[0], p[1], p[2], p[3], p[4], p[5], p[6], s[0], s[1], s[2], s[3], s[4], s[5], s[6])
```
